```python
import math
import jax, jax.numpy as jnp
from jax import lax
import numpy as np

D_MODEL = 2048
BATCH = 2
SEQ = 4096
DEPTH = 4

N_META = 16
HEAD_DIM = 64
ATTN_WIDTH = D_MODEL // 2
N_HEADS = ATTN_WIDTH // HEAD_DIM
N_KV_HEADS = N_HEADS // 4
KV_GROUP = N_HEADS // N_KV_HEADS
KV_WIDTH = N_KV_HEADS * HEAD_DIM
SSM_WIDTH = D_MODEL - ATTN_WIDTH
SSM_GROUP_CH = 16
SSM_GROUPS = SSM_WIDTH // SSM_GROUP_CH
SSM_STATE = 64
WINDOW = 128
BLOCK = 128
PAD = BLOCK - N_META
D_FF = 4 * D_MODEL
IN_WIDTH = ATTN_WIDTH + 2 * KV_WIDTH + SSM_WIDTH
NORM_EPS = 1e-6
NEG_INF = -1e30
STEP_MIN = 1e-3
STEP_MAX = 1e-1

kernel_name = "hymba_s5_swa_alibi_trunk"


def _rmsnorm(x, g):
    xf = x.astype(jnp.float32)
    y = xf * lax.rsqrt(jnp.mean(xf * xf, axis=-1, keepdims=True) + NORM_EPS)
    return (y * g.astype(jnp.float32)).astype(x.dtype)


def _alibi_slopes():
    h = jnp.arange(1, N_HEADS + 1, dtype=jnp.float32)
    return jnp.exp2(-8.0 * h / N_HEADS)


def _sliding_window_attention(q, k, v, sinks):
    b, L = q.shape[0], q.shape[1]
    dtype = q.dtype
    Lp = L + PAD
    nb = Lp // BLOCK
    qf = q.astype(jnp.float32)
    kf = k.astype(jnp.float32)
    vf = v.astype(jnp.float32)
    pad4 = ((0, 0), (PAD, 0), (0, 0), (0, 0))
    qp = jnp.pad(qf, pad4).reshape(b, nb, BLOCK, N_KV_HEADS, KV_GROUP, HEAD_DIM)
    kp = jnp.pad(kf, pad4).reshape(b, nb, BLOCK, N_KV_HEADS, HEAD_DIM)
    vp = jnp.pad(vf, pad4).reshape(b, nb, BLOCK, N_KV_HEADS, HEAD_DIM)
    pad5 = ((0, 0), (1, 0), (0, 0), (0, 0), (0, 0))
    k_band = jnp.concatenate([jnp.pad(kp, pad5)[:, :-1], kp], axis=2)
    v_band = jnp.concatenate([jnp.pad(vp, pad5)[:, :-1], vp], axis=2)
    k_meta = kf[:, :N_META]
    v_meta = vf[:, :N_META]

    n_idx = jnp.arange(nb)[:, None, None]
    i_idx = jnp.arange(BLOCK)[None, :, None]
    j_idx = jnp.arange(2 * BLOCK)[None, None, :]
    t_pos = n_idx * BLOCK + i_idx - PAD
    s_pos = (n_idx - 1) * BLOCK + j_idx - PAD
    band_mask = (s_pos >= N_META) & (s_pos <= t_pos) & (t_pos - s_pos < WINDOW)
    band_dist = jnp.abs(t_pos - s_pos).astype(jnp.float32)
    m_pos = jnp.arange(N_META)[None, None, :]
    meta_mask = m_pos <= t_pos
    meta_dist = jnp.abs(t_pos - m_pos).astype(jnp.float32)

    slopes = _alibi_slopes().reshape(N_KV_HEADS, KV_GROUP, 1, 1)
    scale = 1.0 / math.sqrt(HEAD_DIM)
    s_band = jnp.einsum('bnqkgd,bnskd->bnkgqs', qp, k_band) * scale
    s_band = s_band - slopes * band_dist[:, None, None]
    s_band = jnp.where(band_mask[:, None, None], s_band, NEG_INF)
    s_meta = jnp.einsum('bnqkgd,bmkd->bnkgqm', qp, k_meta) * scale
    s_meta = s_meta - slopes * meta_dist[:, None, None]
    s_meta = jnp.where(meta_mask[:, None, None], s_meta, NEG_INF)
    sink = jnp.broadcast_to(sinks.astype(jnp.float32).reshape(N_KV_HEADS, KV_GROUP, 1, 1),
                            s_band.shape[:-1] + (1,))
    probs = jax.nn.softmax(jnp.concatenate([s_band, s_meta, sink], axis=-1), axis=-1)
    p_band = probs[..., :2 * BLOCK]
    p_meta = probs[..., 2 * BLOCK:2 * BLOCK + N_META]
    out = (jnp.einsum('bnkgqs,bnskd->bnqkgd', p_band, v_band)
           + jnp.einsum('bnkgqm,bmkd->bnqkgd', p_meta, v_meta))
    out = out.reshape(b, Lp, ATTN_WIDTH)[:, PAD:]
    return out.astype(dtype)


def _ssm_combine(e_i, e_j):
    a_i, b_i = e_i
    a_j, b_j = e_j
    return a_j * a_i, a_j * b_i + b_j


def _s5_mixer(u, lam_re, lam_im, log_step, b_re, b_im, c_re, c_im, d, w_glu, b_glu):
    dtype = u.dtype
    b, L = u.shape[0], u.shape[1]
    ul = jnp.moveaxis(u.astype(jnp.float32).reshape(b, L, SSM_GROUPS, SSM_GROUP_CH), 1, 0)
    lam = lax.complex(lam_re.astype(jnp.float32), lam_im.astype(jnp.float32))
    delta = jnp.exp(log_step.astype(jnp.float32))[:, None]
    lam_bar = jnp.exp(lam * delta)
    b_c = lax.complex(b_re.astype(jnp.float32), b_im.astype(jnp.float32))
    b_bar = ((lam_bar - 1.0) / lam)[..., None] * b_c
    c_c = lax.complex(c_re.astype(jnp.float32), c_im.astype(jnp.float32))
    bu = jnp.einsum('lbgh,gph->lbgp', ul.astype(jnp.complex64), b_bar)
    a = jnp.broadcast_to(lam_bar, (L, 1, SSM_GROUPS, SSM_STATE))
    _, states = lax.associative_scan(_ssm_combine, (a, bu), axis=0)
    y = jnp.real(jnp.einsum('lbgp,ghp->lbgh', states, c_c))
    y = y + d.astype(jnp.float32).reshape(SSM_GROUPS, SSM_GROUP_CH) * ul
    y = jnp.moveaxis(y, 0, 1).reshape(b, L, SSM_WIDTH)
    g = jax.nn.gelu(y)
    out = g * jax.nn.sigmoid(g @ w_glu.astype(jnp.float32) + b_glu.astype(jnp.float32))
    return out.astype(dtype)


def setup_inputs(seed: int = 0) -> dict:
    key = jax.random.key(seed)
    ks = jax.random.split(key, 24)
    f32 = jnp.float32
    nrm = lambda k, shape, s: jax.random.normal(k, shape, f32) * s
    x = jax.random.normal(ks[0], (BATCH, SEQ, D_MODEL), f32)
    meta_tokens = nrm(ks[1], (N_META, D_MODEL), 1.0)
    norm_mix_g = 1.0 + nrm(ks[2], (DEPTH, D_MODEL), 0.02)
    w_in = nrm(ks[3], (DEPTH, D_MODEL, IN_WIDTH), D_MODEL ** -0.5)
    q_norm_g = 1.0 + nrm(ks[4], (DEPTH, HEAD_DIM), 0.02)
    k_norm_g = 1.0 + nrm(ks[5], (DEPTH, HEAD_DIM), 0.02)
    attn_sinks = nrm(ks[6], (DEPTH, N_HEADS), 0.5)
    n = jnp.arange(SSM_STATE, dtype=f32)
    ssm_lambda_re = -0.5 + nrm(ks[7], (DEPTH, SSM_GROUPS, SSM_STATE), 1e-3)
    ssm_lambda_im = math.pi * n + nrm(ks[8], (DEPTH, SSM_GROUPS, SSM_STATE), 1e-3)
    ssm_log_step = jax.random.uniform(ks[9], (DEPTH, SSM_GROUPS), f32,
                                      math.log(STEP_MIN), math.log(STEP_MAX))
    bs = (SSM_GROUP_CH ** -0.5) / math.sqrt(2.0)
    cs = (SSM_STATE ** -0.5) / math.sqrt(2.0)
    ssm_b_re = nrm(ks[10], (DEPTH, SSM_GROUPS, SSM_STATE, SSM_GROUP_CH), bs)
    ssm_b_im = nrm(ks[11], (DEPTH, SSM_GROUPS, SSM_STATE, SSM_GROUP_CH), bs)
    ssm_c_re = nrm(ks[12], (DEPTH, SSM_GROUPS, SSM_GROUP_CH, SSM_STATE), cs)
    ssm_c_im = nrm(ks[13], (DEPTH, SSM_GROUPS, SSM_GROUP_CH, SSM_STATE), cs)
    ssm_d = nrm(ks[14], (DEPTH, SSM_WIDTH), 1.0)
    w_glu = nrm(ks[15], (DEPTH, SSM_WIDTH, SSM_WIDTH), SSM_WIDTH ** -0.5)
    b_glu = nrm(ks[16], (DEPTH, SSM_WIDTH), 0.01)
    attn_out_g = 1.0 + nrm(ks[17], (DEPTH, ATTN_WIDTH), 0.02)
    ssm_out_g = 1.0 + nrm(ks[18], (DEPTH, SSM_WIDTH), 0.02)
    w_out = nrm(ks[19], (DEPTH, D_MODEL, D_MODEL), D_MODEL ** -0.5)
    norm_mlp_g = 1.0 + nrm(ks[20], (DEPTH, D_MODEL), 0.02)
    w_up = nrm(ks[21], (DEPTH, D_MODEL, D_FF), D_MODEL ** -0.5)
    w_down = nrm(ks[22], (DEPTH, D_FF, D_MODEL), D_FF ** -0.5)
    return {"x": x, "meta_tokens": meta_tokens, "norm_mix_g": norm_mix_g, "w_in": w_in,
            "q_norm_g": q_norm_g, "k_norm_g": k_norm_g, "attn_sinks": attn_sinks,
            "ssm_lambda_re": ssm_lambda_re, "ssm_lambda_im": ssm_lambda_im,
            "ssm_log_step": ssm_log_step, "ssm_b_re": ssm_b_re, "ssm_b_im": ssm_b_im,
            "ssm_c_re": ssm_c_re, "ssm_c_im": ssm_c_im, "ssm_d": ssm_d,
            "w_glu": w_glu, "b_glu": b_glu, "attn_out_g": attn_out_g, "ssm_out_g": ssm_out_g,
            "w_out": w_out, "norm_mlp_g": norm_mlp_g, "w_up": w_up, "w_down": w_down}


def reference(x, meta_tokens, norm_mix_g, w_in, q_norm_g, k_norm_g, attn_sinks,
              ssm_lambda_re, ssm_lambda_im, ssm_log_step, ssm_b_re, ssm_b_im,
              ssm_c_re, ssm_c_im, ssm_d, w_glu, b_glu, attn_out_g, ssm_out_g,
              w_out, norm_mlp_g, w_up, w_down):
    b = x.shape[0]
    meta = jnp.broadcast_to(meta_tokens.astype(x.dtype)[None], (b, N_META, D_MODEL))
    h_res = jnp.concatenate([meta, x], axis=1)
    L = h_res.shape[1]
    for l in range(DEPTH):
        h = _rmsnorm(h_res, norm_mix_g[l])
        proj = h @ w_in[l]
        q = proj[..., :ATTN_WIDTH].reshape(b, L, N_HEADS, HEAD_DIM)
        k = proj[..., ATTN_WIDTH:ATTN_WIDTH + KV_WIDTH].reshape(b, L, N_KV_HEADS, HEAD_DIM)
        v = proj[..., ATTN_WIDTH + KV_WIDTH:ATTN_WIDTH + 2 * KV_WIDTH].reshape(b, L, N_KV_HEADS, HEAD_DIM)
        u = proj[..., ATTN_WIDTH + 2 * KV_WIDTH:]
        q = _rmsnorm(q, q_norm_g[l])
        k = _rmsnorm(k, k_norm_g[l])
        attn = _sliding_window_attention(q, k, v, attn_sinks[l])
        ssm = _s5_mixer(u, ssm_lambda_re[l], ssm_lambda_im[l], ssm_log_step[l],
                        ssm_b_re[l], ssm_b_im[l], ssm_c_re[l], ssm_c_im[l], ssm_d[l],
                        w_glu[l], b_glu[l])
        mix = jnp.concatenate([_rmsnorm(attn, attn_out_g[l]), _rmsnorm(ssm, ssm_out_g[l])], axis=-1)
        h_res = h_res + mix @ w_out[l]
        h2 = _rmsnorm(h_res, norm_mlp_g[l])
        h_res = h_res + jnp.square(jax.nn.relu(h2 @ w_up[l])) @ w_down[l]
    return h_res[:, N_META:]
```

```python
import functools
import math

import jax
import jax.numpy as jnp
from jax import lax
from jax.experimental import pallas as pl
from jax.experimental.pallas import tpu as pltpu

D_MODEL = 2048
N_META = 16
HEAD_DIM = 64
ATTN_WIDTH = D_MODEL // 2
N_HEADS = ATTN_WIDTH // HEAD_DIM
N_KV_HEADS = N_HEADS // 4
KV_GROUP = N_HEADS // N_KV_HEADS
KV_WIDTH = N_KV_HEADS * HEAD_DIM
SSM_WIDTH = D_MODEL - ATTN_WIDTH
SSM_GROUP_CH = 16
SSM_GROUPS = SSM_WIDTH // SSM_GROUP_CH
SSM_STATE = 64
WINDOW = 128
BLOCK = 128
D_FF = 4 * D_MODEL
IN_WIDTH = ATTN_WIDTH + 2 * KV_WIDTH + SSM_WIDTH
NORM_EPS = 1e-6
NEG_INF = -1e30

CHUNK = 16
CHUNK_W = CHUNK * SSM_GROUP_CH
GROUPS_PER_STEP = 8
PAIRS_PER_STEP = GROUPS_PER_STEP // 2

COL_Q = 0
COL_U = ATTN_WIDTH
COL_K = ATTN_WIDTH + SSM_WIDTH
COL_V = COL_K + KV_WIDTH

VMEM_LIMIT = 48 * 1024 * 1024

F32 = jnp.float32
BF16 = jnp.bfloat16


def _row_tile(rows, target):
    k = rows // BLOCK
    best = 1
    for d in range(1, k + 1):
        if k % d == 0 and d * BLOCK <= target:
            best = d
    return best * BLOCK


def _rms(x, gain):
    ms = jnp.mean(x * x, axis=-1, keepdims=True)
    return x * lax.rsqrt(ms + NORM_EPS) * gain


def _inproj_kernel(x_ref, g_ref, w_ref, o_ref):
    h = _rms(x_ref[...], g_ref[...]).astype(BF16)
    o_ref[...] = jnp.dot(h, w_ref[...], preferred_element_type=F32).astype(o_ref.dtype)


def _inproj(x, gain, w):
    rows = x.shape[0]
    tm = _row_tile(rows, 384)
    return pl.pallas_call(
        _inproj_kernel,
        grid=(rows // tm,),
        in_specs=[
            pl.BlockSpec((tm, D_MODEL), lambda i: (i, 0)),
            pl.BlockSpec((1, D_MODEL), lambda i: (0, 0)),
            pl.BlockSpec((D_MODEL, IN_WIDTH), lambda i: (0, 0)),
        ],
        out_specs=pl.BlockSpec((tm, IN_WIDTH), lambda i: (i, 0)),
        out_shape=jax.ShapeDtypeStruct((rows, IN_WIDTH), BF16),
        compiler_params=pltpu.CompilerParams(
            dimension_semantics=("parallel",), vmem_limit_bytes=VMEM_LIMIT),
        name="inproj",
    )(x, gain, w)


def _attn_kernel(sink_ref, q_ref, kp_ref, kc_ref, vp_ref, vc_ref, km_ref, vm_ref,
                 qg_ref, kg_ref, og_ref, o_ref):
    n = pl.program_id(1)
    nkeys = 2 * BLOCK + N_META
    i_idx = lax.broadcasted_iota(jnp.int32, (BLOCK, nkeys), 0)
    j_idx = lax.broadcasted_iota(jnp.int32, (BLOCK, nkeys), 1)
    t_pos = n * BLOCK + i_idx
    is_meta = j_idx >= 2 * BLOCK
    s_pos = jnp.where(is_meta, j_idx - 2 * BLOCK, (n - 1) * BLOCK + j_idx)
    dist = t_pos - s_pos
    min_pos = jnp.where(is_meta, 0, N_META)
    max_dist = jnp.where(is_meta, jnp.iinfo(jnp.int32).max, WINDOW)
    visible = (s_pos >= min_pos) & (dist >= 0) & (dist < max_dist)
    dist_f = jnp.abs(dist).astype(F32)

    q_gain = qg_ref[...] * (1.0 / math.sqrt(HEAD_DIM))
    k_gain = kg_ref[...]
    outs = []
    for kh in range(N_KV_HEADS):
        ks = slice(kh * HEAD_DIM, (kh + 1) * HEAD_DIM)
        k_all = jnp.concatenate([kp_ref[:, ks], kc_ref[:, ks], km_ref[:, ks]], axis=0)
        v_all = jnp.concatenate([vp_ref[:, ks], vc_ref[:, ks], vm_ref[:, ks]], axis=0)
        k_n = _rms(k_all.astype(F32), k_gain).astype(BF16)
        for g in range(KV_GROUP):
            h = kh * KV_GROUP + g
            slope = 2.0 ** (-8.0 * (h + 1) / N_HEADS)
            q_h = q_ref[:, h * HEAD_DIM:(h + 1) * HEAD_DIM].astype(F32)
            q_n = _rms(q_h, q_gain).astype(BF16)
            s = lax.dot_general(q_n, k_n, (((1,), (1,)), ((), ())),
                                preferred_element_type=F32)
            s = jnp.where(visible, s - slope * dist_f, NEG_INF)
            sink = sink_ref[h]
            m = jnp.maximum(jnp.max(s, axis=-1, keepdims=True), sink)
            p = jnp.exp(s - m)
            denom = jnp.sum(p, axis=-1, keepdims=True) + jnp.exp(sink - m)
            o = jnp.dot(p.astype(BF16), v_all, preferred_element_type=F32)
            outs.append(o / denom)
    attn = jnp.concatenate(outs, axis=-1)
    o_ref[...] = _rms(attn, og_ref[...]).astype(o_ref.dtype)


def _attention(proj3, sinks, q_gain, k_gain, out_gain):
    b, lp, _ = proj3.shape
    nb = lp // BLOCK
    kcol, vcol = COL_K // KV_WIDTH, COL_V // KV_WIDTH
    prev = lambda bi, n: jnp.maximum(n - 1, 0)
    return pl.pallas_call(
        _attn_kernel,
        grid=(b, nb),
        in_specs=[
            pl.BlockSpec(memory_space=pltpu.SMEM),
            pl.BlockSpec((None, BLOCK, ATTN_WIDTH), lambda bi, n: (bi, n, 0)),
            pl.BlockSpec((None, BLOCK, KV_WIDTH), lambda bi, n: (bi, prev(bi, n), kcol)),
            pl.BlockSpec((None, BLOCK, KV_WIDTH), lambda bi, n: (bi, n, kcol)),
            pl.BlockSpec((None, BLOCK, KV_WIDTH), lambda bi, n: (bi, prev(bi, n), vcol)),
            pl.BlockSpec((None, BLOCK, KV_WIDTH), lambda bi, n: (bi, n, vcol)),
            pl.BlockSpec((None, N_META, KV_WIDTH), lambda bi, n: (bi, 0, kcol)),
            pl.BlockSpec((None, N_META, KV_WIDTH), lambda bi, n: (bi, 0, vcol)),
            pl.BlockSpec((1, HEAD_DIM), lambda bi, n: (0, 0)),
            pl.BlockSpec((1, HEAD_DIM), lambda bi, n: (0, 0)),
            pl.BlockSpec((1, ATTN_WIDTH), lambda bi, n: (0, 0)),
        ],
        out_specs=pl.BlockSpec((None, BLOCK, ATTN_WIDTH), lambda bi, n: (bi, n, 0)),
        out_shape=jax.ShapeDtypeStruct((b, lp, ATTN_WIDTH), BF16),
        compiler_params=pltpu.CompilerParams(
            dimension_semantics=("parallel", "parallel"), vmem_limit_bytes=VMEM_LIMIT),
        name="attention",
    )(sinks, proj3, proj3, proj3, proj3, proj3, proj3, proj3, q_gain, k_gain, out_gain)


def _gelu_tanh(x):
    c = math.sqrt(2.0 / math.pi)
    return 0.5 * x * (1.0 + jnp.tanh(c * (x + 0.044715 * (x * x * x))))


def _ssm_kernel(u_ref, m_ref, win_ref, wc_ref, are_ref, aim_ref, d_ref, o_ref,
                inre, inim, sre, sim, *, n_chunks, n_batch):
    pw = 2 * SSM_STATE
    for p in range(PAIRS_PER_STEP):
        u_pair = u_ref[:, p * 2 * CHUNK_W:(p + 1) * 2 * CHUNK_W]
        z = jnp.dot(u_pair, win_ref[p], preferred_element_type=F32)
        inre[:, p * pw:(p + 1) * pw] = z[:, :pw]
        inim[:, p * pw:(p + 1) * pw] = z[:, pw:]

    a_re = are_ref[...]
    a_im = aim_ref[...]

    def step(c, carry):
        nxt = []
        for b in range(n_batch):
            s_r, s_i = carry[b]
            row = b * n_chunks + c
            sre[pl.ds(row, 1), :] = s_r
            sim[pl.ds(row, 1), :] = s_i
            x_r = inre[pl.ds(row, 1), :]
            x_i = inim[pl.ds(row, 1), :]
            nxt.append((a_re * s_r - a_im * s_i + x_r, a_re * s_i + a_im * s_r + x_i))
        return tuple(nxt)

    zero = jnp.zeros((1, PAIRS_PER_STEP * pw), F32)
    lax.fori_loop(0, n_chunks, step, tuple((zero, zero) for _ in range(n_batch)))

    for p in range(PAIRS_PER_STEP):
        s_cat = jnp.concatenate(
            [sre[:, p * pw:(p + 1) * pw], sim[:, p * pw:(p + 1) * pw]], axis=-1).astype(BF16)
        y_carry = jnp.dot(s_cat, wc_ref[p], preferred_element_type=F32)
        for q in range(2):
            g = 2 * p + q
            cols = slice(g * CHUNK_W, (g + 1) * CHUNK_W)
            u_g = u_ref[:, cols]
            y = jnp.dot(u_g, m_ref[g], preferred_element_type=F32)
            y = y + y_carry[:, q * CHUNK_W:(q + 1) * CHUNK_W] + d_ref[:, cols] * u_g.astype(F32)
            o_ref[:, cols] = _gelu_tanh(y).astype(o_ref.dtype)


def _ssm(u_chunks, m, win, wc, a_re, a_im, d_chunk, n_batch):
    rc, width = u_chunks.shape
    n_chunks = rc // n_batch
    bw = GROUPS_PER_STEP * CHUNK_W
    sw = GROUPS_PER_STEP * SSM_STATE
    steps = SSM_GROUPS // GROUPS_PER_STEP
    kern = functools.partial(_ssm_kernel, n_chunks=n_chunks, n_batch=n_batch)
    return pl.pallas_call(
        kern,
        grid=(steps,),
        in_specs=[
            pl.BlockSpec((rc, bw), lambda j: (0, j)),
            pl.BlockSpec((GROUPS_PER_STEP, CHUNK_W, CHUNK_W), lambda j: (j, 0, 0)),
            pl.BlockSpec((PAIRS_PER_STEP, 2 * CHUNK_W, 4 * SSM_STATE), lambda j: (j, 0, 0)),
            pl.BlockSpec((PAIRS_PER_STEP, 4 * SSM_STATE, 2 * CHUNK_W), lambda j: (j, 0, 0)),
            pl.BlockSpec((1, sw), lambda j: (0, j)),
            pl.BlockSpec((1, sw), lambda j: (0, j)),
            pl.BlockSpec((1, bw), lambda j: (0, j)),
        ],
        out_specs=pl.BlockSpec((rc, bw), lambda j: (0, j)),
        out_shape=jax.ShapeDtypeStruct((rc, width), BF16),
        scratch_shapes=[pltpu.VMEM((rc, sw), F32) for _ in range(4)],
        compiler_params=pltpu.CompilerParams(
            dimension_semantics=("parallel",), vmem_limit_bytes=VMEM_LIMIT),
        name="ssm",
    )(u_chunks, m, win, wc, a_re, a_im, d_chunk)


def _ssm_weights(lam_re, lam_im, log_step, b_re, b_im, c_re, c_im, ssm_d):
    depth = lam_re.shape[0]
    g, p, h, t = SSM_GROUPS, SSM_STATE, SSM_GROUP_CH, CHUNK
    lam = lax.complex(lam_re.astype(F32), lam_im.astype(F32))
    delta = jnp.exp(log_step.astype(F32))[..., None]
    ld = lam * delta
    lam_bar = jnp.exp(ld)
    b_bar = ((lam_bar - 1.0) / lam)[..., None] * lax.complex(b_re.astype(F32), b_im.astype(F32))
    c_c = lax.complex(c_re.astype(F32), c_im.astype(F32))
    tau = jnp.arange(t + 1, dtype=F32)
    powers = jnp.exp(ld[:, :, None, :] * tau[None, None, :, None])

    kt = jnp.real(jnp.einsum('dghp,dgtp,dgpk->dgthk', c_c, powers[:, :, :t], b_bar))
    s_idx = jnp.arange(t)[:, None]
    t_idx = jnp.arange(t)[None, :]
    lag = t_idx - s_idx
    kk = kt[:, :, jnp.clip(lag, 0, t - 1)]
    kk = jnp.where((lag >= 0)[None, None, :, :, None, None], kk, 0.0)
    m = kk.transpose(0, 1, 2, 5, 3, 4).reshape(depth, g, t * h, t * h)

    win = jnp.einsum('dgtp,dgpk->dgtkp', powers[:, :, t - 1::-1], b_bar).reshape(depth, g, t * h, p)
    wr = jnp.real(win).reshape(depth, g // 2, 2, t * h, p)
    wi = jnp.imag(win).reshape(depth, g // 2, 2, t * h, p)
    z = jnp.zeros_like(wr[:, :, 0])
    win_pair = jnp.concatenate([
        jnp.concatenate([wr[:, :, 0], z, wi[:, :, 0], z], axis=-1),
        jnp.concatenate([z, wr[:, :, 1], z, wi[:, :, 1]], axis=-1)], axis=-2)

    e = jnp.einsum('dghp,dgtp->dgpth', c_c, powers[:, :, 1:]).reshape(depth, g, p, t * h)
    er = jnp.real(e).reshape(depth, g // 2, 2, p, t * h)
    ei = -jnp.imag(e).reshape(depth, g // 2, 2, p, t * h)
    z = jnp.zeros_like(er[:, :, 0])
    wc_pair = jnp.concatenate([
        jnp.concatenate([er[:, :, 0], z], axis=-1),
        jnp.concatenate([z, er[:, :, 1]], axis=-1),
        jnp.concatenate([ei[:, :, 0], z], axis=-1),
        jnp.concatenate([z, ei[:, :, 1]], axis=-1)], axis=-2)

    a_chunk = powers[:, :, t]
    a_re = jnp.real(a_chunk).reshape(depth, 1, g * p)
    a_im = jnp.imag(a_chunk).reshape(depth, 1, g * p)
    d_chunk = jnp.broadcast_to(ssm_d.astype(F32).reshape(depth, g, 1, h),
                               (depth, g, t, h)).reshape(depth, 1, g * t * h)
    return m.astype(BF16), win_pair.astype(BF16), wc_pair.astype(BF16), a_re, a_im, d_chunk


def _outproj_kernel(a_ref, g_ref, x_ref, wglu_ref, bglu_ref, sg_ref, wout_ref, ng_ref,
                    o_ref, xn_ref):
    g = g_ref[...]
    z = jnp.dot(g, wglu_ref[...], preferred_element_type=F32) + bglu_ref[...]
    s = g.astype(F32) * jax.nn.sigmoid(z)
    s_n = _rms(s, sg_ref[...]).astype(BF16)
    y = jnp.dot(a_ref[...], wout_ref[:ATTN_WIDTH, :], preferred_element_type=F32)
    y = y + jnp.dot(s_n, wout_ref[ATTN_WIDTH:, :], preferred_element_type=F32)
    h = x_ref[...] + y
    o_ref[...] = h
    xn_ref[...] = _rms(h, ng_ref[...]).astype(xn_ref.dtype)


def _outproj(attn_n, g_tok, x, w_glu, b_glu, ssm_gain, w_out, mlp_gain):
    rows = x.shape[0]
    tm = _row_tile(rows, 384)
    row = lambda i: (i, 0)
    fixed = lambda i: (0, 0)
    return pl.pallas_call(
        _outproj_kernel,
        grid=(rows // tm,),
        in_specs=[
            pl.BlockSpec((tm, ATTN_WIDTH), row),
            pl.BlockSpec((tm, SSM_WIDTH), row),
            pl.BlockSpec((tm, D_MODEL), row),
            pl.BlockSpec((SSM_WIDTH, SSM_WIDTH), fixed),
            pl.BlockSpec((1, SSM_WIDTH), fixed),
            pl.BlockSpec((1, SSM_WIDTH), fixed),
            pl.BlockSpec((D_MODEL, D_MODEL), fixed),
            pl.BlockSpec((1, D_MODEL), fixed),
        ],
        out_specs=[pl.BlockSpec((tm, D_MODEL), row), pl.BlockSpec((tm, D_MODEL), row)],
        out_shape=[jax.ShapeDtypeStruct((rows, D_MODEL), F32),
                   jax.ShapeDtypeStruct((rows, D_MODEL), BF16)],
        compiler_params=pltpu.CompilerParams(
            dimension_semantics=("parallel",), vmem_limit_bytes=VMEM_LIMIT),
        name="outproj",
    )(attn_n, g_tok, x, w_glu, b_glu, ssm_gain, w_out, mlp_gain)


def _mlp_kernel(xn_ref, x_ref, wu_ref, wd_ref, o_ref):
    f = pl.program_id(1)
    a = jnp.dot(xn_ref[...], wu_ref[...], preferred_element_type=F32)
    a = jnp.square(jnp.maximum(a, 0.0)).astype(BF16)
    y = jnp.dot(a, wd_ref[...], preferred_element_type=F32)

    @pl.when(f == 0)
    def _():
        o_ref[...] = x_ref[...] + y

    @pl.when(f > 0)
    def _():
        o_ref[...] += y


def _mlp(xn, x, w_up, w_down):
    rows = x.shape[0]
    tm = _row_tile(rows, 768)
    tf = 512
    return pl.pallas_call(
        _mlp_kernel,
        grid=(rows // tm, D_FF // tf),
        in_specs=[
            pl.BlockSpec((tm, D_MODEL), lambda i, f: (i, 0)),
            pl.BlockSpec((tm, D_MODEL), lambda i, f: (i, 0)),
            pl.BlockSpec((D_MODEL, tf), lambda i, f: (0, f)),
            pl.BlockSpec((tf, D_MODEL), lambda i, f: (f, 0)),
        ],
        out_specs=pl.BlockSpec((tm, D_MODEL), lambda i, f: (i, 0)),
        out_shape=jax.ShapeDtypeStruct((rows, D_MODEL), F32),
        compiler_params=pltpu.CompilerParams(
            dimension_semantics=("parallel", "arbitrary"), vmem_limit_bytes=VMEM_LIMIT),
        name="mlp",
    )(xn, x, w_up, w_down)


def kernel(x, meta_tokens, norm_mix_g, w_in, q_norm_g, k_norm_g, attn_sinks, ssm_lambda_re, ssm_lambda_im, ssm_log_step, ssm_b_re, ssm_b_im, ssm_c_re, ssm_c_im, ssm_d, w_glu, b_glu, attn_out_g, ssm_out_g, w_out, norm_mlp_g, w_up, w_down):
    b, seq, d = x.shape
    depth = w_in.shape[0]
    assert d == D_MODEL
    length = N_META + seq
    lp = -(-length // BLOCK) * BLOCK
    rows = b * lp
    n_chunks = lp // CHUNK

    meta = jnp.broadcast_to(meta_tokens.astype(x.dtype)[None], (b, N_META, d))
    pad = jnp.zeros((b, lp - length, d), x.dtype)
    h_res = jnp.concatenate([meta, x, pad], axis=1).reshape(rows, d)

    w_in_r = jnp.concatenate([
        w_in[..., :ATTN_WIDTH], w_in[..., ATTN_WIDTH + 2 * KV_WIDTH:],
        w_in[..., ATTN_WIDTH:ATTN_WIDTH + 2 * KV_WIDTH]], axis=-1).astype(BF16)
    w_glu_b = w_glu.astype(BF16)
    w_out_b = w_out.astype(BF16)
    w_up_b = w_up.astype(BF16)
    w_down_b = w_down.astype(BF16)
    m, win_pair, wc_pair, a_re, a_im, d_chunk = _ssm_weights(
        ssm_lambda_re, ssm_lambda_im, ssm_log_step, ssm_b_re, ssm_b_im, ssm_c_re, ssm_c_im, ssm_d)

    row_vec = lambda v: v.astype(F32).reshape(1, -1)
    for l in range(depth):
        proj = _inproj(h_res, row_vec(norm_mix_g[l]), w_in_r[l])
        attn_n = _attention(proj.reshape(b, lp, IN_WIDTH), attn_sinks[l].astype(F32),
                            row_vec(q_norm_g[l]), row_vec(k_norm_g[l]), row_vec(attn_out_g[l]))
        u_chunks = proj[:, COL_U:COL_U + SSM_WIDTH].reshape(
            b * n_chunks, CHUNK, SSM_GROUPS, SSM_GROUP_CH).transpose(0, 2, 1, 3).reshape(
            b * n_chunks, SSM_GROUPS * CHUNK_W)
        g_chunks = _ssm(u_chunks, m[l], win_pair[l], wc_pair[l], a_re[l], a_im[l], d_chunk[l], b)
        g_tok = g_chunks.reshape(b * n_chunks, SSM_GROUPS, CHUNK, SSM_GROUP_CH).transpose(
            0, 2, 1, 3).reshape(rows, SSM_WIDTH)
        h_res, xn = _outproj(attn_n.reshape(rows, ATTN_WIDTH), g_tok, h_res, w_glu_b[l],
                             row_vec(b_glu[l]), row_vec(ssm_out_g[l]), w_out_b[l],
                             row_vec(norm_mlp_g[l]))
        h_res = _mlp(xn, h_res, w_up_b[l], w_down_b[l])
    return h_res.reshape(b, lp, d)[:, N_META:length]
```

```python
import functools
import math

import jax
import jax.numpy as jnp
from jax import lax
from jax.experimental import pallas as pl
from jax.experimental.pallas import tpu as pltpu

D_MODEL = 2048
N_META = 16
HEAD_DIM = 64
ATTN_WIDTH = D_MODEL // 2
N_HEADS = ATTN_WIDTH // HEAD_DIM
N_KV_HEADS = N_HEADS // 4
KV_GROUP = N_HEADS // N_KV_HEADS
KV_WIDTH = N_KV_HEADS * HEAD_DIM
SSM_WIDTH = D_MODEL - ATTN_WIDTH
SSM_GROUP_CH = 16
SSM_GROUPS = SSM_WIDTH // SSM_GROUP_CH
SSM_STATE = 64
WINDOW = 128
BLOCK = 128
D_FF = 4 * D_MODEL
IN_WIDTH = ATTN_WIDTH + 2 * KV_WIDTH + SSM_WIDTH
NORM_EPS = 1e-6
NEG_INF = -1e30

CHUNK = 16
CHUNK_W = CHUNK * SSM_GROUP_CH
GROUPS_PER_STEP = 8
PAIRS_PER_STEP = GROUPS_PER_STEP // 2

QKV_WIDTH = ATTN_WIDTH + 2 * KV_WIDTH
COL_K = ATTN_WIDTH
COL_V = ATTN_WIDTH + KV_WIDTH
LANES = 128
BLOCKS_PER_VREG = LANES // SSM_GROUP_CH

VMEM_LIMIT = 48 * 1024 * 1024

F32 = jnp.float32
BF16 = jnp.bfloat16


def _row_tile(rows, target):
    k = rows // BLOCK
    best = 1
    for d in range(1, k + 1):
        if k % d == 0 and d * BLOCK <= target:
            best = d
    return best * BLOCK


def _rms(x, gain):
    ms = jnp.mean(x * x, axis=-1, keepdims=True)
    return x * lax.rsqrt(ms + NORM_EPS) * gain


def _inproj_kernel(x_ref, g_ref, w_ref, qkv_ref, u_ref):
    h = _rms(x_ref[...], g_ref[...]).astype(BF16)
    proj = jnp.dot(h, w_ref[...], preferred_element_type=F32)
    qkv_ref[...] = proj[:, :QKV_WIDTH].astype(qkv_ref.dtype)
    u_ref[...] = proj[:, QKV_WIDTH:]


def _inproj(x, gain, w):
    rows = x.shape[0]
    tm = _row_tile(rows, 384)
    return pl.pallas_call(
        _inproj_kernel,
        grid=(rows // tm,),
        in_specs=[
            pl.BlockSpec((tm, D_MODEL), lambda i: (i, 0)),
            pl.BlockSpec((1, D_MODEL), lambda i: (0, 0)),
            pl.BlockSpec((D_MODEL, IN_WIDTH), lambda i: (0, 0)),
        ],
        out_specs=[pl.BlockSpec((tm, QKV_WIDTH), lambda i: (i, 0)),
                   pl.BlockSpec((tm, SSM_WIDTH), lambda i: (i, 0))],
        out_shape=[jax.ShapeDtypeStruct((rows, QKV_WIDTH), BF16),
                   jax.ShapeDtypeStruct((rows, SSM_WIDTH), F32)],
        compiler_params=pltpu.CompilerParams(
            dimension_semantics=("parallel",), vmem_limit_bytes=VMEM_LIMIT),
        name="inproj",
    )(x, gain, w)


def _attn_kernel(sink_ref, q_ref, kp_ref, kc_ref, vp_ref, vc_ref, km_ref, vm_ref,
                 qg_ref, kg_ref, og_ref, o_ref):
    n = pl.program_id(1)
    nkeys = 2 * BLOCK + N_META
    i_idx = lax.broadcasted_iota(jnp.int32, (BLOCK, nkeys), 0)
    j_idx = lax.broadcasted_iota(jnp.int32, (BLOCK, nkeys), 1)
    t_pos = n * BLOCK + i_idx
    is_meta = j_idx >= 2 * BLOCK
    s_pos = jnp.where(is_meta, j_idx - 2 * BLOCK, (n - 1) * BLOCK + j_idx)
    dist = t_pos - s_pos
    min_pos = jnp.where(is_meta, 0, N_META)
    max_dist = jnp.where(is_meta, jnp.iinfo(jnp.int32).max, WINDOW)
    visible = (s_pos >= min_pos) & (dist >= 0) & (dist < max_dist)
    dist_f = jnp.abs(dist).astype(F32)

    q_gain = qg_ref[...] * (1.0 / math.sqrt(HEAD_DIM))
    k_gain = kg_ref[...]
    outs = []
    for kh in range(N_KV_HEADS):
        ks = slice(kh * HEAD_DIM, (kh + 1) * HEAD_DIM)
        k_all = jnp.concatenate([kp_ref[:, ks], kc_ref[:, ks], km_ref[:, ks]], axis=0)
        v_all = jnp.concatenate([vp_ref[:, ks], vc_ref[:, ks], vm_ref[:, ks]], axis=0)
        k_n = _rms(k_all.astype(F32), k_gain).astype(BF16)
        for g in range(KV_GROUP):
            h = kh * KV_GROUP + g
            slope = 2.0 ** (-8.0 * (h + 1) / N_HEADS)
            q_h = q_ref[:, h * HEAD_DIM:(h + 1) * HEAD_DIM].astype(F32)
            q_n = _rms(q_h, q_gain).astype(BF16)
            s = lax.dot_general(q_n, k_n, (((1,), (1,)), ((), ())),
                                preferred_element_type=F32)
            s = jnp.where(visible, s - slope * dist_f, NEG_INF)
            sink = sink_ref[h]
            m = jnp.maximum(jnp.max(s, axis=-1, keepdims=True), sink)
            p = jnp.exp(s - m)
            denom = jnp.sum(p, axis=-1, keepdims=True) + jnp.exp(sink - m)
            o = jnp.dot(p.astype(BF16), v_all, preferred_element_type=F32)
            outs.append(o / denom)
    attn = jnp.concatenate(outs, axis=-1)
    o_ref[...] = _rms(attn, og_ref[...]).astype(o_ref.dtype)


def _attention(proj3, sinks, q_gain, k_gain, out_gain):
    b, lp, _ = proj3.shape
    nb = lp // BLOCK
    kcol, vcol = COL_K // KV_WIDTH, COL_V // KV_WIDTH
    prev = lambda bi, n: jnp.maximum(n - 1, 0)
    return pl.pallas_call(
        _attn_kernel,
        grid=(b, nb),
        in_specs=[
            pl.BlockSpec(memory_space=pltpu.SMEM),
            pl.BlockSpec((None, BLOCK, ATTN_WIDTH), lambda bi, n: (bi, n, 0)),
            pl.BlockSpec((None, BLOCK, KV_WIDTH), lambda bi, n: (bi, prev(bi, n), kcol)),
            pl.BlockSpec((None, BLOCK, KV_WIDTH), lambda bi, n: (bi, n, kcol)),
            pl.BlockSpec((None, BLOCK, KV_WIDTH), lambda bi, n: (bi, prev(bi, n), vcol)),
            pl.BlockSpec((None, BLOCK, KV_WIDTH), lambda bi, n: (bi, n, vcol)),
            pl.BlockSpec((None, N_META, KV_WIDTH), lambda bi, n: (bi, 0, kcol)),
            pl.BlockSpec((None, N_META, KV_WIDTH), lambda bi, n: (bi, 0, vcol)),
            pl.BlockSpec((1, HEAD_DIM), lambda bi, n: (0, 0)),
            pl.BlockSpec((1, HEAD_DIM), lambda bi, n: (0, 0)),
            pl.BlockSpec((1, ATTN_WIDTH), lambda bi, n: (0, 0)),
        ],
        out_specs=pl.BlockSpec((None, BLOCK, ATTN_WIDTH), lambda bi, n: (bi, n, 0)),
        out_shape=jax.ShapeDtypeStruct((b, lp, ATTN_WIDTH), BF16),
        compiler_params=pltpu.CompilerParams(
            dimension_semantics=("parallel", "parallel"), vmem_limit_bytes=VMEM_LIMIT),
        name="attention",
    )(sinks, proj3, proj3, proj3, proj3, proj3, proj3, proj3, q_gain, k_gain, out_gain)


def _gelu_tanh(x):
    c = math.sqrt(2.0 / math.pi)
    return 0.5 * x * (1.0 + jnp.tanh(c * (x + 0.044715 * (x * x * x))))


def _ssm_kernel(u_ref, m_ref, win_ref, wc_ref, are_ref, aim_ref, d_ref, o_ref,
                lhs, yscr, inre, inim, sre, sim, *, n_chunks, n_batch):
    pw = 2 * SSM_STATE
    nblk = BLOCKS_PER_VREG
    rt = 16
    n_rt = (n_batch * n_chunks) // rt
    lane_blk = lax.broadcasted_iota(jnp.int32, (rt, LANES), 1) // SSM_GROUP_CH
    blk_mask = [lane_blk == k for k in range(nblk)]

    def shifted(x, k):
        return x if k == 0 else pltpu.roll(x, k * SSM_GROUP_CH, axis=1)

    def tok_rows(r, t):
        return pl.ds(r * (rt * CHUNK) + t, rt, stride=CHUNK)

    def relayout_in(r, carry):
        crow = pl.ds(pl.multiple_of(r * rt, rt), rt)
        for hf in range(CHUNK // nblk):
            xs = [u_ref[tok_rows(r, hf * nblk + tb), :] for tb in range(nblk)]
            rots = [[shifted(xs[tb], k) for k in range(nblk)] for tb in range(nblk)]
            for g in range(nblk):
                acc = rots[0][(-g) % nblk]
                for tb in range(1, nblk):
                    acc = jnp.where(blk_mask[tb], rots[tb][(tb - g) % nblk], acc)
                col = g * CHUNK_W + hf * LANES
                lhs[crow, col:col + LANES] = acc.astype(BF16)
        return carry

    lax.fori_loop(0, n_rt, relayout_in, 0)

    for p in range(PAIRS_PER_STEP):
        u_pair = lhs[:, p * 2 * CHUNK_W:(p + 1) * 2 * CHUNK_W]
        z = jnp.dot(u_pair, win_ref[p], preferred_element_type=F32)
        inre[:, p * pw:(p + 1) * pw] = z[:, :pw]
        inim[:, p * pw:(p + 1) * pw] = z[:, pw:]

    a_re = are_ref[...]
    a_im = aim_ref[...]

    def step(c, carry):
        nxt = []
        for b in range(n_batch):
            s_r, s_i = carry[b]
            row = b * n_chunks + c
            sre[pl.ds(row, 1), :] = s_r
            sim[pl.ds(row, 1), :] = s_i
            x_r = inre[pl.ds(row, 1), :]
            x_i = inim[pl.ds(row, 1), :]
            nxt.append((a_re * s_r - a_im * s_i + x_r, a_re * s_i + a_im * s_r + x_i))
        return tuple(nxt)

    zero = jnp.zeros((1, PAIRS_PER_STEP * pw), F32)
    lax.fori_loop(0, n_chunks, step, tuple((zero, zero) for _ in range(n_batch)))

    for p in range(PAIRS_PER_STEP):
        s_cat = jnp.concatenate(
            [sre[:, p * pw:(p + 1) * pw], sim[:, p * pw:(p + 1) * pw]], axis=-1).astype(BF16)
        y_carry = jnp.dot(s_cat, wc_ref[p], preferred_element_type=F32)
        for q in range(2):
            g = 2 * p + q
            cols = slice(g * CHUNK_W, (g + 1) * CHUNK_W)
            y = jnp.dot(lhs[:, cols], m_ref[g], preferred_element_type=F32)
            yscr[:, cols] = y + y_carry[:, q * CHUNK_W:(q + 1) * CHUNK_W]

    d_row = d_ref[...]

    def relayout_out(r, carry):
        crow = pl.ds(pl.multiple_of(r * rt, rt), rt)
        for hf in range(CHUNK // nblk):
            ys = [yscr[crow, g * CHUNK_W + hf * LANES:g * CHUNK_W + (hf + 1) * LANES]
                  for g in range(nblk)]
            rots = [[shifted(ys[g], k) for k in range(nblk)] for g in range(nblk)]
            for tb in range(nblk):
                acc = rots[0][(-tb) % nblk]
                for g in range(1, nblk):
                    acc = jnp.where(blk_mask[g], rots[g][(g - tb) % nblk], acc)
                rows = tok_rows(r, hf * nblk + tb)
                o_ref[rows, :] = _gelu_tanh(acc + d_row * u_ref[rows, :])
        return carry

    lax.fori_loop(0, n_rt, relayout_out, 0)


def _ssm(u_tok, m, win, wc, a_re, a_im, d, n_batch):
    rows, width = u_tok.shape
    rc = rows // CHUNK
    n_chunks = rc // n_batch
    assert GROUPS_PER_STEP == BLOCKS_PER_VREG and rc % 16 == 0
    bw = GROUPS_PER_STEP * CHUNK_W
    sw = GROUPS_PER_STEP * SSM_STATE
    steps = SSM_GROUPS // GROUPS_PER_STEP
    kern = functools.partial(_ssm_kernel, n_chunks=n_chunks, n_batch=n_batch)
    return pl.pallas_call(
        kern,
        grid=(steps,),
        in_specs=[
            pl.BlockSpec((rows, LANES), lambda j: (0, j)),
            pl.BlockSpec((GROUPS_PER_STEP, CHUNK_W, CHUNK_W), lambda j: (j, 0, 0)),
            pl.BlockSpec((PAIRS_PER_STEP, 2 * CHUNK_W, 4 * SSM_STATE), lambda j: (j, 0, 0)),
            pl.BlockSpec((PAIRS_PER_STEP, 4 * SSM_STATE, 2 * CHUNK_W), lambda j: (j, 0, 0)),
            pl.BlockSpec((1, sw), lambda j: (0, j)),
            pl.BlockSpec((1, sw), lambda j: (0, j)),
            pl.BlockSpec((1, LANES), lambda j: (0, j)),
        ],
        out_specs=pl.BlockSpec((rows, LANES), lambda j: (0, j)),
        out_shape=jax.ShapeDtypeStruct((rows, width), F32),
        scratch_shapes=[pltpu.VMEM((rc, bw), BF16), pltpu.VMEM((rc, bw), F32)]
        + [pltpu.VMEM((rc, sw), F32) for _ in range(4)],
        compiler_params=pltpu.CompilerParams(
            dimension_semantics=("parallel",), vmem_limit_bytes=VMEM_LIMIT),
        name="ssm",
    )(u_tok, m, win, wc, a_re, a_im, d)


def _ssm_weights(lam_re, lam_im, log_step, b_re, b_im, c_re, c_im):
    depth = lam_re.shape[0]
    g, p, h, t = SSM_GROUPS, SSM_STATE, SSM_GROUP_CH, CHUNK
    lam = lax.complex(lam_re.astype(F32), lam_im.astype(F32))
    delta = jnp.exp(log_step.astype(F32))[..., None]
    ld = lam * delta
    lam_bar = jnp.exp(ld)
    b_bar = ((lam_bar - 1.0) / lam)[..., None] * lax.complex(b_re.astype(F32), b_im.astype(F32))
    c_c = lax.complex(c_re.astype(F32), c_im.astype(F32))
    tau = jnp.arange(t + 1, dtype=F32)
    powers = jnp.exp(ld[:, :, None, :] * tau[None, None, :, None])

    kt = jnp.real(jnp.einsum('dghp,dgtp,dgpk->dgthk', c_c, powers[:, :, :t], b_bar))
    s_idx = jnp.arange(t)[:, None]
    t_idx = jnp.arange(t)[None, :]
    lag = t_idx - s_idx
    kk = kt[:, :, jnp.clip(lag, 0, t - 1)]
    kk = jnp.where((lag >= 0)[None, None, :, :, None, None], kk, 0.0)
    m = kk.transpose(0, 1, 2, 5, 3, 4).reshape(depth, g, t * h, t * h)

    win = jnp.einsum('dgtp,dgpk->dgtkp', powers[:, :, t - 1::-1], b_bar).reshape(depth, g, t * h, p)
    wr = jnp.real(win).reshape(depth, g // 2, 2, t * h, p)
    wi = jnp.imag(win).reshape(depth, g // 2, 2, t * h, p)
    z = jnp.zeros_like(wr[:, :, 0])
    win_pair = jnp.concatenate([
        jnp.concatenate([wr[:, :, 0], z, wi[:, :, 0], z], axis=-1),
        jnp.concatenate([z, wr[:, :, 1], z, wi[:, :, 1]], axis=-1)], axis=-2)

    e = jnp.einsum('dghp,dgtp->dgpth', c_c, powers[:, :, 1:]).reshape(depth, g, p, t * h)
    er = jnp.real(e).reshape(depth, g // 2, 2, p, t * h)
    ei = -jnp.imag(e).reshape(depth, g // 2, 2, p, t * h)
    z = jnp.zeros_like(er[:, :, 0])
    wc_pair = jnp.concatenate([
        jnp.concatenate([er[:, :, 0], z], axis=-1),
        jnp.concatenate([z, er[:, :, 1]], axis=-1),
        jnp.concatenate([ei[:, :, 0], z], axis=-1),
        jnp.concatenate([z, ei[:, :, 1]], axis=-1)], axis=-2)

    a_chunk = powers[:, :, t]
    a_re = jnp.real(a_chunk).reshape(depth, 1, g * p)
    a_im = jnp.imag(a_chunk).reshape(depth, 1, g * p)
    return m.astype(BF16), win_pair.astype(BF16), wc_pair.astype(BF16), a_re, a_im


def _outproj_kernel(a_ref, g_ref, x_ref, wglu_ref, bglu_ref, sg_ref, wout_ref, ng_ref,
                    o_ref, xn_ref):
    g = g_ref[...]
    z = jnp.dot(g.astype(BF16), wglu_ref[...], preferred_element_type=F32) + bglu_ref[...]
    s = g * jax.nn.sigmoid(z)
    s_n = _rms(s, sg_ref[...]).astype(BF16)
    y = jnp.dot(a_ref[...], wout_ref[:ATTN_WIDTH, :], preferred_element_type=F32)
    y = y + jnp.dot(s_n, wout_ref[ATTN_WIDTH:, :], preferred_element_type=F32)
    h = x_ref[...] + y
    o_ref[...] = h
    xn_ref[...] = _rms(h, ng_ref[...]).astype(xn_ref.dtype)


def _outproj(attn_n, g_tok, x, w_glu, b_glu, ssm_gain, w_out, mlp_gain):
    rows = x.shape[0]
    tm = _row_tile(rows, 384)
    row = lambda i: (i, 0)
    fixed = lambda i: (0, 0)
    return pl.pallas_call(
        _outproj_kernel,
        grid=(rows // tm,),
        in_specs=[
            pl.BlockSpec((tm, ATTN_WIDTH), row),
            pl.BlockSpec((tm, SSM_WIDTH), row),
            pl.BlockSpec((tm, D_MODEL), row),
            pl.BlockSpec((SSM_WIDTH, SSM_WIDTH), fixed),
            pl.BlockSpec((1, SSM_WIDTH), fixed),
            pl.BlockSpec((1, SSM_WIDTH), fixed),
            pl.BlockSpec((D_MODEL, D_MODEL), fixed),
            pl.BlockSpec((1, D_MODEL), fixed),
        ],
        out_specs=[pl.BlockSpec((tm, D_MODEL), row), pl.BlockSpec((tm, D_MODEL), row)],
        out_shape=[jax.ShapeDtypeStruct((rows, D_MODEL), F32),
                   jax.ShapeDtypeStruct((rows, D_MODEL), BF16)],
        compiler_params=pltpu.CompilerParams(
            dimension_semantics=("parallel",), vmem_limit_bytes=VMEM_LIMIT),
        name="outproj",
    )(attn_n, g_tok, x, w_glu, b_glu, ssm_gain, w_out, mlp_gain)


def _mlp_kernel(xn_ref, x_ref, wu_ref, wd_ref, o_ref):
    f = pl.program_id(1)
    a = jnp.dot(xn_ref[...], wu_ref[...], preferred_element_type=F32)
    a = jnp.square(jnp.maximum(a, 0.0)).astype(BF16)
    y = jnp.dot(a, wd_ref[...], preferred_element_type=F32)

    @pl.when(f == 0)
    def _():
        o_ref[...] = x_ref[...] + y

    @pl.when(f > 0)
    def _():
        o_ref[...] += y


def _mlp(xn, x, w_up, w_down):
    rows = x.shape[0]
    tm = _row_tile(rows, 768)
    tf = 512
    return pl.pallas_call(
        _mlp_kernel,
        grid=(rows // tm, D_FF // tf),
        in_specs=[
            pl.BlockSpec((tm, D_MODEL), lambda i, f: (i, 0)),
            pl.BlockSpec((tm, D_MODEL), lambda i, f: (i, 0)),
            pl.BlockSpec((D_MODEL, tf), lambda i, f: (0, f)),
            pl.BlockSpec((tf, D_MODEL), lambda i, f: (f, 0)),
        ],
        out_specs=pl.BlockSpec((tm, D_MODEL), lambda i, f: (i, 0)),
        out_shape=jax.ShapeDtypeStruct((rows, D_MODEL), F32),
        compiler_params=pltpu.CompilerParams(
            dimension_semantics=("parallel", "arbitrary"), vmem_limit_bytes=VMEM_LIMIT),
        name="mlp",
    )(xn, x, w_up, w_down)


def kernel(x, meta_tokens, norm_mix_g, w_in, q_norm_g, k_norm_g, attn_sinks, ssm_lambda_re, ssm_lambda_im, ssm_log_step, ssm_b_re, ssm_b_im, ssm_c_re, ssm_c_im, ssm_d, w_glu, b_glu, attn_out_g, ssm_out_g, w_out, norm_mlp_g, w_up, w_down):
    b, seq, d = x.shape
    depth = w_in.shape[0]
    assert d == D_MODEL
    length = N_META + seq
    lp = -(-length // BLOCK) * BLOCK
    rows = b * lp

    meta = jnp.broadcast_to(meta_tokens.astype(x.dtype)[None], (b, N_META, d))
    pad = jnp.zeros((b, lp - length, d), x.dtype)
    h_res = jnp.concatenate([meta, x, pad], axis=1).reshape(rows, d)

    w_in_b = w_in.astype(BF16)
    w_glu_b = w_glu.astype(BF16)
    w_out_b = w_out.astype(BF16)
    w_up_b = w_up.astype(BF16)
    w_down_b = w_down.astype(BF16)
    m, win_pair, wc_pair, a_re, a_im = _ssm_weights(
        ssm_lambda_re, ssm_lambda_im, ssm_log_step, ssm_b_re, ssm_b_im, ssm_c_re, ssm_c_im)

    row_vec = lambda v: v.astype(F32).reshape(1, -1)
    for l in range(depth):
        qkv, u_tok = _inproj(h_res, row_vec(norm_mix_g[l]), w_in_b[l])
        attn_n = _attention(qkv.reshape(b, lp, QKV_WIDTH), attn_sinks[l].astype(F32),
                            row_vec(q_norm_g[l]), row_vec(k_norm_g[l]), row_vec(attn_out_g[l]))
        g_tok = _ssm(u_tok, m[l], win_pair[l], wc_pair[l], a_re[l], a_im[l], row_vec(ssm_d[l]), b)
        h_res, xn = _outproj(attn_n.reshape(rows, ATTN_WIDTH), g_tok, h_res, w_glu_b[l],
                             row_vec(b_glu[l]), row_vec(ssm_out_g[l]), w_out_b[l],
                             row_vec(norm_mlp_g[l]))
        h_res = _mlp(xn, h_res, w_up_b[l], w_down_b[l])
    return h_res.reshape(b, lp, d)[:, N_META:length]
```

```python
import functools
import math

import jax
import jax.numpy as jnp
from jax import lax
from jax.experimental import pallas as pl
from jax.experimental.pallas import tpu as pltpu

D_MODEL = 2048
N_META = 16
HEAD_DIM = 64
ATTN_WIDTH = D_MODEL // 2
N_HEADS = ATTN_WIDTH // HEAD_DIM
N_KV_HEADS = N_HEADS // 4
KV_GROUP = N_HEADS // N_KV_HEADS
KV_WIDTH = N_KV_HEADS * HEAD_DIM
SSM_WIDTH = D_MODEL - ATTN_WIDTH
SSM_GROUP_CH = 16
SSM_GROUPS = SSM_WIDTH // SSM_GROUP_CH
SSM_STATE = 64
WINDOW = 128
BLOCK = 128
D_FF = 4 * D_MODEL
IN_WIDTH = ATTN_WIDTH + 2 * KV_WIDTH + SSM_WIDTH
NORM_EPS = 1e-6
NEG_INF = -1e30

CHUNK = 16
CHUNK_W = CHUNK * SSM_GROUP_CH
GROUPS_PER_STEP = 8

QKV_WIDTH = ATTN_WIDTH + 2 * KV_WIDTH
COL_K = ATTN_WIDTH
COL_V = ATTN_WIDTH + KV_WIDTH
LANES = 128
BLOCKS_PER_VREG = LANES // SSM_GROUP_CH

VMEM_LIMIT = 48 * 1024 * 1024

F32 = jnp.float32
BF16 = jnp.bfloat16


def _row_tile(rows, target):
    k = rows // BLOCK
    best = 1
    for d in range(1, k + 1):
        if k % d == 0 and d * BLOCK <= target:
            best = d
    return best * BLOCK


def _rms(x, gain):
    ms = jnp.mean(x * x, axis=-1, keepdims=True)
    return x * lax.rsqrt(ms + NORM_EPS) * gain


def _head_norm(x, head_ones, gain):
    x2 = x * x
    hi = x2.astype(BF16)
    lo = (x2 - hi.astype(F32)).astype(BF16)
    ssq = (jnp.dot(hi, head_ones, preferred_element_type=F32)
           + jnp.dot(lo, head_ones, preferred_element_type=F32))
    return x * lax.rsqrt(ssq * (1.0 / HEAD_DIM) + NORM_EPS) * gain


def _inproj_kernel(x_ref, g_ref, w_ref, ones_ref, qg_ref, kg_ref, qkv_ref, u_ref):
    h = _rms(x_ref[...], g_ref[...]).astype(BF16)
    proj = jnp.dot(h, w_ref[...], preferred_element_type=F32)
    head_ones = ones_ref[...]
    for blk in range(ATTN_WIDTH // KV_WIDTH):
        cols = slice(blk * KV_WIDTH, (blk + 1) * KV_WIDTH)
        qkv_ref[:, cols] = _head_norm(proj[:, cols], head_ones, qg_ref[...]).astype(qkv_ref.dtype)
    qkv_ref[:, COL_K:COL_V] = _head_norm(
        proj[:, COL_K:COL_V], head_ones, kg_ref[...]).astype(qkv_ref.dtype)
    qkv_ref[:, COL_V:QKV_WIDTH] = proj[:, COL_V:QKV_WIDTH].astype(qkv_ref.dtype)
    u_ref[...] = proj[:, QKV_WIDTH:]


def _inproj(x, gain, w, head_ones, q_gain, k_gain):
    rows = x.shape[0]
    tm = _row_tile(rows, 384)
    fixed = lambda i: (0, 0)
    return pl.pallas_call(
        _inproj_kernel,
        grid=(rows // tm,),
        in_specs=[
            pl.BlockSpec((tm, D_MODEL), lambda i: (i, 0)),
            pl.BlockSpec((1, D_MODEL), fixed),
            pl.BlockSpec((D_MODEL, IN_WIDTH), fixed),
            pl.BlockSpec((KV_WIDTH, KV_WIDTH), fixed),
            pl.BlockSpec((1, KV_WIDTH), fixed),
            pl.BlockSpec((1, KV_WIDTH), fixed),
        ],
        out_specs=[pl.BlockSpec((tm, QKV_WIDTH), lambda i: (i, 0)),
                   pl.BlockSpec((tm, SSM_WIDTH), lambda i: (i, 0))],
        out_shape=[jax.ShapeDtypeStruct((rows, QKV_WIDTH), BF16),
                   jax.ShapeDtypeStruct((rows, SSM_WIDTH), F32)],
        compiler_params=pltpu.CompilerParams(
            dimension_semantics=("parallel",), vmem_limit_bytes=VMEM_LIMIT),
        name="inproj",
    )(x, gain, w, head_ones, q_gain, k_gain)


SUB = 64
BAND = WINDOW + SUB
KEYS = 2 * BLOCK


def _attn_bias(slopes):
    j = jnp.arange(KEYS)[:, None]
    i = jnp.arange(SUB)[None, :]
    is_band = j < BAND
    is_meta = (j >= BAND) & (j < BAND + N_META)
    dist = WINDOW + i - j
    sl = slopes.reshape(N_KV_HEADS, 1, KV_GROUP, 1)
    dist_f = dist.astype(F32)[None, :, None, :]
    band = jnp.where(((dist >= 0) & (dist < WINDOW))[None, :, None, :], -sl * dist_f, NEG_INF)
    meta = -sl * (i - (j - BAND)).astype(F32)[None, :, None, :]
    bias = jnp.where(is_band[None, :, None, :], band,
                     jnp.where(is_meta[None, :, None, :], meta, NEG_INF))
    lanes = KV_GROUP * SUB
    off_coef = jnp.broadcast_to(-sl, (N_KV_HEADS, N_META, KV_GROUP, SUB))
    return bias.reshape(N_KV_HEADS, KEYS, lanes), off_coef.reshape(N_KV_HEADS, N_META, lanes)


def _attn_kernel(q_ref, kp_ref, kc_ref, vp_ref, vc_ref, km_ref, vm_ref,
                 bias_ref, offc_ref, sink_ref, og_ref, o_ref, acc):
    n = pl.program_id(1)
    lanes = KV_GROUP * SUB
    ones = jnp.ones((KEYS, 8), BF16)
    fill = KEYS - BAND - N_META

    def body(early):
        chains = [(sb, kh) for sb in range(BLOCK // SUB) for kh in range(N_KV_HEADS)]
        scores, values = [], []
        for sb, kh in chains:
            start = n * BLOCK + sb * SUB
            lo = sb * SUB
            ks = slice(kh * HEAD_DIM, (kh + 1) * HEAD_DIM)
            k_cat = jnp.concatenate([kp_ref[:, ks], kc_ref[:, ks]], axis=0)
            v_cat = jnp.concatenate([vp_ref[:, ks], vc_ref[:, ks]], axis=0)
            k_win = jnp.concatenate([k_cat[lo:lo + BAND], km_ref[:, ks], k_cat[:fill]], axis=0)
            v_win = jnp.concatenate([v_cat[lo:lo + BAND], vm_ref[:, ks], v_cat[:fill]], axis=0)
            values.append(jnp.concatenate([v_win, ones], axis=1))
            q_s = jnp.concatenate(
                [q_ref[lo:lo + SUB, (kh * KV_GROUP + g) * HEAD_DIM:(kh * KV_GROUP + g + 1) * HEAD_DIM]
                 for g in range(KV_GROUP)], axis=0)
            s = lax.dot_general(k_win, q_s, (((1,), (1,)), ((), ())),
                                preferred_element_type=F32)
            s = s + bias_ref[kh]
            s = jnp.concatenate([s[:BAND], s[BAND:BAND + N_META] + offc_ref[kh] * start.astype(F32),
                                 s[BAND + N_META:]], axis=0)
            if early:
                j = lax.broadcasted_iota(jnp.int32, (KEYS, lanes), 0)
                i = lax.broadcasted_iota(jnp.int32, (KEYS, lanes), 1) % SUB
                ok = jnp.where(j < BAND, start - WINDOW + j - N_META, start + i - (j - BAND)) >= 0
                s = jnp.where(ok, s, NEG_INF)
            scores.append(s)
        maxes = [jnp.maximum(jnp.max(s, axis=0, keepdims=True), sink_ref[kh])
                 for s, (sb, kh) in zip(scores, chains)]
        probs = [jnp.exp(s - m).astype(BF16) for s, m in zip(scores, maxes)]
        for (sb, kh), p, m, v_ext in zip(chains, probs, maxes, values):
            o_ext = lax.dot_general(v_ext, p, (((0,), (0,)), ((), ())),
                                    preferred_element_type=F32)
            denom = o_ext[HEAD_DIM:HEAD_DIM + 1] + jnp.exp(sink_ref[kh] - m)
            o = o_ext[:HEAD_DIM] / denom
            for g in range(KV_GROUP):
                h = kh * KV_GROUP + g
                acc[h * HEAD_DIM:(h + 1) * HEAD_DIM, sb * SUB:(sb + 1) * SUB] = o[:, g * SUB:(g + 1) * SUB]
        a_t = acc[...]
        ms = jnp.mean(a_t * a_t, axis=0, keepdims=True)
        a_n = a_t * lax.rsqrt(ms + NORM_EPS)
        o_ref[...] = (a_n.T * og_ref[...]).astype(o_ref.dtype)

    @pl.when(n >= 2)
    def _():
        body(False)

    @pl.when(n < 2)
    def _():
        body(True)


def _attention(qkv3, bias, off_coef, sink_row, out_gain):
    b, lp, _ = qkv3.shape
    nb = lp // BLOCK
    kcol, vcol = COL_K // KV_WIDTH, COL_V // KV_WIDTH
    prev = lambda n: jnp.maximum(n - 1, 0)
    fixed3 = lambda bi, n: (0, 0, 0)
    return pl.pallas_call(
        _attn_kernel,
        grid=(b, nb),
        in_specs=[
            pl.BlockSpec((None, BLOCK, ATTN_WIDTH), lambda bi, n: (bi, n, 0)),
            pl.BlockSpec((None, BLOCK, KV_WIDTH), lambda bi, n: (bi, prev(n), kcol)),
            pl.BlockSpec((None, BLOCK, KV_WIDTH), lambda bi, n: (bi, n, kcol)),
            pl.BlockSpec((None, BLOCK, KV_WIDTH), lambda bi, n: (bi, prev(n), vcol)),
            pl.BlockSpec((None, BLOCK, KV_WIDTH), lambda bi, n: (bi, n, vcol)),
            pl.BlockSpec((None, N_META, KV_WIDTH), lambda bi, n: (bi, 0, kcol)),
            pl.BlockSpec((None, N_META, KV_WIDTH), lambda bi, n: (bi, 0, vcol)),
            pl.BlockSpec(bias.shape, fixed3),
            pl.BlockSpec(off_coef.shape, fixed3),
            pl.BlockSpec(sink_row.shape, fixed3),
            pl.BlockSpec((1, ATTN_WIDTH), lambda bi, n: (0, 0)),
        ],
        out_specs=pl.BlockSpec((None, BLOCK, ATTN_WIDTH), lambda bi, n: (bi, n, 0)),
        out_shape=jax.ShapeDtypeStruct((b, lp, ATTN_WIDTH), BF16),
        compiler_params=pltpu.CompilerParams(
            dimension_semantics=("parallel", "parallel"), vmem_limit_bytes=VMEM_LIMIT),
        scratch_shapes=[pltpu.VMEM((ATTN_WIDTH, BLOCK), F32)],
        name="attention",
    )(qkv3, qkv3, qkv3, qkv3, qkv3, qkv3, qkv3, bias, off_coef, sink_row, out_gain)


def _gelu_tanh(x):
    c = math.sqrt(2.0 / math.pi)
    return 0.5 * x * (1.0 + jnp.tanh(c * (x + 0.044715 * (x * x * x))))


def _ssm_kernel(u_ref, ktr_ref, bb_ref, cc_ref, pw_ref, a_ref, d_ref, o_ref,
                lhs, yscr, m_scr, win_scr, wsw_scr, wct_scr, zin, zsw, st,
                *, n_chunks, n_batch):
    nblk = BLOCKS_PER_VREG
    sw = 2 * SSM_STATE
    rt = 16
    n_rt = (n_batch * n_chunks) // rt
    half = CHUNK // nblk

    def block_transpose(arrs, lane_blk):
        for dist in (4, 2, 1):
            keep = (lane_blk & dist) == 0
            nxt = list(arrs)
            for i in range(nblk):
                if i & dist == 0:
                    lo, hi = arrs[i], arrs[i + dist]
                    nxt[i] = jnp.where(keep, lo, pltpu.roll(hi, dist * SSM_GROUP_CH, axis=1))
                    nxt[i + dist] = jnp.where(
                        keep, pltpu.roll(lo, (nblk - dist) * SSM_GROUP_CH, axis=1), hi)
            arrs = nxt
        return arrs

    def tok_rows(r, t):
        return pl.ds(r * (rt * CHUNK) + t, rt, stride=CHUNK)

    lane256 = lax.broadcasted_iota(jnp.int32, (CHUNK, CHUNK_W), 1)
    for g in range(GROUPS_PER_STEP):
        base = ktr_ref[g]
        for sblk in range(CHUNK):
            row = base if sblk == 0 else jnp.where(
                lane256 >= sblk * SSM_GROUP_CH, pltpu.roll(base, sblk * SSM_GROUP_CH, axis=1), 0.0)
            m_scr[g, sblk * CHUNK:(sblk + 1) * CHUNK, :] = row.astype(BF16)
        bb, bbs, bsw, bsws = bb_ref[g, 0], bb_ref[g, 1], bb_ref[g, 2], bb_ref[g, 3]
        cc, ccs = cc_ref[g, 0], cc_ref[g, 1]
        for t in range(CHUNK):
            rows = slice(t * CHUNK, (t + 1) * CHUNK)
            p_re = pw_ref[g, 0, CHUNK - 1 - t:CHUNK - t, :]
            p_im = pw_ref[g, 1, CHUNK - 1 - t:CHUNK - t, :]
            win_scr[g, rows, :] = (p_re * bb + p_im * bbs).astype(BF16)
            wsw_scr[g, rows, :] = (p_re * bsw + p_im * bsws).astype(BF16)
            q_re = pw_ref[g, 0, t + 1:t + 2, :]
            q_im = pw_ref[g, 1, t + 1:t + 2, :]
            wct_scr[g, rows, :] = (q_re * cc + q_im * ccs).astype(BF16)

    lane_blk16 = lax.broadcasted_iota(jnp.int32, (rt // 2, LANES), 1) // SSM_GROUP_CH

    def relayout_in(r, carry):
        crow = pl.ds(pl.multiple_of(r * rt, rt), rt)
        for hf in range(half):
            xs = [pltpu.bitcast(u_ref[tok_rows(r, hf * nblk + tb), :].astype(BF16), jnp.uint32)
                  for tb in range(nblk)]
            outs = block_transpose(xs, lane_blk16)
            for g in range(nblk):
                col = g * CHUNK_W + hf * LANES
                lhs[crow, col:col + LANES] = pltpu.bitcast(outs[g], BF16)
        return carry

    lax.fori_loop(0, n_rt, relayout_in, 0)

    for g in range(GROUPS_PER_STEP):
        u_g = lhs[:, g * CHUNK_W:(g + 1) * CHUNK_W]
        zin[:, g * sw:(g + 1) * sw] = jnp.dot(u_g, win_scr[g], preferred_element_type=F32)
        zsw[:, g * sw:(g + 1) * sw] = jnp.dot(u_g, wsw_scr[g], preferred_element_type=F32)

    a1 = a_ref[0:1, :]
    a2 = a_ref[1:2, :]

    def step(c, carry):
        nxt = []
        for b in range(n_batch):
            s, x = carry[b]
            row = b * n_chunks + c
            st[pl.ds(row, 1), :] = s
            nxt.append((a1 * s + a2 * x + zin[pl.ds(row, 1), :],
                        a1 * x - a2 * s + zsw[pl.ds(row, 1), :]))
        return tuple(nxt)

    zero = jnp.zeros((1, GROUPS_PER_STEP * sw), F32)
    lax.fori_loop(0, n_chunks, step, tuple((zero, zero) for _ in range(n_batch)))

    for g in range(GROUPS_PER_STEP):
        cols = slice(g * CHUNK_W, (g + 1) * CHUNK_W)
        y = jnp.dot(lhs[:, cols], m_scr[g], preferred_element_type=F32)
        y = y + lax.dot_general(st[:, g * sw:(g + 1) * sw].astype(BF16), wct_scr[g],
                                (((1,), (1,)), ((), ())), preferred_element_type=F32)
        yscr[:, cols] = y

    d_row = d_ref[...]
    lane_blk32 = lax.broadcasted_iota(jnp.int32, (rt, LANES), 1) // SSM_GROUP_CH

    def relayout_out(r, carry):
        crow = pl.ds(pl.multiple_of(r * rt, rt), rt)
        for hf in range(half):
            ys = [yscr[crow, g * CHUNK_W + hf * LANES:g * CHUNK_W + (hf + 1) * LANES]
                  for g in range(nblk)]
            outs = block_transpose(ys, lane_blk32)
            for tb in range(nblk):
                rows = tok_rows(r, hf * nblk + tb)
                o_ref[rows, :] = _gelu_tanh(outs[tb] + d_row * u_ref[rows, :])
        return carry

    lax.fori_loop(0, n_rt, relayout_out, 0)


def _ssm(u_tok, ktr, bb, cc, pw, a16, d, n_batch):
    rows, width = u_tok.shape
    rc = rows // CHUNK
    n_chunks = rc // n_batch
    assert GROUPS_PER_STEP == BLOCKS_PER_VREG and rc % 16 == 0
    gs = GROUPS_PER_STEP
    bw = gs * CHUNK_W
    sw = gs * 2 * SSM_STATE
    kern = functools.partial(_ssm_kernel, n_chunks=n_chunks, n_batch=n_batch)
    grp = lambda j: (j, 0, 0)
    grp4 = lambda j: (j, 0, 0, 0)
    return pl.pallas_call(
        kern,
        grid=(SSM_GROUPS // gs,),
        in_specs=[
            pl.BlockSpec((rows, LANES), lambda j: (0, j)),
            pl.BlockSpec((gs,) + ktr.shape[1:], grp),
            pl.BlockSpec((gs,) + bb.shape[1:], grp4),
            pl.BlockSpec((gs,) + cc.shape[1:], grp4),
            pl.BlockSpec((gs,) + pw.shape[1:], grp4),
            pl.BlockSpec((2, sw), lambda j: (0, j)),
            pl.BlockSpec((1, LANES), lambda j: (0, j)),
        ],
        out_specs=pl.BlockSpec((rows, LANES), lambda j: (0, j)),
        out_shape=jax.ShapeDtypeStruct((rows, width), F32),
        scratch_shapes=[
            pltpu.VMEM((rc, bw), BF16),
            pltpu.VMEM((rc, bw), F32),
            pltpu.VMEM((gs, CHUNK_W, CHUNK_W), BF16),
            pltpu.VMEM((gs, CHUNK_W, 2 * SSM_STATE), BF16),
            pltpu.VMEM((gs, CHUNK_W, 2 * SSM_STATE), BF16),
            pltpu.VMEM((gs, CHUNK_W, 2 * SSM_STATE), BF16),
            pltpu.VMEM((rc, sw), F32),
            pltpu.VMEM((rc, sw), F32),
            pltpu.VMEM((rc, sw), F32),
        ],
        compiler_params=pltpu.CompilerParams(
            dimension_semantics=("parallel",), vmem_limit_bytes=VMEM_LIMIT),
        name="ssm",
    )(u_tok, ktr, bb, cc, pw, a16, d)


def _ssm_tables(lam_re, lam_im, log_step, b_re, b_im, c_re, c_im):
    depth = lam_re.shape[0]
    g, p, h, t = SSM_GROUPS, SSM_STATE, SSM_GROUP_CH, CHUNK
    lam = lax.complex(lam_re.astype(F32), lam_im.astype(F32))
    delta = jnp.exp(log_step.astype(F32))[..., None]
    ld = lam * delta
    lam_bar = jnp.exp(ld)
    b_bar = ((lam_bar - 1.0) / lam)[..., None] * lax.complex(b_re.astype(F32), b_im.astype(F32))
    c_c = lax.complex(c_re.astype(F32), c_im.astype(F32))
    tau = jnp.arange(t + 1, dtype=F32)
    powers = jnp.exp(ld[:, :, None, :] * tau[None, None, :, None])

    kt = jnp.real(jnp.einsum('dghp,dgtp,dgpk->dgkth', c_c, powers[:, :, :t], b_bar))
    ktr = kt.reshape(depth, g, h, t * h)

    cat = lambda x, y: jnp.concatenate([x, y], axis=-1)
    br, bi = jnp.real(b_bar).swapaxes(-1, -2), jnp.imag(b_bar).swapaxes(-1, -2)
    bb = jnp.stack([cat(br, bi), cat(-bi, br), cat(bi, br), cat(br, -bi)], axis=2)
    cr, ci = jnp.real(c_c), jnp.imag(c_c)
    cc = jnp.stack([cat(cr, -ci), cat(-ci, -cr)], axis=2)
    pr, pi = jnp.real(powers), jnp.imag(powers)
    pw = jnp.stack([cat(pr, pr), cat(pi, pi)], axis=2)
    pw = jnp.pad(pw, ((0, 0), (0, 0), (0, 0), (0, 24 - (t + 1)), (0, 0)))
    ar, ai = pr[:, :, t], pi[:, :, t]
    a16 = jnp.stack([cat(ar, ar).reshape(depth, g * 2 * p),
                     cat(-ai, ai).reshape(depth, g * 2 * p)], axis=1)
    return ktr, bb, cc, pw, a16


def _outproj_kernel(a_ref, g_ref, x_ref, wglu_ref, bglu_ref, sg_ref, wout_ref, ng_ref,
                    o_ref, xn_ref):
    g = g_ref[...]
    z = jnp.dot(g.astype(BF16), wglu_ref[...], preferred_element_type=F32) + bglu_ref[...]
    s = g * jax.nn.sigmoid(z)
    s_n = _rms(s, sg_ref[...]).astype(BF16)
    y = jnp.dot(a_ref[...], wout_ref[:ATTN_WIDTH, :], preferred_element_type=F32)
    y = y + jnp.dot(s_n, wout_ref[ATTN_WIDTH:, :], preferred_element_type=F32)
    h = x_ref[...] + y
    o_ref[...] = h
    xn_ref[...] = _rms(h, ng_ref[...]).astype(xn_ref.dtype)


def _outproj(attn_n, g_tok, x, w_glu, b_glu, ssm_gain, w_out, mlp_gain):
    rows = x.shape[0]
    tm = _row_tile(rows, 384)
    row = lambda i: (i, 0)
    fixed = lambda i: (0, 0)
    return pl.pallas_call(
        _outproj_kernel,
        grid=(rows // tm,),
        in_specs=[
            pl.BlockSpec((tm, ATTN_WIDTH), row),
            pl.BlockSpec((tm, SSM_WIDTH), row),
            pl.BlockSpec((tm, D_MODEL), row),
            pl.BlockSpec((SSM_WIDTH, SSM_WIDTH), fixed),
            pl.BlockSpec((1, SSM_WIDTH), fixed),
            pl.BlockSpec((1, SSM_WIDTH), fixed),
            pl.BlockSpec((D_MODEL, D_MODEL), fixed),
            pl.BlockSpec((1, D_MODEL), fixed),
        ],
        out_specs=[pl.BlockSpec((tm, D_MODEL), row), pl.BlockSpec((tm, D_MODEL), row)],
        out_shape=[jax.ShapeDtypeStruct((rows, D_MODEL), F32),
                   jax.ShapeDtypeStruct((rows, D_MODEL), BF16)],
        compiler_params=pltpu.CompilerParams(
            dimension_semantics=("parallel",), vmem_limit_bytes=VMEM_LIMIT),
        name="outproj",
    )(attn_n, g_tok, x, w_glu, b_glu, ssm_gain, w_out, mlp_gain)


def _mlp_kernel(xn_ref, x_ref, wu_ref, wd_ref, o_ref):
    f = pl.program_id(1)
    a = jnp.dot(xn_ref[...], wu_ref[...], preferred_element_type=F32)
    a = jnp.square(jnp.maximum(a, 0.0)).astype(BF16)
    y = jnp.dot(a, wd_ref[...], preferred_element_type=F32)

    @pl.when(f == 0)
    def _():
        o_ref[...] = x_ref[...] + y

    @pl.when(f > 0)
    def _():
        o_ref[...] += y


def _mlp(xn, x, w_up, w_down):
    rows = x.shape[0]
    tm = _row_tile(rows, 768)
    tf = 512
    return pl.pallas_call(
        _mlp_kernel,
        grid=(rows // tm, D_FF // tf),
        in_specs=[
            pl.BlockSpec((tm, D_MODEL), lambda i, f: (i, 0)),
            pl.BlockSpec((tm, D_MODEL), lambda i, f: (i, 0)),
            pl.BlockSpec((D_MODEL, tf), lambda i, f: (0, f)),
            pl.BlockSpec((tf, D_MODEL), lambda i, f: (f, 0)),
        ],
        out_specs=pl.BlockSpec((tm, D_MODEL), lambda i, f: (i, 0)),
        out_shape=jax.ShapeDtypeStruct((rows, D_MODEL), F32),
        compiler_params=pltpu.CompilerParams(
            dimension_semantics=("parallel", "arbitrary"), vmem_limit_bytes=VMEM_LIMIT),
        name="mlp",
    )(xn, x, w_up, w_down)


def kernel(x, meta_tokens, norm_mix_g, w_in, q_norm_g, k_norm_g, attn_sinks, ssm_lambda_re, ssm_lambda_im, ssm_log_step, ssm_b_re, ssm_b_im, ssm_c_re, ssm_c_im, ssm_d, w_glu, b_glu, attn_out_g, ssm_out_g, w_out, norm_mlp_g, w_up, w_down):
    b, seq, d = x.shape
    depth = w_in.shape[0]
    assert d == D_MODEL
    length = N_META + seq
    lp = -(-length // BLOCK) * BLOCK
    rows = b * lp

    meta = jnp.broadcast_to(meta_tokens.astype(x.dtype)[None], (b, N_META, d))
    pad = jnp.zeros((b, lp - length, d), x.dtype)
    h_res = jnp.concatenate([meta, x, pad], axis=1).reshape(rows, d)

    w_in_b = w_in.astype(BF16)
    w_glu_b = w_glu.astype(BF16)
    w_out_b = w_out.astype(BF16)
    w_up_b = w_up.astype(BF16)
    w_down_b = w_down.astype(BF16)
    ktr, bb, cc, pw, a16 = _ssm_tables(
        ssm_lambda_re, ssm_lambda_im, ssm_log_step, ssm_b_re, ssm_b_im, ssm_c_re, ssm_c_im)

    head_id = jnp.arange(KV_WIDTH) // HEAD_DIM
    head_ones = (head_id[:, None] == head_id[None, :]).astype(BF16)
    slopes = jnp.exp2(-8.0 * jnp.arange(1, N_HEADS + 1, dtype=F32) / N_HEADS)
    attn_bias, attn_off = _attn_bias(slopes)

    row_vec = lambda v: v.astype(F32).reshape(1, -1)
    for l in range(depth):
        q_gain = jnp.tile(row_vec(q_norm_g[l]), (1, N_KV_HEADS)) * (1.0 / math.sqrt(HEAD_DIM))
        k_gain = jnp.tile(row_vec(k_norm_g[l]), (1, N_KV_HEADS))
        qkv, u_tok = _inproj(h_res, row_vec(norm_mix_g[l]), w_in_b[l], head_ones, q_gain, k_gain)
        sink_row = jnp.broadcast_to(
            attn_sinks[l].astype(F32).reshape(N_KV_HEADS, 1, KV_GROUP, 1),
            (N_KV_HEADS, 1, KV_GROUP, SUB)).reshape(N_KV_HEADS, 1, KV_GROUP * SUB)
        attn_n = _attention(qkv.reshape(b, lp, QKV_WIDTH), attn_bias, attn_off, sink_row,
                            row_vec(attn_out_g[l]))
        g_tok = _ssm(u_tok, ktr[l], bb[l], cc[l], pw[l], a16[l], row_vec(ssm_d[l]), b)
        h_res, xn = _outproj(attn_n.reshape(rows, ATTN_WIDTH), g_tok, h_res, w_glu_b[l],
                             row_vec(b_glu[l]), row_vec(ssm_out_g[l]), w_out_b[l],
                             row_vec(norm_mlp_g[l]))
        h_res = _mlp(xn, h_res, w_up_b[l], w_down_b[l])
    return h_res.reshape(b, lp, d)[:, N_META:length]
```

```python
import functools
import math

import jax
import jax.numpy as jnp
from jax import lax
from jax.experimental import pallas as pl
from jax.experimental.pallas import tpu as pltpu

D_MODEL = 2048
N_META = 16
HEAD_DIM = 64
ATTN_WIDTH = D_MODEL // 2
N_HEADS = ATTN_WIDTH // HEAD_DIM
N_KV_HEADS = N_HEADS // 4
KV_GROUP = N_HEADS // N_KV_HEADS
KV_WIDTH = N_KV_HEADS * HEAD_DIM
SSM_WIDTH = D_MODEL - ATTN_WIDTH
SSM_GROUP_CH = 16
SSM_GROUPS = SSM_WIDTH // SSM_GROUP_CH
SSM_STATE = 64
WINDOW = 128
BLOCK = 128
D_FF = 4 * D_MODEL
IN_WIDTH = ATTN_WIDTH + 2 * KV_WIDTH + SSM_WIDTH
NORM_EPS = 1e-6
NEG_INF = -1e30

CHUNK = 16
CHUNK_W = CHUNK * SSM_GROUP_CH
GROUPS_PER_STEP = 8
RELAYOUT_UNROLL = 7

QKV_WIDTH = ATTN_WIDTH + 2 * KV_WIDTH
COL_K = ATTN_WIDTH
COL_V = ATTN_WIDTH + KV_WIDTH
LANES = 128
BLOCKS_PER_VREG = LANES // SSM_GROUP_CH

VMEM_LIMIT = 48 * 1024 * 1024
MLP_ROWS = 768
MLP_FF_TILE = 1024
MLP_FF_TILE_FIRST = 512
BIG_VMEM_LIMIT = 56 * 1024 * 1024

F32 = jnp.float32
BF16 = jnp.bfloat16


def _row_tile(rows, target):
    k = rows // BLOCK
    best = 1
    for d in range(1, k + 1):
        if k % d == 0 and d * BLOCK <= target:
            best = d
    return best * BLOCK


def _rms(x, gain):
    ms = jnp.mean(x * x, axis=-1, keepdims=True)
    return x * lax.rsqrt(ms + NORM_EPS) * gain


def _head_norm(x, head_ones, gain):
    x2 = x * x
    hi = x2.astype(BF16)
    lo = (x2 - hi.astype(F32)).astype(BF16)
    ssq = (jnp.dot(hi, head_ones, preferred_element_type=F32)
           + jnp.dot(lo, head_ones, preferred_element_type=F32))
    return x * lax.rsqrt(ssq * (1.0 / HEAD_DIM) + NORM_EPS) * gain


def _inproj_kernel(x_ref, g_ref, w_ref, ones_ref, qg_ref, kg_ref, qkv_ref, u_ref, w_bf):
    @pl.when(pl.program_id(0) == 0)
    def _():
        w_bf[...] = w_ref[...].astype(BF16)

    h = _rms(x_ref[...], g_ref[...]).astype(BF16)
    proj = jnp.dot(h, w_bf[...], preferred_element_type=F32)
    head_ones = ones_ref[...]
    for blk in range(ATTN_WIDTH // KV_WIDTH):
        cols = slice(blk * KV_WIDTH, (blk + 1) * KV_WIDTH)
        qkv_ref[:, cols] = _head_norm(proj[:, cols], head_ones, qg_ref[...]).astype(qkv_ref.dtype)
    qkv_ref[:, COL_K:COL_V] = _head_norm(
        proj[:, COL_K:COL_V], head_ones, kg_ref[...]).astype(qkv_ref.dtype)
    qkv_ref[:, COL_V:QKV_WIDTH] = proj[:, COL_V:QKV_WIDTH].astype(qkv_ref.dtype)
    u_ref[...] = proj[:, QKV_WIDTH:]


def _inproj(x, gain, w_all, layer, head_ones, q_gain, k_gain):
    rows = x.shape[0]
    tm = _row_tile(rows, 384)
    fixed = lambda i: (0, 0)
    return pl.pallas_call(
        _inproj_kernel,
        grid=(rows // tm,),
        in_specs=[
            pl.BlockSpec((tm, D_MODEL), lambda i: (i, 0)),
            pl.BlockSpec((1, D_MODEL), fixed),
            pl.BlockSpec((None, D_MODEL, IN_WIDTH), lambda i: (layer, 0, 0)),
            pl.BlockSpec((KV_WIDTH, KV_WIDTH), fixed),
            pl.BlockSpec((1, KV_WIDTH), fixed),
            pl.BlockSpec((1, KV_WIDTH), fixed),
        ],
        out_specs=[pl.BlockSpec((tm, QKV_WIDTH), lambda i: (i, 0)),
                   pl.BlockSpec((tm, SSM_WIDTH), lambda i: (i, 0))],
        out_shape=[jax.ShapeDtypeStruct((rows, QKV_WIDTH), BF16),
                   jax.ShapeDtypeStruct((rows, SSM_WIDTH), F32)],
        scratch_shapes=[pltpu.VMEM((D_MODEL, IN_WIDTH), BF16)],
        compiler_params=pltpu.CompilerParams(
            dimension_semantics=("arbitrary",), vmem_limit_bytes=BIG_VMEM_LIMIT),
        name="inproj",
    )(x, gain, w_all, head_ones, q_gain, k_gain)


SUB = 64
BAND = WINDOW + SUB
KEYS = 2 * BLOCK


def _attn_bias(slopes):
    j = jnp.arange(KEYS)[:, None]
    i = jnp.arange(SUB)[None, :]
    is_band = j < BAND
    is_meta = (j >= BAND) & (j < BAND + N_META)
    dist = WINDOW + i - j
    sl = slopes.reshape(N_KV_HEADS, 1, KV_GROUP, 1)
    dist_f = dist.astype(F32)[None, :, None, :]
    band = jnp.where(((dist >= 0) & (dist < WINDOW))[None, :, None, :], -sl * dist_f, NEG_INF)
    meta = -sl * (i - (j - BAND)).astype(F32)[None, :, None, :]
    bias = jnp.where(is_band[None, :, None, :], band,
                     jnp.where(is_meta[None, :, None, :], meta, NEG_INF))
    lanes = KV_GROUP * SUB
    off_coef = jnp.broadcast_to(-sl, (N_KV_HEADS, N_META, KV_GROUP, SUB))
    return bias.reshape(N_KV_HEADS, KEYS, lanes), off_coef.reshape(N_KV_HEADS, N_META, lanes)


def _attn_kernel(q_ref, kp_ref, kc_ref, vp_ref, vc_ref, km_ref, vm_ref,
                 bias_ref, offc_ref, sink_ref, og_ref, o_ref, acc):
    n = pl.program_id(1)
    lanes = KV_GROUP * SUB
    ones = jnp.ones((KEYS, 8), BF16)
    fill = KEYS - BAND - N_META

    def body(early):
        chains = [(sb, kh) for sb in range(BLOCK // SUB) for kh in range(N_KV_HEADS)]
        scores, values = [], []
        for sb, kh in chains:
            start = n * BLOCK + sb * SUB
            lo = sb * SUB
            ks = slice(kh * HEAD_DIM, (kh + 1) * HEAD_DIM)
            k_cat = jnp.concatenate([kp_ref[:, ks], kc_ref[:, ks]], axis=0)
            v_cat = jnp.concatenate([vp_ref[:, ks], vc_ref[:, ks]], axis=0)
            k_win = jnp.concatenate([k_cat[lo:lo + BAND], km_ref[:, ks], k_cat[:fill]], axis=0)
            v_win = jnp.concatenate([v_cat[lo:lo + BAND], vm_ref[:, ks], v_cat[:fill]], axis=0)
            values.append(jnp.concatenate([v_win, ones], axis=1))
            q_s = jnp.concatenate(
                [q_ref[lo:lo + SUB, (kh * KV_GROUP + g) * HEAD_DIM:(kh * KV_GROUP + g + 1) * HEAD_DIM]
                 for g in range(KV_GROUP)], axis=0)
            s = lax.dot_general(k_win, q_s, (((1,), (1,)), ((), ())),
                                preferred_element_type=F32)
            s = s + bias_ref[kh]
            s = jnp.concatenate([s[:BAND], s[BAND:BAND + N_META] + offc_ref[kh] * start.astype(F32),
                                 s[BAND + N_META:]], axis=0)
            if early:
                j = lax.broadcasted_iota(jnp.int32, (KEYS, lanes), 0)
                i = lax.broadcasted_iota(jnp.int32, (KEYS, lanes), 1) % SUB
                ok = jnp.where(j < BAND, start - WINDOW + j - N_META, start + i - (j - BAND)) >= 0
                s = jnp.where(ok, s, NEG_INF)
            scores.append(s)
        maxes = [jnp.maximum(jnp.max(s, axis=0, keepdims=True), sink_ref[kh])
                 for s, (sb, kh) in zip(scores, chains)]
        probs = [jnp.exp(s - m).astype(BF16) for s, m in zip(scores, maxes)]
        for (sb, kh), p, m, v_ext in zip(chains, probs, maxes, values):
            o_ext = lax.dot_general(v_ext, p, (((0,), (0,)), ((), ())),
                                    preferred_element_type=F32)
            denom = o_ext[HEAD_DIM:HEAD_DIM + 1] + jnp.exp(sink_ref[kh] - m)
            o = o_ext[:HEAD_DIM] / denom
            for g in range(KV_GROUP):
                h = kh * KV_GROUP + g
                acc[h * HEAD_DIM:(h + 1) * HEAD_DIM, sb * SUB:(sb + 1) * SUB] = o[:, g * SUB:(g + 1) * SUB]
        a_t = acc[...]
        ms = jnp.mean(a_t * a_t, axis=0, keepdims=True)
        a_n = a_t * lax.rsqrt(ms + NORM_EPS)
        o_ref[...] = (a_n.T * og_ref[...]).astype(o_ref.dtype)

    @pl.when(n >= 2)
    def _():
        body(False)

    @pl.when(n < 2)
    def _():
        body(True)


def _attention(qkv3, bias, off_coef, sink_row, out_gain):
    b, lp, _ = qkv3.shape
    nb = lp // BLOCK
    kcol, vcol = COL_K // KV_WIDTH, COL_V // KV_WIDTH
    prev = lambda n: jnp.maximum(n - 1, 0)
    fixed3 = lambda bi, n: (0, 0, 0)
    return pl.pallas_call(
        _attn_kernel,
        grid=(b, nb),
        in_specs=[
            pl.BlockSpec((None, BLOCK, ATTN_WIDTH), lambda bi, n: (bi, n, 0)),
            pl.BlockSpec((None, BLOCK, KV_WIDTH), lambda bi, n: (bi, prev(n), kcol)),
            pl.BlockSpec((None, BLOCK, KV_WIDTH), lambda bi, n: (bi, n, kcol)),
            pl.BlockSpec((None, BLOCK, KV_WIDTH), lambda bi, n: (bi, prev(n), vcol)),
            pl.BlockSpec((None, BLOCK, KV_WIDTH), lambda bi, n: (bi, n, vcol)),
            pl.BlockSpec((None, N_META, KV_WIDTH), lambda bi, n: (bi, 0, kcol)),
            pl.BlockSpec((None, N_META, KV_WIDTH), lambda bi, n: (bi, 0, vcol)),
            pl.BlockSpec(bias.shape, fixed3),
            pl.BlockSpec(off_coef.shape, fixed3),
            pl.BlockSpec(sink_row.shape, fixed3),
            pl.BlockSpec((1, ATTN_WIDTH), lambda bi, n: (0, 0)),
        ],
        out_specs=pl.BlockSpec((None, BLOCK, ATTN_WIDTH), lambda bi, n: (bi, n, 0)),
        out_shape=jax.ShapeDtypeStruct((b, lp, ATTN_WIDTH), BF16),
        compiler_params=pltpu.CompilerParams(
            dimension_semantics=("parallel", "parallel"), vmem_limit_bytes=VMEM_LIMIT),
        scratch_shapes=[pltpu.VMEM((ATTN_WIDTH, BLOCK), F32)],
        name="attention",
    )(qkv3, qkv3, qkv3, qkv3, qkv3, qkv3, qkv3, bias, off_coef, sink_row, out_gain)


def _gelu_tanh(x):
    c = math.sqrt(2.0 / math.pi)
    return 0.5 * x * (1.0 + jnp.tanh(c * (x + 0.044715 * (x * x * x))))


def _block_transpose_stage(arrs, dist, lane_blk):
    nblk = BLOCKS_PER_VREG
    keep = (lane_blk & dist) == 0
    nxt = list(arrs)
    for i in range(nblk):
        if i & dist == 0:
            lo, hi = arrs[i], arrs[i + dist]
            nxt[i] = jnp.where(keep, lo, pltpu.roll(hi, dist * SSM_GROUP_CH, axis=1))
            nxt[i + dist] = jnp.where(keep, pltpu.roll(lo, (nblk - dist) * SSM_GROUP_CH, axis=1), hi)
    return nxt


def _pipelined_block_transpose(n_tiles, load, store, zero, lane_blk):
    def trip(r, carry):
        after1, after2 = carry
        fresh = load(jnp.minimum(r, n_tiles - 1))
        new1 = [_block_transpose_stage(g, 4, lane_blk) for g in fresh]
        new2 = [_block_transpose_stage(g, 2, lane_blk) for g in after1]
        done = [_block_transpose_stage(g, 1, lane_blk) for g in after2]
        store(jnp.maximum(r - 2, 0), done)
        return new1, new2

    groups = CHUNK // BLOCKS_PER_VREG
    init = [[zero] * BLOCKS_PER_VREG for _ in range(groups)]
    lax.fori_loop(0, n_tiles + 2, trip, (init, init), unroll=RELAYOUT_UNROLL)


def _ssm_kernel(*refs, n_chunks, n_batch):
    u_refs = refs[:CHUNK]
    (ktr_ref, bb_ref, cc_ref, pw_ref, a_ref, d_ref, o_ref,
     lhs, yscr, m_scr, win_scr, wsw_scr, wct_scr, zin, zsw, st) = refs[CHUNK:]
    nblk = BLOCKS_PER_VREG
    sw = 2 * SSM_STATE
    rc = n_batch * n_chunks
    half = CHUNK // nblk

    lane256 = lax.broadcasted_iota(jnp.int32, (CHUNK, CHUNK_W), 1)
    for g in range(GROUPS_PER_STEP):
        base = ktr_ref[g]
        for sblk in range(CHUNK):
            row = base if sblk == 0 else jnp.where(
                lane256 >= sblk * SSM_GROUP_CH, pltpu.roll(base, sblk * SSM_GROUP_CH, axis=1), 0.0)
            m_scr[g, sblk * CHUNK:(sblk + 1) * CHUNK, :] = row.astype(BF16)
        bb, bbs, bsw, bsws = bb_ref[g, 0], bb_ref[g, 1], bb_ref[g, 2], bb_ref[g, 3]
        cc, ccs = cc_ref[g, 0], cc_ref[g, 1]
        for t in range(CHUNK):
            rows = slice(t * CHUNK, (t + 1) * CHUNK)
            p_re = pw_ref[g, 0, CHUNK - 1 - t:CHUNK - t, :]
            p_im = pw_ref[g, 1, CHUNK - 1 - t:CHUNK - t, :]
            win_scr[g, rows, :] = (p_re * bb + p_im * bbs).astype(BF16)
            wsw_scr[g, rows, :] = (p_re * bsw + p_im * bsws).astype(BF16)
            q_re = pw_ref[g, 0, t + 1:t + 2, :]
            q_im = pw_ref[g, 1, t + 1:t + 2, :]
            wct_scr[g, rows, :] = (q_re * cc + q_im * ccs).astype(BF16)

    rt_in = 16
    lane_blk_in = lax.broadcasted_iota(jnp.int32, (rt_in // 2, LANES), 1) // SSM_GROUP_CH

    def load_u(r):
        rows = pl.ds(pl.multiple_of(r * rt_in, rt_in), rt_in)
        return [[pltpu.bitcast(u_refs[hf * nblk + tb][rows, :].astype(BF16), jnp.uint32)
                 for tb in range(nblk)] for hf in range(half)]

    def store_lhs(r, groups):
        rows = pl.ds(pl.multiple_of(r * rt_in, rt_in), rt_in)
        for hf in range(half):
            for g in range(nblk):
                col = g * CHUNK_W + hf * LANES
                lhs[rows, col:col + LANES] = pltpu.bitcast(groups[hf][g], BF16)

    _pipelined_block_transpose(rc // rt_in, load_u, store_lhs,
                               jnp.zeros((rt_in // 2, LANES), jnp.uint32), lane_blk_in)

    for g in range(GROUPS_PER_STEP):
        u_g = lhs[:, g * CHUNK_W:(g + 1) * CHUNK_W]
        zin[:, g * sw:(g + 1) * sw] = jnp.dot(u_g, win_scr[g], preferred_element_type=F32)
        zsw[:, g * sw:(g + 1) * sw] = jnp.dot(u_g, wsw_scr[g], preferred_element_type=F32)

    a1 = a_ref[0:1, :]
    a2 = a_ref[1:2, :]

    def step(c, carry):
        nxt = []
        for b in range(n_batch):
            s, x = carry[b]
            row = b * n_chunks + c
            st[pl.ds(row, 1), :] = s
            nxt.append((a1 * s + a2 * x + zin[pl.ds(row, 1), :],
                        a1 * x - a2 * s + zsw[pl.ds(row, 1), :]))
        return tuple(nxt)

    zero = jnp.zeros((1, GROUPS_PER_STEP * sw), F32)
    lax.fori_loop(0, n_chunks, step, tuple((zero, zero) for _ in range(n_batch)))

    for g in range(GROUPS_PER_STEP):
        cols = slice(g * CHUNK_W, (g + 1) * CHUNK_W)
        y = jnp.dot(lhs[:, cols], m_scr[g], preferred_element_type=F32)
        y = y + lax.dot_general(st[:, g * sw:(g + 1) * sw].astype(BF16), wct_scr[g],
                                (((1,), (1,)), ((), ())), preferred_element_type=F32)
        yscr[:, cols] = y

    rt_out = 8
    d_row = d_ref[...]
    lane_blk_out = lax.broadcasted_iota(jnp.int32, (rt_out, LANES), 1) // SSM_GROUP_CH

    def load_y(r):
        rows = pl.ds(pl.multiple_of(r * rt_out, rt_out), rt_out)
        return [[yscr[rows, g * CHUNK_W + hf * LANES:g * CHUNK_W + (hf + 1) * LANES]
                 for g in range(nblk)] for hf in range(half)]

    def store_tokens(r, groups):
        rows = pl.ds(pl.multiple_of(r * rt_out, rt_out), rt_out)
        for hf in range(half):
            for tb in range(nblk):
                t = hf * nblk + tb
                y = groups[hf][tb] + d_row * u_refs[t][rows, :]
                o_ref[pl.ds(r * (rt_out * CHUNK) + t, rt_out, stride=CHUNK), :] = _gelu_tanh(y)

    _pipelined_block_transpose(rc // rt_out, load_y, store_tokens,
                               jnp.zeros((rt_out, LANES), F32), lane_blk_out)


def _ssm(u_tok, ktr, bb, cc, pw, a16, d, n_batch):
    rows, width = u_tok.shape
    rc = rows // CHUNK
    n_chunks = rc // n_batch
    assert GROUPS_PER_STEP == BLOCKS_PER_VREG and rc % 16 == 0
    gs = GROUPS_PER_STEP
    bw = gs * CHUNK_W
    sw = gs * 2 * SSM_STATE
    kern = functools.partial(_ssm_kernel, n_chunks=n_chunks, n_batch=n_batch)
    grp = lambda j: (j, 0, 0)
    grp4 = lambda j: (j, 0, 0, 0)
    u_chunks = u_tok.reshape(rc, CHUNK * width)
    lane_blocks = width // LANES
    u_specs = [pl.BlockSpec((rc, LANES), functools.partial(lambda j, t: (0, t * lane_blocks + j), t=t))
               for t in range(CHUNK)]
    return pl.pallas_call(
        kern,
        grid=(SSM_GROUPS // gs,),
        in_specs=u_specs + [
            pl.BlockSpec((gs,) + ktr.shape[1:], grp),
            pl.BlockSpec((gs,) + bb.shape[1:], grp4),
            pl.BlockSpec((gs,) + cc.shape[1:], grp4),
            pl.BlockSpec((gs,) + pw.shape[1:], grp4),
            pl.BlockSpec((2, sw), lambda j: (0, j)),
            pl.BlockSpec((1, LANES), lambda j: (0, j)),
        ],
        out_specs=pl.BlockSpec((rows, LANES), lambda j: (0, j)),
        out_shape=jax.ShapeDtypeStruct((rows, width), F32),
        scratch_shapes=[
            pltpu.VMEM((rc, bw), BF16),
            pltpu.VMEM((rc, bw), F32),
            pltpu.VMEM((gs, CHUNK_W, CHUNK_W), BF16),
            pltpu.VMEM((gs, CHUNK_W, 2 * SSM_STATE), BF16),
            pltpu.VMEM((gs, CHUNK_W, 2 * SSM_STATE), BF16),
            pltpu.VMEM((gs, CHUNK_W, 2 * SSM_STATE), BF16),
            pltpu.VMEM((rc, sw), F32),
            pltpu.VMEM((rc, sw), F32),
            pltpu.VMEM((rc, sw), F32),
        ],
        compiler_params=pltpu.CompilerParams(
            dimension_semantics=("parallel",), vmem_limit_bytes=VMEM_LIMIT),
        name="ssm",
    )(*([u_chunks] * CHUNK), ktr, bb, cc, pw, a16, d)


def _ssm_tables(lam_re, lam_im, log_step, b_re, b_im, c_re, c_im):
    depth = lam_re.shape[0]
    g, p, h, t = SSM_GROUPS, SSM_STATE, SSM_GROUP_CH, CHUNK
    lam = lax.complex(lam_re.astype(F32), lam_im.astype(F32))
    delta = jnp.exp(log_step.astype(F32))[..., None]
    ld = lam * delta
    lam_bar = jnp.exp(ld)
    b_bar = ((lam_bar - 1.0) / lam)[..., None] * lax.complex(b_re.astype(F32), b_im.astype(F32))
    c_c = lax.complex(c_re.astype(F32), c_im.astype(F32))
    tau = jnp.arange(t + 1, dtype=F32)
    powers = jnp.exp(ld[:, :, None, :] * tau[None, None, :, None])

    kt = jnp.real(jnp.einsum('dghp,dgtp,dgpk->dgkth', c_c, powers[:, :, :t], b_bar))
    ktr = kt.reshape(depth, g, h, t * h)

    cat = lambda x, y: jnp.concatenate([x, y], axis=-1)
    br, bi = jnp.real(b_bar).swapaxes(-1, -2), jnp.imag(b_bar).swapaxes(-1, -2)
    bb = jnp.stack([cat(br, bi), cat(-bi, br), cat(bi, br), cat(br, -bi)], axis=2)
    cr, ci = jnp.real(c_c), jnp.imag(c_c)
    cc = jnp.stack([cat(cr, -ci), cat(-ci, -cr)], axis=2)
    pr, pi = jnp.real(powers), jnp.imag(powers)
    pw = jnp.stack([cat(pr, pr), cat(pi, pi)], axis=2)
    pw = jnp.pad(pw, ((0, 0), (0, 0), (0, 0), (0, 24 - (t + 1)), (0, 0)))
    ar, ai = pr[:, :, t], pi[:, :, t]
    a16 = jnp.stack([cat(ar, ar).reshape(depth, g * 2 * p),
                     cat(-ai, ai).reshape(depth, g * 2 * p)], axis=1)
    return ktr, bb, cc, pw, a16


def _outproj_kernel(a_ref, g_ref, x_ref, wglu_f32, bglu_ref, sg_ref, wout_f32, ng_ref,
                    o_ref, xn_ref, wglu_ref, wout_ref):
    @pl.when(pl.program_id(0) == 0)
    def _():
        wglu_ref[...] = wglu_f32[...].astype(BF16)
        wout_ref[...] = wout_f32[...].astype(BF16)

    g = g_ref[...]
    z = jnp.dot(g.astype(BF16), wglu_ref[...], preferred_element_type=F32) + bglu_ref[...]
    s = g * jax.nn.sigmoid(z)
    s_n = _rms(s, sg_ref[...]).astype(BF16)
    y = jnp.dot(a_ref[...], wout_ref[:ATTN_WIDTH, :], preferred_element_type=F32)
    y = y + jnp.dot(s_n, wout_ref[ATTN_WIDTH:, :], preferred_element_type=F32)
    h = x_ref[...] + y
    o_ref[...] = h
    xn_ref[...] = _rms(h, ng_ref[...]).astype(xn_ref.dtype)


def _outproj(attn_n, g_tok, x, w_glu_all, b_glu, ssm_gain, w_out_all, mlp_gain, layer):
    rows = x.shape[0]
    tm = _row_tile(rows, 384)
    row = lambda i: (i, 0)
    fixed = lambda i: (0, 0)
    return pl.pallas_call(
        _outproj_kernel,
        grid=(rows // tm,),
        in_specs=[
            pl.BlockSpec((tm, ATTN_WIDTH), row),
            pl.BlockSpec((tm, SSM_WIDTH), row),
            pl.BlockSpec((tm, D_MODEL), row),
            pl.BlockSpec((None, SSM_WIDTH, SSM_WIDTH), lambda i: (layer, 0, 0)),
            pl.BlockSpec((1, SSM_WIDTH), fixed),
            pl.BlockSpec((1, SSM_WIDTH), fixed),
            pl.BlockSpec((None, D_MODEL, D_MODEL), lambda i: (layer, 0, 0)),
            pl.BlockSpec((1, D_MODEL), fixed),
        ],
        out_specs=[pl.BlockSpec((tm, D_MODEL), row), pl.BlockSpec((tm, D_MODEL), row)],
        out_shape=[jax.ShapeDtypeStruct((rows, D_MODEL), F32),
                   jax.ShapeDtypeStruct((rows, D_MODEL), BF16)],
        scratch_shapes=[pltpu.VMEM((SSM_WIDTH, SSM_WIDTH), BF16), pltpu.VMEM((D_MODEL, D_MODEL), BF16)],
        compiler_params=pltpu.CompilerParams(
            dimension_semantics=("arbitrary",), vmem_limit_bytes=BIG_VMEM_LIMIT),
        name="outproj",
    )(attn_n, g_tok, x, w_glu_all, b_glu, ssm_gain, w_out_all, mlp_gain)


def _mlp_tile(xn_ref, x_ref, w_up, w_down, o_ref):
    a = jnp.dot(xn_ref[...], w_up, preferred_element_type=F32)
    a = jnp.square(jnp.maximum(a, 0.0)).astype(BF16)
    o_ref[...] += jnp.dot(a, w_down, preferred_element_type=F32)


def _mlp_first_kernel(xn_ref, x_ref, wu_ref, wd_ref, o_ref, wu_bf_ref, wd_bf_ref):
    @pl.when(pl.program_id(0) == 0)
    def _():
        o_ref[...] = x_ref[...]

    wu_bf_ref[...] = wu_ref[...].astype(BF16)
    wd_bf_ref[...] = wd_ref[...].astype(BF16)
    _mlp_tile(xn_ref, x_ref, wu_bf_ref[...], wd_bf_ref[...], o_ref)


def _mlp_rest_kernel(xn_ref, x_ref, wu_ref, wd_ref, partial_ref, o_ref):
    del partial_ref

    @pl.when(pl.program_id(1) == 0)
    def _():
        o_ref[...] = x_ref[...]

    _mlp_tile(xn_ref, x_ref, wu_ref[...], wd_ref[...], o_ref)


def _mlp(xn, x, w_up_all, w_down_all, layer):
    rows = x.shape[0]
    tm = _row_tile(rows, MLP_ROWS)
    tf_first, tf = MLP_FF_TILE_FIRST, MLP_FF_TILE
    first_rows = lambda f: (0, 0)
    partial, w_up_bf, w_down_bf = pl.pallas_call(
        _mlp_first_kernel,
        grid=(D_FF // tf_first,),
        in_specs=[
            pl.BlockSpec((tm, D_MODEL), first_rows),
            pl.BlockSpec((tm, D_MODEL), first_rows),
            pl.BlockSpec((None, D_MODEL, tf_first), lambda f: (layer, 0, f)),
            pl.BlockSpec((None, tf_first, D_MODEL), lambda f: (layer, f, 0)),
        ],
        out_specs=[pl.BlockSpec((tm, D_MODEL), first_rows),
                   pl.BlockSpec((D_MODEL, tf_first), lambda f: (0, f)),
                   pl.BlockSpec((tf_first, D_MODEL), lambda f: (f, 0))],
        out_shape=[jax.ShapeDtypeStruct((rows, D_MODEL), F32),
                   jax.ShapeDtypeStruct((D_MODEL, D_FF), BF16),
                   jax.ShapeDtypeStruct((D_FF, D_MODEL), BF16)],
        compiler_params=pltpu.CompilerParams(
            dimension_semantics=("arbitrary",), vmem_limit_bytes=BIG_VMEM_LIMIT),
        name="mlp_first",
    )(xn, x, w_up_all, w_down_all)
    if rows == tm:
        return partial
    return pl.pallas_call(
        _mlp_rest_kernel,
        grid=(rows // tm - 1, D_FF // tf),
        in_specs=[
            pl.BlockSpec((tm, D_MODEL), lambda i, f: (i + 1, 0)),
            pl.BlockSpec((tm, D_MODEL), lambda i, f: (i + 1, 0)),
            pl.BlockSpec((D_MODEL, tf), lambda i, f: (0, f)),
            pl.BlockSpec((tf, D_MODEL), lambda i, f: (f, 0)),
            pl.BlockSpec(memory_space=pl.ANY),
        ],
        out_specs=pl.BlockSpec((tm, D_MODEL), lambda i, f: (i + 1, 0)),
        out_shape=jax.ShapeDtypeStruct((rows, D_MODEL), F32),
        input_output_aliases={4: 0},
        compiler_params=pltpu.CompilerParams(
            dimension_semantics=("parallel", "arbitrary"), vmem_limit_bytes=BIG_VMEM_LIMIT),
        name="mlp_rest",
    )(xn, x, w_up_bf, w_down_bf, partial)


def kernel(x, meta_tokens, norm_mix_g, w_in, q_norm_g, k_norm_g, attn_sinks, ssm_lambda_re, ssm_lambda_im, ssm_log_step, ssm_b_re, ssm_b_im, ssm_c_re, ssm_c_im, ssm_d, w_glu, b_glu, attn_out_g, ssm_out_g, w_out, norm_mlp_g, w_up, w_down):
    b, seq, d = x.shape
    depth = w_in.shape[0]
    assert d == D_MODEL
    length = N_META + seq
    lp = -(-length // BLOCK) * BLOCK
    rows = b * lp

    meta = jnp.broadcast_to(meta_tokens.astype(x.dtype)[None], (b, N_META, d))
    pad = jnp.zeros((b, lp - length, d), x.dtype)
    h_res = jnp.concatenate([meta, x, pad], axis=1).reshape(rows, d)

    ktr, bb, cc, pw, a16 = _ssm_tables(
        ssm_lambda_re, ssm_lambda_im, ssm_log_step, ssm_b_re, ssm_b_im, ssm_c_re, ssm_c_im)

    head_id = jnp.arange(KV_WIDTH) // HEAD_DIM
    head_ones = (head_id[:, None] == head_id[None, :]).astype(BF16)
    slopes = jnp.exp2(-8.0 * jnp.arange(1, N_HEADS + 1, dtype=F32) / N_HEADS)
    attn_bias, attn_off = _attn_bias(slopes)

    row_vec = lambda v: v.astype(F32).reshape(1, -1)
    for l in range(depth):
        q_gain = jnp.tile(row_vec(q_norm_g[l]), (1, N_KV_HEADS)) * (1.0 / math.sqrt(HEAD_DIM))
        k_gain = jnp.tile(row_vec(k_norm_g[l]), (1, N_KV_HEADS))
        qkv, u_tok = _inproj(h_res, row_vec(norm_mix_g[l]), w_in, l, head_ones, q_gain, k_gain)
        sink_row = jnp.broadcast_to(
            attn_sinks[l].astype(F32).reshape(N_KV_HEADS, 1, KV_GROUP, 1),
            (N_KV_HEADS, 1, KV_GROUP, SUB)).reshape(N_KV_HEADS, 1, KV_GROUP * SUB)
        attn_n = _attention(qkv.reshape(b, lp, QKV_WIDTH), attn_bias, attn_off, sink_row,
                            row_vec(attn_out_g[l]))
        g_tok = _ssm(u_tok, ktr[l], bb[l], cc[l], pw[l], a16[l], row_vec(ssm_d[l]), b)
        h_res, xn = _outproj(attn_n.reshape(rows, ATTN_WIDTH), g_tok, h_res, w_glu,
                             row_vec(b_glu[l]), row_vec(ssm_out_g[l]), w_out,
                             row_vec(norm_mlp_g[l]), l)
        h_res = _mlp(xn, h_res, w_up, w_down, l)
    return h_res.reshape(b, lp, d)[:, N_META:length]
```

```python
import functools
import math

import jax
import jax.numpy as jnp
from jax import lax
from jax.experimental import pallas as pl
from jax.experimental.pallas import tpu as pltpu

D_MODEL = 2048
N_META = 16
HEAD_DIM = 64
ATTN_WIDTH = D_MODEL // 2
N_HEADS = ATTN_WIDTH // HEAD_DIM
N_KV_HEADS = N_HEADS // 4
KV_GROUP = N_HEADS // N_KV_HEADS
KV_WIDTH = N_KV_HEADS * HEAD_DIM
SSM_WIDTH = D_MODEL - ATTN_WIDTH
SSM_GROUP_CH = 16
SSM_GROUPS = SSM_WIDTH // SSM_GROUP_CH
SSM_STATE = 64
WINDOW = 128
BLOCK = 128
D_FF = 4 * D_MODEL
IN_WIDTH = ATTN_WIDTH + 2 * KV_WIDTH + SSM_WIDTH
NORM_EPS = 1e-6
NEG_INF = -1e30

CHUNK = 16
CHUNK_W = CHUNK * SSM_GROUP_CH
GROUPS_PER_STEP = 8
RELAYOUT_UNROLL = 7

QKV_WIDTH = ATTN_WIDTH + 2 * KV_WIDTH
COL_K = ATTN_WIDTH
COL_V = ATTN_WIDTH + KV_WIDTH
LANES = 128
BLOCKS_PER_VREG = LANES // SSM_GROUP_CH

VMEM_LIMIT = 48 * 1024 * 1024
MLP_ROWS = 768
MLP_FF_TILE = 1024
MLP_FF_TILE_FIRST = 512
BIG_VMEM_LIMIT = 56 * 1024 * 1024

F32 = jnp.float32
BF16 = jnp.bfloat16


def _row_tile(rows, target):
    k = rows // BLOCK
    best = 1
    for d in range(1, k + 1):
        if k % d == 0 and d * BLOCK <= target:
            best = d
    return best * BLOCK


def _rms(x, gain):
    ms = jnp.mean(x * x, axis=-1, keepdims=True)
    return x * lax.rsqrt(ms + NORM_EPS) * gain


def _head_norm(x, head_ones, gain):
    x2 = x * x
    hi = x2.astype(BF16)
    lo = (x2 - hi.astype(F32)).astype(BF16)
    ssq = (jnp.dot(hi, head_ones, preferred_element_type=F32)
           + jnp.dot(lo, head_ones, preferred_element_type=F32))
    return x * lax.rsqrt(ssq * (1.0 / HEAD_DIM) + NORM_EPS) * gain


def _inproj_kernel(x_ref, g_ref, w_ref, ones_ref, qg_ref, kg_ref, qkv_ref, u_ref, w_bf):
    @pl.when(pl.program_id(0) == 0)
    def _():
        w_bf[...] = w_ref[...].astype(BF16)

    h = _rms(x_ref[...], g_ref[...]).astype(BF16)
    proj = jnp.dot(h, w_bf[...], preferred_element_type=F32)
    head_ones = ones_ref[...]
    for blk in range(ATTN_WIDTH // KV_WIDTH):
        cols = slice(blk * KV_WIDTH, (blk + 1) * KV_WIDTH)
        qkv_ref[:, cols] = _head_norm(proj[:, cols], head_ones, qg_ref[...]).astype(qkv_ref.dtype)
    qkv_ref[:, COL_K:COL_V] = _head_norm(
        proj[:, COL_K:COL_V], head_ones, kg_ref[...]).astype(qkv_ref.dtype)
    qkv_ref[:, COL_V:QKV_WIDTH] = proj[:, COL_V:QKV_WIDTH].astype(qkv_ref.dtype)
    u_ref[...] = proj[:, QKV_WIDTH:]


def _inproj(x, gain, w_all, layer, head_ones, q_gain, k_gain):
    rows = x.shape[0]
    tm = _row_tile(rows, 384)
    fixed = lambda i: (0, 0)
    return pl.pallas_call(
        _inproj_kernel,
        grid=(rows // tm,),
        in_specs=[
            pl.BlockSpec((tm, D_MODEL), lambda i: (i, 0)),
            pl.BlockSpec((1, D_MODEL), fixed),
            pl.BlockSpec((None, D_MODEL, IN_WIDTH), lambda i: (layer, 0, 0)),
            pl.BlockSpec((KV_WIDTH, KV_WIDTH), fixed),
            pl.BlockSpec((1, KV_WIDTH), fixed),
            pl.BlockSpec((1, KV_WIDTH), fixed),
        ],
        out_specs=[pl.BlockSpec((tm, QKV_WIDTH), lambda i: (i, 0)),
                   pl.BlockSpec((tm, SSM_WIDTH), lambda i: (i, 0))],
        out_shape=[jax.ShapeDtypeStruct((rows, QKV_WIDTH), BF16),
                   jax.ShapeDtypeStruct((rows, SSM_WIDTH), F32)],
        scratch_shapes=[pltpu.VMEM((D_MODEL, IN_WIDTH), BF16)],
        compiler_params=pltpu.CompilerParams(
            dimension_semantics=("arbitrary",), vmem_limit_bytes=BIG_VMEM_LIMIT),
        name="inproj",
    )(x, gain, w_all, head_ones, q_gain, k_gain)


SUB = 64
BAND = WINDOW + SUB
KEYS = BAND + N_META


def _attn_bias(slopes):
    j = jnp.arange(KEYS)[:, None]
    i = jnp.arange(SUB)[None, :]
    is_band = j < BAND
    dist = WINDOW + i - j
    sl = slopes.reshape(N_KV_HEADS, 1, KV_GROUP, 1)
    dist_f = dist.astype(F32)[None, :, None, :]
    band = jnp.where(((dist >= 0) & (dist < WINDOW))[None, :, None, :], -sl * dist_f, NEG_INF)
    meta = -sl * (i - (j - BAND)).astype(F32)[None, :, None, :]
    bias = jnp.where(is_band[None, :, None, :], band, meta)
    lanes = KV_GROUP * SUB
    off_coef = jnp.broadcast_to(-sl, (N_KV_HEADS, N_META, KV_GROUP, SUB))
    return bias.reshape(N_KV_HEADS, KEYS, lanes), off_coef.reshape(N_KV_HEADS, N_META, lanes)


def _attn_kernel(q_ref, kp_ref, kc_ref, vp_ref, vc_ref, km_ref, vm_ref,
                 bias_ref, offc_ref, sink_ref, og_ref, o_ref, acc):
    n = pl.program_id(1)
    lanes = KV_GROUP * SUB
    ones = jnp.ones((KEYS, 8), BF16)

    def body(early):
        chains = [(sb, kh) for sb in range(BLOCK // SUB) for kh in range(N_KV_HEADS)]
        scores, values = [], []
        for sb, kh in chains:
            start = n * BLOCK + sb * SUB
            lo = sb * SUB
            ks = slice(kh * HEAD_DIM, (kh + 1) * HEAD_DIM)
            k_cat = jnp.concatenate([kp_ref[:, ks], kc_ref[:, ks]], axis=0)
            v_cat = jnp.concatenate([vp_ref[:, ks], vc_ref[:, ks]], axis=0)
            k_win = jnp.concatenate([k_cat[lo:lo + BAND], km_ref[:, ks]], axis=0)
            v_win = jnp.concatenate([v_cat[lo:lo + BAND], vm_ref[:, ks]], axis=0)
            values.append(jnp.concatenate([v_win, ones], axis=1))
            q_s = jnp.concatenate(
                [q_ref[lo:lo + SUB, (kh * KV_GROUP + g) * HEAD_DIM:(kh * KV_GROUP + g + 1) * HEAD_DIM]
                 for g in range(KV_GROUP)], axis=0)
            s = lax.dot_general(k_win, q_s, (((1,), (1,)), ((), ())),
                                preferred_element_type=F32)
            s = s + bias_ref[kh]
            s = jnp.concatenate([s[:BAND], s[BAND:] + offc_ref[kh] * start.astype(F32)], axis=0)
            if early:
                j = lax.broadcasted_iota(jnp.int32, (KEYS, lanes), 0)
                i = lax.broadcasted_iota(jnp.int32, (KEYS, lanes), 1) % SUB
                ok = jnp.where(j < BAND, start - WINDOW + j - N_META, start + i - (j - BAND)) >= 0
                s = jnp.where(ok, s, NEG_INF)
            scores.append(s)
        maxes = [jnp.maximum(jnp.max(s, axis=0, keepdims=True), sink_ref[kh])
                 for s, (sb, kh) in zip(scores, chains)]
        probs = [jnp.exp(s - m).astype(BF16) for s, m in zip(scores, maxes)]
        for (sb, kh), p, m, v_ext in zip(chains, probs, maxes, values):
            o_ext = lax.dot_general(v_ext, p, (((0,), (0,)), ((), ())),
                                    preferred_element_type=F32)
            denom = o_ext[HEAD_DIM:HEAD_DIM + 1] + jnp.exp(sink_ref[kh] - m)
            o = o_ext[:HEAD_DIM] / denom
            for g in range(KV_GROUP):
                h = kh * KV_GROUP + g
                acc[h * HEAD_DIM:(h + 1) * HEAD_DIM, sb * SUB:(sb + 1) * SUB] = o[:, g * SUB:(g + 1) * SUB]
        a_t = acc[...]
        ms = jnp.mean(a_t * a_t, axis=0, keepdims=True)
        a_n = a_t * lax.rsqrt(ms + NORM_EPS)
        o_ref[...] = (a_n.T * og_ref[...]).astype(o_ref.dtype)

    @pl.when(n >= 2)
    def _():
        body(False)

    @pl.when(n < 2)
    def _():
        body(True)


def _attention(qkv3, bias, off_coef, sink_row, out_gain):
    b, lp, _ = qkv3.shape
    nb = lp // BLOCK
    kcol, vcol = COL_K // KV_WIDTH, COL_V // KV_WIDTH
    prev = lambda n: jnp.maximum(n - 1, 0)
    fixed3 = lambda bi, n: (0, 0, 0)
    return pl.pallas_call(
        _attn_kernel,
        grid=(b, nb),
        in_specs=[
            pl.BlockSpec((None, BLOCK, ATTN_WIDTH), lambda bi, n: (bi, n, 0)),
            pl.BlockSpec((None, BLOCK, KV_WIDTH), lambda bi, n: (bi, prev(n), kcol)),
            pl.BlockSpec((None, BLOCK, KV_WIDTH), lambda bi, n: (bi, n, kcol)),
            pl.BlockSpec((None, BLOCK, KV_WIDTH), lambda bi, n: (bi, prev(n), vcol)),
            pl.BlockSpec((None, BLOCK, KV_WIDTH), lambda bi, n: (bi, n, vcol)),
            pl.BlockSpec((None, N_META, KV_WIDTH), lambda bi, n: (bi, 0, kcol)),
            pl.BlockSpec((None, N_META, KV_WIDTH), lambda bi, n: (bi, 0, vcol)),
            pl.BlockSpec(bias.shape, fixed3),
            pl.BlockSpec(off_coef.shape, fixed3),
            pl.BlockSpec(sink_row.shape, fixed3),
            pl.BlockSpec((1, ATTN_WIDTH), lambda bi, n: (0, 0)),
        ],
        out_specs=pl.BlockSpec((None, BLOCK, ATTN_WIDTH), lambda bi, n: (bi, n, 0)),
        out_shape=jax.ShapeDtypeStruct((b, lp, ATTN_WIDTH), BF16),
        compiler_params=pltpu.CompilerParams(
            dimension_semantics=("parallel", "parallel"), vmem_limit_bytes=VMEM_LIMIT),
        scratch_shapes=[pltpu.VMEM((ATTN_WIDTH, BLOCK), F32)],
        name="attention",
    )(qkv3, qkv3, qkv3, qkv3, qkv3, qkv3, qkv3, bias, off_coef, sink_row, out_gain)


def _gelu_tanh(x):
    c = math.sqrt(2.0 / math.pi)
    return 0.5 * x * (1.0 + jnp.tanh(c * (x + 0.044715 * (x * x * x))))


def _block_transpose_stage(arrs, dist, lane_blk):
    nblk = BLOCKS_PER_VREG
    keep = (lane_blk & dist) == 0
    nxt = list(arrs)
    for i in range(nblk):
        if i & dist == 0:
            lo, hi = arrs[i], arrs[i + dist]
            nxt[i] = jnp.where(keep, lo, pltpu.roll(hi, dist * SSM_GROUP_CH, axis=1))
            nxt[i + dist] = jnp.where(keep, pltpu.roll(lo, (nblk - dist) * SSM_GROUP_CH, axis=1), hi)
    return nxt


def _pipelined_block_transpose(n_tiles, load, store, zero, lane_blk):
    def trip(r, carry):
        after1, after2 = carry
        fresh = load(jnp.minimum(r, n_tiles - 1))
        new1 = [_block_transpose_stage(g, 4, lane_blk) for g in fresh]
        new2 = [_block_transpose_stage(g, 2, lane_blk) for g in after1]
        done = [_block_transpose_stage(g, 1, lane_blk) for g in after2]
        store(jnp.maximum(r - 2, 0), done)
        return new1, new2

    groups = CHUNK // BLOCKS_PER_VREG
    init = [[zero] * BLOCKS_PER_VREG for _ in range(groups)]
    lax.fori_loop(0, n_tiles + 2, trip, (init, init), unroll=RELAYOUT_UNROLL)


def _ssm_kernel(u_ref, bb_ref, cc_ref, pw_ref, a_ref, d_ref, o_ref,
                lhs, yscr, m_scr, win_scr, wsw_scr, wct_scr, zin, zsw, st,
                *, n_chunks, n_batch):
    nblk = BLOCKS_PER_VREG
    sw = 2 * SSM_STATE
    rc = n_batch * n_chunks
    half = CHUNK // nblk

    def tok_rows(tile, tile_chunks, t):
        return pl.ds(tile * (tile_chunks * CHUNK) + t, tile_chunks, stride=CHUNK)

    lane256 = lax.broadcasted_iota(jnp.int32, (CHUNK, CHUNK_W), 1)
    for g in range(GROUPS_PER_STEP):
        bb, bbs, bsw, bsws = bb_ref[g, 0], bb_ref[g, 1], bb_ref[g, 2], bb_ref[g, 3]
        cc, ccs = cc_ref[g, 0], cc_ref[g, 1]
        c_pow = [pw_ref[g, 0, tau:tau + 1, :] * cc + pw_ref[g, 1, tau:tau + 1, :] * ccs
                 for tau in range(CHUNK + 1)]
        for t in range(CHUNK):
            rows = slice(t * CHUNK, (t + 1) * CHUNK)
            p_re = pw_ref[g, 0, CHUNK - 1 - t:CHUNK - t, :]
            p_im = pw_ref[g, 1, CHUNK - 1 - t:CHUNK - t, :]
            win_scr[g, rows, :] = (p_re * bb + p_im * bbs).astype(BF16)
            wsw_scr[g, rows, :] = (p_re * bsw + p_im * bsws).astype(BF16)
            wct_scr[g, rows, :] = c_pow[t + 1].astype(BF16)
        base = lax.dot_general(bb, jnp.concatenate(c_pow[:CHUNK], axis=0), (((1,), (1,)), ((), ())),
                               precision=lax.Precision.HIGHEST, preferred_element_type=F32)
        for sblk in range(CHUNK):
            row = base if sblk == 0 else jnp.where(
                lane256 >= sblk * SSM_GROUP_CH, pltpu.roll(base, sblk * SSM_GROUP_CH, axis=1), 0.0)
            m_scr[g, sblk * CHUNK:(sblk + 1) * CHUNK, :] = row.astype(BF16)

    rt_in = 16
    lane_blk_in = lax.broadcasted_iota(jnp.int32, (rt_in // 2, LANES), 1) // SSM_GROUP_CH

    def load_u(r):
        return [[pltpu.bitcast(u_ref[tok_rows(r, rt_in, hf * nblk + tb), :].astype(BF16), jnp.uint32)
                 for tb in range(nblk)] for hf in range(half)]

    def store_lhs(r, groups):
        rows = pl.ds(pl.multiple_of(r * rt_in, rt_in), rt_in)
        for hf in range(half):
            for g in range(nblk):
                col = g * CHUNK_W + hf * LANES
                lhs[rows, col:col + LANES] = pltpu.bitcast(groups[hf][g], BF16)

    _pipelined_block_transpose(rc // rt_in, load_u, store_lhs,
                               jnp.zeros((rt_in // 2, LANES), jnp.uint32), lane_blk_in)

    for g in range(GROUPS_PER_STEP):
        u_g = lhs[:, g * CHUNK_W:(g + 1) * CHUNK_W]
        zin[:, g * sw:(g + 1) * sw] = jnp.dot(u_g, win_scr[g], preferred_element_type=F32)
        zsw[:, g * sw:(g + 1) * sw] = jnp.dot(u_g, wsw_scr[g], preferred_element_type=F32)

    a1 = a_ref[0:1, :]
    a2 = a_ref[1:2, :]

    def step(c, carry):
        nxt = []
        for b in range(n_batch):
            s, x = carry[b]
            row = b * n_chunks + c
            st[pl.ds(row, 1), :] = s
            nxt.append((a1 * s + a2 * x + zin[pl.ds(row, 1), :],
                        a1 * x - a2 * s + zsw[pl.ds(row, 1), :]))
        return tuple(nxt)

    zero = jnp.zeros((1, GROUPS_PER_STEP * sw), F32)
    lax.fori_loop(0, n_chunks, step, tuple((zero, zero) for _ in range(n_batch)))

    for g in range(GROUPS_PER_STEP):
        cols = slice(g * CHUNK_W, (g + 1) * CHUNK_W)
        y = jnp.dot(lhs[:, cols], m_scr[g], preferred_element_type=F32)
        y = y + lax.dot_general(st[:, g * sw:(g + 1) * sw].astype(BF16), wct_scr[g],
                                (((1,), (1,)), ((), ())), preferred_element_type=F32)
        yscr[:, cols] = y

    rt_out = 8
    d_row = d_ref[...]
    lane_blk_out = lax.broadcasted_iota(jnp.int32, (rt_out, LANES), 1) // SSM_GROUP_CH

    def load_y(r):
        rows = pl.ds(pl.multiple_of(r * rt_out, rt_out), rt_out)
        return [[yscr[rows, g * CHUNK_W + hf * LANES:g * CHUNK_W + (hf + 1) * LANES]
                 for g in range(nblk)] for hf in range(half)]

    def store_tokens(r, groups):
        for hf in range(half):
            for tb in range(nblk):
                rows = tok_rows(r, rt_out, hf * nblk + tb)
                o_ref[rows, :] = _gelu_tanh(groups[hf][tb] + d_row * u_ref[rows, :])

    _pipelined_block_transpose(rc // rt_out, load_y, store_tokens,
                               jnp.zeros((rt_out, LANES), F32), lane_blk_out)


def _ssm(u_tok, bb, cc, pw, a16, d, n_batch):
    rows, width = u_tok.shape
    rc = rows // CHUNK
    n_chunks = rc // n_batch
    assert GROUPS_PER_STEP == BLOCKS_PER_VREG and rc % 16 == 0
    gs = GROUPS_PER_STEP
    bw = gs * CHUNK_W
    sw = gs * 2 * SSM_STATE
    kern = functools.partial(_ssm_kernel, n_chunks=n_chunks, n_batch=n_batch)
    grp4 = lambda j: (j, 0, 0, 0)
    return pl.pallas_call(
        kern,
        grid=(SSM_GROUPS // gs,),
        in_specs=[
            pl.BlockSpec((rows, LANES), lambda j: (0, j)),
            pl.BlockSpec((gs,) + bb.shape[1:], grp4),
            pl.BlockSpec((gs,) + cc.shape[1:], grp4),
            pl.BlockSpec((gs,) + pw.shape[1:], grp4),
            pl.BlockSpec((2, sw), lambda j: (0, j)),
            pl.BlockSpec((1, LANES), lambda j: (0, j)),
        ],
        out_specs=pl.BlockSpec((rows, LANES), lambda j: (0, j)),
        out_shape=jax.ShapeDtypeStruct((rows, width), F32),
        scratch_shapes=[
            pltpu.VMEM((rc, bw), BF16),
            pltpu.VMEM((rc, bw), F32),
            pltpu.VMEM((gs, CHUNK_W, CHUNK_W), BF16),
            pltpu.VMEM((gs, CHUNK_W, 2 * SSM_STATE), BF16),
            pltpu.VMEM((gs, CHUNK_W, 2 * SSM_STATE), BF16),
            pltpu.VMEM((gs, CHUNK_W, 2 * SSM_STATE), BF16),
            pltpu.VMEM((rc, sw), F32),
            pltpu.VMEM((rc, sw), F32),
            pltpu.VMEM((rc, sw), F32),
        ],
        compiler_params=pltpu.CompilerParams(
            dimension_semantics=("parallel",), vmem_limit_bytes=VMEM_LIMIT),
        name="ssm",
    )(u_tok, bb, cc, pw, a16, d)


def _ssm_tables(lam_re, lam_im, log_step, b_re, b_im, c_re, c_im):
    depth = lam_re.shape[0]
    g, p, h, t = SSM_GROUPS, SSM_STATE, SSM_GROUP_CH, CHUNK
    lam = lax.complex(lam_re.astype(F32), lam_im.astype(F32))
    delta = jnp.exp(log_step.astype(F32))[..., None]
    ld = lam * delta
    lam_bar = jnp.exp(ld)
    b_bar = ((lam_bar - 1.0) / lam)[..., None] * lax.complex(b_re.astype(F32), b_im.astype(F32))
    c_c = lax.complex(c_re.astype(F32), c_im.astype(F32))
    tau = jnp.arange(t + 1, dtype=F32)
    powers = jnp.exp(ld[:, :, None, :] * tau[None, None, :, None])

    cat = lambda x, y: jnp.concatenate([x, y], axis=-1)
    br, bi = jnp.real(b_bar).swapaxes(-1, -2), jnp.imag(b_bar).swapaxes(-1, -2)
    bb = jnp.stack([cat(br, bi), cat(-bi, br), cat(bi, br), cat(br, -bi)], axis=2)
    cr, ci = jnp.real(c_c), jnp.imag(c_c)
    cc = jnp.stack([cat(cr, -ci), cat(-ci, -cr)], axis=2)
    pr, pi = jnp.real(powers), jnp.imag(powers)
    pw = jnp.stack([cat(pr, pr), cat(pi, pi)], axis=2)
    pw = jnp.pad(pw, ((0, 0), (0, 0), (0, 0), (0, 24 - (t + 1)), (0, 0)))
    ar, ai = pr[:, :, t], pi[:, :, t]
    a16 = jnp.stack([cat(ar, ar).reshape(depth, g * 2 * p),
                     cat(-ai, ai).reshape(depth, g * 2 * p)], axis=1)
    return bb, cc, pw, a16


def _outproj_kernel(a_ref, g_ref, x_ref, wglu_f32, bglu_ref, sg_ref, wout_f32, ng_ref,
                    o_ref, xn_ref, wglu_ref, wout_ref):
    @pl.when(pl.program_id(0) == 0)
    def _():
        wglu_ref[...] = wglu_f32[...].astype(BF16)
        wout_ref[...] = wout_f32[...].astype(BF16)

    g = g_ref[...]
    z = jnp.dot(g.astype(BF16), wglu_ref[...], preferred_element_type=F32) + bglu_ref[...]
    s = g * jax.nn.sigmoid(z)
    s_n = _rms(s, sg_ref[...]).astype(BF16)
    y = jnp.dot(a_ref[...], wout_ref[:ATTN_WIDTH, :], preferred_element_type=F32)
    y = y + jnp.dot(s_n, wout_ref[ATTN_WIDTH:, :], preferred_element_type=F32)
    h = x_ref[...] + y
    o_ref[...] = h
    xn_ref[...] = _rms(h, ng_ref[...]).astype(xn_ref.dtype)


def _outproj(attn_n, g_tok, x, w_glu_all, b_glu, ssm_gain, w_out_all, mlp_gain, layer):
    rows = x.shape[0]
    tm = _row_tile(rows, 384)
    row = lambda i: (i, 0)
    fixed = lambda i: (0, 0)
    return pl.pallas_call(
        _outproj_kernel,
        grid=(rows // tm,),
        in_specs=[
            pl.BlockSpec((tm, ATTN_WIDTH), row),
            pl.BlockSpec((tm, SSM_WIDTH), row),
            pl.BlockSpec((tm, D_MODEL), row),
            pl.BlockSpec((None, SSM_WIDTH, SSM_WIDTH), lambda i: (layer, 0, 0)),
            pl.BlockSpec((1, SSM_WIDTH), fixed),
            pl.BlockSpec((1, SSM_WIDTH), fixed),
            pl.BlockSpec((None, D_MODEL, D_MODEL), lambda i: (layer, 0, 0)),
            pl.BlockSpec((1, D_MODEL), fixed),
        ],
        out_specs=[pl.BlockSpec((tm, D_MODEL), row), pl.BlockSpec((tm, D_MODEL), row)],
        out_shape=[jax.ShapeDtypeStruct((rows, D_MODEL), F32),
                   jax.ShapeDtypeStruct((rows, D_MODEL), BF16)],
        scratch_shapes=[pltpu.VMEM((SSM_WIDTH, SSM_WIDTH), BF16), pltpu.VMEM((D_MODEL, D_MODEL), BF16)],
        compiler_params=pltpu.CompilerParams(
            dimension_semantics=("arbitrary",), vmem_limit_bytes=BIG_VMEM_LIMIT),
        name="outproj",
    )(attn_n, g_tok, x, w_glu_all, b_glu, ssm_gain, w_out_all, mlp_gain)


def _mlp_tile(xn_ref, x_ref, w_up, w_down, o_ref):
    a = jnp.dot(xn_ref[...], w_up, preferred_element_type=F32)
    a = jnp.square(jnp.maximum(a, 0.0)).astype(BF16)
    o_ref[...] += jnp.dot(a, w_down, preferred_element_type=F32)


def _mlp_first_kernel(xn_ref, x_ref, wu_ref, wd_ref, o_ref, wu_bf_ref, wd_bf_ref):
    @pl.when(pl.program_id(0) == 0)
    def _():
        o_ref[...] = x_ref[...]

    wu_bf_ref[...] = wu_ref[...].astype(BF16)
    wd_bf_ref[...] = wd_ref[...].astype(BF16)
    _mlp_tile(xn_ref, x_ref, wu_bf_ref[...], wd_bf_ref[...], o_ref)


def _mlp_rest_kernel(xn_ref, x_ref, wu_ref, wd_ref, partial_ref, o_ref):
    del partial_ref

    @pl.when(pl.program_id(1) == 0)
    def _():
        o_ref[...] = x_ref[...]

    _mlp_tile(xn_ref, x_ref, wu_ref[...], wd_ref[...], o_ref)


def _mlp(xn, x, w_up_all, w_down_all, layer):
    rows = x.shape[0]
    tm = _row_tile(rows, MLP_ROWS)
    tf_first, tf = MLP_FF_TILE_FIRST, MLP_FF_TILE
    first_rows = lambda f: (0, 0)
    partial, w_up_bf, w_down_bf = pl.pallas_call(
        _mlp_first_kernel,
        grid=(D_FF // tf_first,),
        in_specs=[
            pl.BlockSpec((tm, D_MODEL), first_rows),
            pl.BlockSpec((tm, D_MODEL), first_rows),
            pl.BlockSpec((None, D_MODEL, tf_first), lambda f: (layer, 0, f)),
            pl.BlockSpec((None, tf_first, D_MODEL), lambda f: (layer, f, 0)),
        ],
        out_specs=[pl.BlockSpec((tm, D_MODEL), first_rows),
                   pl.BlockSpec((D_MODEL, tf_first), lambda f: (0, f)),
                   pl.BlockSpec((tf_first, D_MODEL), lambda f: (f, 0))],
        out_shape=[jax.ShapeDtypeStruct((rows, D_MODEL), F32),
                   jax.ShapeDtypeStruct((D_MODEL, D_FF), BF16),
                   jax.ShapeDtypeStruct((D_FF, D_MODEL), BF16)],
        compiler_params=pltpu.CompilerParams(
            dimension_semantics=("arbitrary",), vmem_limit_bytes=BIG_VMEM_LIMIT),
        name="mlp_first",
    )(xn, x, w_up_all, w_down_all)
    if rows == tm:
        return partial
    return pl.pallas_call(
        _mlp_rest_kernel,
        grid=(rows // tm - 1, D_FF // tf),
        in_specs=[
            pl.BlockSpec((tm, D_MODEL), lambda i, f: (i + 1, 0)),
            pl.BlockSpec((tm, D_MODEL), lambda i, f: (i + 1, 0)),
            pl.BlockSpec((D_MODEL, tf), lambda i, f: (0, f)),
            pl.BlockSpec((tf, D_MODEL), lambda i, f: (f, 0)),
            pl.BlockSpec(memory_space=pl.ANY),
        ],
        out_specs=pl.BlockSpec((tm, D_MODEL), lambda i, f: (i + 1, 0)),
        out_shape=jax.ShapeDtypeStruct((rows, D_MODEL), F32),
        input_output_aliases={4: 0},
        compiler_params=pltpu.CompilerParams(
            dimension_semantics=("parallel", "arbitrary"), vmem_limit_bytes=BIG_VMEM_LIMIT),
        name="mlp_rest",
    )(xn, x, w_up_bf, w_down_bf, partial)


def kernel(x, meta_tokens, norm_mix_g, w_in, q_norm_g, k_norm_g, attn_sinks, ssm_lambda_re, ssm_lambda_im, ssm_log_step, ssm_b_re, ssm_b_im, ssm_c_re, ssm_c_im, ssm_d, w_glu, b_glu, attn_out_g, ssm_out_g, w_out, norm_mlp_g, w_up, w_down):
    b, seq, d = x.shape
    depth = w_in.shape[0]
    assert d == D_MODEL
    length = N_META + seq
    lp = -(-length // BLOCK) * BLOCK
    rows = b * lp

    meta = jnp.broadcast_to(meta_tokens.astype(x.dtype)[None], (b, N_META, d))
    pad = jnp.zeros((b, lp - length, d), x.dtype)
    h_res = jnp.concatenate([meta, x, pad], axis=1).reshape(rows, d)

    bb, cc, pw, a16 = _ssm_tables(
        ssm_lambda_re, ssm_lambda_im, ssm_log_step, ssm_b_re, ssm_b_im, ssm_c_re, ssm_c_im)

    head_id = jnp.arange(KV_WIDTH) // HEAD_DIM
    head_ones = (head_id[:, None] == head_id[None, :]).astype(BF16)
    slopes = jnp.exp2(-8.0 * jnp.arange(1, N_HEADS + 1, dtype=F32) / N_HEADS)
    attn_bias, attn_off = _attn_bias(slopes)

    row_vec = lambda v: v.astype(F32).reshape(1, -1)
    for l in range(depth):
        q_gain = jnp.tile(row_vec(q_norm_g[l]), (1, N_KV_HEADS)) * (1.0 / math.sqrt(HEAD_DIM))
        k_gain = jnp.tile(row_vec(k_norm_g[l]), (1, N_KV_HEADS))
        qkv, u_tok = _inproj(h_res, row_vec(norm_mix_g[l]), w_in, l, head_ones, q_gain, k_gain)
        sink_row = jnp.broadcast_to(
            attn_sinks[l].astype(F32).reshape(N_KV_HEADS, 1, KV_GROUP, 1),
            (N_KV_HEADS, 1, KV_GROUP, SUB)).reshape(N_KV_HEADS, 1, KV_GROUP * SUB)
        attn_n = _attention(qkv.reshape(b, lp, QKV_WIDTH), attn_bias, attn_off, sink_row,
                            row_vec(attn_out_g[l]))
        g_tok = _ssm(u_tok, bb[l], cc[l], pw[l], a16[l], row_vec(ssm_d[l]), b)
        h_res, xn = _outproj(attn_n.reshape(rows, ATTN_WIDTH), g_tok, h_res, w_glu,
                             row_vec(b_glu[l]), row_vec(ssm_out_g[l]), w_out,
                             row_vec(norm_mlp_g[l]), l)
        h_res = _mlp(xn, h_res, w_up, w_down, l)
    return h_res.reshape(b, lp, d)[:, N_META:length]
```

```python
import functools
import math

import jax
import jax.numpy as jnp
from jax import lax
from jax.experimental import pallas as pl
from jax.experimental.pallas import tpu as pltpu

D_MODEL = 2048
N_META = 16
HEAD_DIM = 64
ATTN_WIDTH = D_MODEL // 2
N_HEADS = ATTN_WIDTH // HEAD_DIM
N_KV_HEADS = N_HEADS // 4
KV_GROUP = N_HEADS // N_KV_HEADS
KV_WIDTH = N_KV_HEADS * HEAD_DIM
SSM_WIDTH = D_MODEL - ATTN_WIDTH
SSM_GROUP_CH = 16
SSM_GROUPS = SSM_WIDTH // SSM_GROUP_CH
SSM_STATE = 64
WINDOW = 128
BLOCK = 128
D_FF = 4 * D_MODEL
IN_WIDTH = ATTN_WIDTH + 2 * KV_WIDTH + SSM_WIDTH
NORM_EPS = 1e-6
NEG_INF = -1e30

CHUNK = 16
CHUNK_W = CHUNK * SSM_GROUP_CH
GROUPS_PER_STEP = 8
RELAYOUT_UNROLL = 7

QKV_WIDTH = ATTN_WIDTH + 2 * KV_WIDTH
COL_K = ATTN_WIDTH
COL_V = ATTN_WIDTH + KV_WIDTH
LANES = 128
BLOCKS_PER_VREG = LANES // SSM_GROUP_CH

VMEM_LIMIT = 48 * 1024 * 1024
MLP_ROWS = 768
MLP_FF_TILE = 1024
MLP_FF_TILE_FIRST = 512
BIG_VMEM_LIMIT = 56 * 1024 * 1024

F32 = jnp.float32
BF16 = jnp.bfloat16


def _row_tile(rows, target):
    k = rows // BLOCK
    best = 1
    for d in range(1, k + 1):
        if k % d == 0 and d * BLOCK <= target:
            best = d
    return best * BLOCK


def _rms(x, gain):
    ms = jnp.mean(x * x, axis=-1, keepdims=True)
    return x * lax.rsqrt(ms + NORM_EPS) * gain


def _head_norm(x, head_ones, gain):
    x2 = x * x
    hi = x2.astype(BF16)
    lo = (x2 - hi.astype(F32)).astype(BF16)
    ssq = (jnp.dot(hi, head_ones, preferred_element_type=F32)
           + jnp.dot(lo, head_ones, preferred_element_type=F32))
    return x * lax.rsqrt(ssq * (1.0 / HEAD_DIM) + NORM_EPS) * gain


def _inproj_kernel(x_ref, g_ref, w_ref, ones_ref, qg_ref, kg_ref, qkv_ref, u_ref, w_bf):
    @pl.when(pl.program_id(0) == 0)
    def _():
        w_bf[...] = w_ref[...].astype(BF16)

    h = _rms(x_ref[...], g_ref[...]).astype(BF16)
    proj = jnp.dot(h, w_bf[...], preferred_element_type=F32)
    head_ones = ones_ref[...]
    for blk in range(ATTN_WIDTH // KV_WIDTH):
        cols = slice(blk * KV_WIDTH, (blk + 1) * KV_WIDTH)
        qkv_ref[:, cols] = _head_norm(proj[:, cols], head_ones, qg_ref[...]).astype(qkv_ref.dtype)
    qkv_ref[:, COL_K:COL_V] = _head_norm(
        proj[:, COL_K:COL_V], head_ones, kg_ref[...]).astype(qkv_ref.dtype)
    qkv_ref[:, COL_V:QKV_WIDTH] = proj[:, COL_V:QKV_WIDTH].astype(qkv_ref.dtype)
    u_ref[...] = proj[:, QKV_WIDTH:]


def _inproj(x, gain, w_all, layer, head_ones, q_gain, k_gain):
    rows = x.shape[0]
    tm = _row_tile(rows, 384)
    fixed = lambda i: (0, 0)
    return pl.pallas_call(
        _inproj_kernel,
        grid=(rows // tm,),
        in_specs=[
            pl.BlockSpec((tm, D_MODEL), lambda i: (i, 0)),
            pl.BlockSpec((1, D_MODEL), fixed),
            pl.BlockSpec((None, D_MODEL, IN_WIDTH), lambda i: (layer, 0, 0)),
            pl.BlockSpec((KV_WIDTH, KV_WIDTH), fixed),
            pl.BlockSpec((1, KV_WIDTH), fixed),
            pl.BlockSpec((1, KV_WIDTH), fixed),
        ],
        out_specs=[pl.BlockSpec((tm, QKV_WIDTH), lambda i: (i, 0)),
                   pl.BlockSpec((tm, SSM_WIDTH), lambda i: (i, 0))],
        out_shape=[jax.ShapeDtypeStruct((rows, QKV_WIDTH), BF16),
                   jax.ShapeDtypeStruct((rows, SSM_WIDTH), F32)],
        scratch_shapes=[pltpu.VMEM((D_MODEL, IN_WIDTH), BF16)],
        compiler_params=pltpu.CompilerParams(
            dimension_semantics=("arbitrary",), vmem_limit_bytes=BIG_VMEM_LIMIT),
        name="inproj",
    )(x, gain, w_all, head_ones, q_gain, k_gain)


SUB = 64
BAND = WINDOW + SUB
KEYS = BAND + N_META


def _attn_bias(slopes):
    j = jnp.arange(KEYS)[:, None]
    i = jnp.arange(SUB)[None, :]
    is_band = j < BAND
    dist = WINDOW + i - j
    sl = slopes.reshape(N_KV_HEADS, 1, KV_GROUP, 1)
    dist_f = dist.astype(F32)[None, :, None, :]
    band = jnp.where(((dist >= 0) & (dist < WINDOW))[None, :, None, :], -sl * dist_f, NEG_INF)
    meta = -sl * (i - (j - BAND)).astype(F32)[None, :, None, :]
    bias = jnp.where(is_band[None, :, None, :], band, meta)
    lanes = KV_GROUP * SUB
    off_coef = jnp.broadcast_to(-sl, (N_KV_HEADS, N_META, KV_GROUP, SUB))
    return bias.reshape(N_KV_HEADS, KEYS, lanes), off_coef.reshape(N_KV_HEADS, N_META, lanes)


def _attn_kernel(q_ref, kp_ref, kc_ref, vp_ref, vc_ref, km_ref, vm_ref,
                 bias_ref, offc_ref, sink_ref, og_ref, o_ref, acc):
    n = pl.program_id(1)
    lanes = KV_GROUP * SUB
    ones = jnp.ones((KEYS, 8), BF16)

    def body(early):
        chains = [(sb, kh) for sb in range(BLOCK // SUB) for kh in range(N_KV_HEADS)]
        scores, values = [], []
        for sb, kh in chains:
            start = n * BLOCK + sb * SUB
            lo = sb * SUB
            ks = slice(kh * HEAD_DIM, (kh + 1) * HEAD_DIM)
            k_cat = jnp.concatenate([kp_ref[:, ks], kc_ref[:, ks]], axis=0)
            v_cat = jnp.concatenate([vp_ref[:, ks], vc_ref[:, ks]], axis=0)
            k_win = jnp.concatenate([k_cat[lo:lo + BAND], km_ref[:, ks]], axis=0)
            v_win = jnp.concatenate([v_cat[lo:lo + BAND], vm_ref[:, ks]], axis=0)
            values.append(jnp.concatenate([v_win, ones], axis=1))
            q_s = jnp.concatenate(
                [q_ref[lo:lo + SUB, (kh * KV_GROUP + g) * HEAD_DIM:(kh * KV_GROUP + g + 1) * HEAD_DIM]
                 for g in range(KV_GROUP)], axis=0)
            s = lax.dot_general(k_win, q_s, (((1,), (1,)), ((), ())),
                                preferred_element_type=F32)
            s = s + bias_ref[kh]
            s = jnp.concatenate([s[:BAND], s[BAND:] + offc_ref[kh] * start.astype(F32)], axis=0)
            if early:
                j = lax.broadcasted_iota(jnp.int32, (KEYS, lanes), 0)
                i = lax.broadcasted_iota(jnp.int32, (KEYS, lanes), 1) % SUB
                ok = jnp.where(j < BAND, start - WINDOW + j - N_META, start + i - (j - BAND)) >= 0
                s = jnp.where(ok, s, NEG_INF)
            scores.append(s)
        maxes = [jnp.maximum(jnp.max(s, axis=0, keepdims=True), sink_ref[kh])
                 for s, (sb, kh) in zip(scores, chains)]
        probs = [jnp.exp(s - m).astype(BF16) for s, m in zip(scores, maxes)]
        for (sb, kh), p, m, v_ext in zip(chains, probs, maxes, values):
            o_ext = lax.dot_general(v_ext, p, (((0,), (0,)), ((), ())),
                                    preferred_element_type=F32)
            denom = o_ext[HEAD_DIM:HEAD_DIM + 1] + jnp.exp(sink_ref[kh] - m)
            o = o_ext[:HEAD_DIM] / denom
            for g in range(KV_GROUP):
                h = kh * KV_GROUP + g
                acc[h * HEAD_DIM:(h + 1) * HEAD_DIM, sb * SUB:(sb + 1) * SUB] = o[:, g * SUB:(g + 1) * SUB]
        a_t = acc[...]
        ms = jnp.mean(a_t * a_t, axis=0, keepdims=True)
        a_n = a_t * lax.rsqrt(ms + NORM_EPS)
        o_ref[...] = (a_n.T * og_ref[...]).astype(o_ref.dtype)

    @pl.when(n >= 2)
    def _():
        body(False)

    @pl.when(n < 2)
    def _():
        body(True)


def _attention(qkv3, bias, off_coef, sink_row, out_gain):
    b, lp, _ = qkv3.shape
    nb = lp // BLOCK
    kcol, vcol = COL_K // KV_WIDTH, COL_V // KV_WIDTH
    prev = lambda n: jnp.maximum(n - 1, 0)
    fixed3 = lambda bi, n: (0, 0, 0)
    return pl.pallas_call(
        _attn_kernel,
        grid=(b, nb),
        in_specs=[
            pl.BlockSpec((None, BLOCK, ATTN_WIDTH), lambda bi, n: (bi, n, 0)),
            pl.BlockSpec((None, BLOCK, KV_WIDTH), lambda bi, n: (bi, prev(n), kcol)),
            pl.BlockSpec((None, BLOCK, KV_WIDTH), lambda bi, n: (bi, n, kcol)),
            pl.BlockSpec((None, BLOCK, KV_WIDTH), lambda bi, n: (bi, prev(n), vcol)),
            pl.BlockSpec((None, BLOCK, KV_WIDTH), lambda bi, n: (bi, n, vcol)),
            pl.BlockSpec((None, N_META, KV_WIDTH), lambda bi, n: (bi, 0, kcol)),
            pl.BlockSpec((None, N_META, KV_WIDTH), lambda bi, n: (bi, 0, vcol)),
            pl.BlockSpec(bias.shape, fixed3),
            pl.BlockSpec(off_coef.shape, fixed3),
            pl.BlockSpec(sink_row.shape, fixed3),
            pl.BlockSpec((1, ATTN_WIDTH), lambda bi, n: (0, 0)),
        ],
        out_specs=pl.BlockSpec((None, BLOCK, ATTN_WIDTH), lambda bi, n: (bi, n, 0)),
        out_shape=jax.ShapeDtypeStruct((b, lp, ATTN_WIDTH), BF16),
        compiler_params=pltpu.CompilerParams(
            dimension_semantics=("parallel", "parallel"), vmem_limit_bytes=VMEM_LIMIT),
        scratch_shapes=[pltpu.VMEM((ATTN_WIDTH, BLOCK), F32)],
        name="attention",
    )(qkv3, qkv3, qkv3, qkv3, qkv3, qkv3, qkv3, bias, off_coef, sink_row, out_gain)


def _gelu_tanh(x):
    c = math.sqrt(2.0 / math.pi)
    return 0.5 * x * (1.0 + jnp.tanh(c * (x + 0.044715 * (x * x * x))))


def _block_transpose_stage(arrs, dist, lane_blk):
    nblk = BLOCKS_PER_VREG
    keep = (lane_blk & dist) == 0
    nxt = list(arrs)
    for i in range(nblk):
        if i & dist == 0:
            lo, hi = arrs[i], arrs[i + dist]
            nxt[i] = jnp.where(keep, lo, pltpu.roll(hi, dist * SSM_GROUP_CH, axis=1))
            nxt[i + dist] = jnp.where(keep, pltpu.roll(lo, (nblk - dist) * SSM_GROUP_CH, axis=1), hi)
    return nxt


def _pipelined_block_transpose(n_tiles, load, store, zero, lane_blk):
    def trip(r, carry):
        after1, after2 = carry
        fresh = load(jnp.minimum(r, n_tiles - 1))
        new1 = [_block_transpose_stage(g, 4, lane_blk) for g in fresh]
        new2 = [_block_transpose_stage(g, 2, lane_blk) for g in after1]
        done = [_block_transpose_stage(g, 1, lane_blk) for g in after2]
        store(jnp.maximum(r - 2, 0), done)
        return new1, new2

    groups = CHUNK // BLOCKS_PER_VREG
    init = [[zero] * BLOCKS_PER_VREG for _ in range(groups)]
    lax.fori_loop(0, n_tiles + 2, trip, (init, init), unroll=RELAYOUT_UNROLL)


def _ssm_kernel(u_ref, bb_ref, cc_ref, pw_ref, a_ref, d_ref, o_ref,
                lhs, yscr, m_scr, win_scr, wsw_scr, wct_scr, zin, zsw, st,
                *, n_chunks, n_batch):
    nblk = BLOCKS_PER_VREG
    sw = 2 * SSM_STATE
    rc = n_batch * n_chunks
    half = CHUNK // nblk

    def tok_rows(tile, tile_chunks, t):
        return pl.ds(tile * (tile_chunks * CHUNK) + t, tile_chunks, stride=CHUNK)

    lane256 = lax.broadcasted_iota(jnp.int32, (CHUNK, CHUNK_W), 1)
    for g in range(GROUPS_PER_STEP):
        bb, bbs, bsw, bsws = bb_ref[g, 0], bb_ref[g, 1], bb_ref[g, 2], bb_ref[g, 3]
        cc, ccs = cc_ref[g, 0], cc_ref[g, 1]
        c_pow = [pw_ref[g, 0, tau:tau + 1, :] * cc + pw_ref[g, 1, tau:tau + 1, :] * ccs
                 for tau in range(CHUNK + 1)]
        for t in range(CHUNK):
            rows = slice(t * CHUNK, (t + 1) * CHUNK)
            p_re = pw_ref[g, 0, CHUNK - 1 - t:CHUNK - t, :]
            p_im = pw_ref[g, 1, CHUNK - 1 - t:CHUNK - t, :]
            win_scr[g, rows, :] = (p_re * bb + p_im * bbs).astype(BF16)
            wsw_scr[g, rows, :] = (p_re * bsw + p_im * bsws).astype(BF16)
            wct_scr[g, rows, :] = c_pow[t + 1].astype(BF16)
        base = lax.dot_general(bb, jnp.concatenate(c_pow[:CHUNK], axis=0), (((1,), (1,)), ((), ())),
                               precision=lax.Precision.HIGHEST, preferred_element_type=F32)
        for sblk in range(CHUNK):
            row = base if sblk == 0 else jnp.where(
                lane256 >= sblk * SSM_GROUP_CH, pltpu.roll(base, sblk * SSM_GROUP_CH, axis=1), 0.0)
            m_scr[g, sblk * CHUNK:(sblk + 1) * CHUNK, :] = row.astype(BF16)

    rt_in = 16
    lane_blk_in = lax.broadcasted_iota(jnp.int32, (rt_in // 2, LANES), 1) // SSM_GROUP_CH

    def load_u(r):
        return [[pltpu.bitcast(u_ref[tok_rows(r, rt_in, hf * nblk + tb), :].astype(BF16), jnp.uint32)
                 for tb in range(nblk)] for hf in range(half)]

    def store_lhs(r, groups):
        rows = pl.ds(pl.multiple_of(r * rt_in, rt_in), rt_in)
        for hf in range(half):
            for g in range(nblk):
                col = g * CHUNK_W + hf * LANES
                lhs[rows, col:col + LANES] = pltpu.bitcast(groups[hf][g], BF16)

    _pipelined_block_transpose(rc // rt_in, load_u, store_lhs,
                               jnp.zeros((rt_in // 2, LANES), jnp.uint32), lane_blk_in)

    for g in range(GROUPS_PER_STEP):
        u_g = lhs[:, g * CHUNK_W:(g + 1) * CHUNK_W]
        zin[:, g * sw:(g + 1) * sw] = jnp.dot(u_g, win_scr[g], preferred_element_type=F32)
        zsw[:, g * sw:(g + 1) * sw] = jnp.dot(u_g, wsw_scr[g], preferred_element_type=F32)

    a1 = a_ref[0:1, :]
    a2 = a_ref[1:2, :]

    def step(c, carry):
        nxt = []
        for b in range(n_batch):
            s, x = carry[b]
            row = b * n_chunks + c
            st[pl.ds(row, 1), :] = s
            nxt.append((a1 * s + a2 * x + zin[pl.ds(row, 1), :],
                        a1 * x - a2 * s + zsw[pl.ds(row, 1), :]))
        return tuple(nxt)

    zero = jnp.zeros((1, GROUPS_PER_STEP * sw), F32)
    lax.fori_loop(0, n_chunks, step, tuple((zero, zero) for _ in range(n_batch)))

    for g in range(GROUPS_PER_STEP):
        cols = slice(g * CHUNK_W, (g + 1) * CHUNK_W)
        y = jnp.dot(lhs[:, cols], m_scr[g], preferred_element_type=F32)
        y = y + lax.dot_general(st[:, g * sw:(g + 1) * sw].astype(BF16), wct_scr[g],
                                (((1,), (1,)), ((), ())), preferred_element_type=F32)
        yscr[:, cols] = y

    rt_out = 8
    d_row = d_ref[...]
    lane_blk_out = lax.broadcasted_iota(jnp.int32, (rt_out, LANES), 1) // SSM_GROUP_CH

    def load_y(r):
        rows = pl.ds(pl.multiple_of(r * rt_out, rt_out), rt_out)
        return [[yscr[rows, g * CHUNK_W + hf * LANES:g * CHUNK_W + (hf + 1) * LANES]
                 for g in range(nblk)] for hf in range(half)]

    def store_tokens(r, groups):
        for hf in range(half):
            for tb in range(nblk):
                rows = tok_rows(r, rt_out, hf * nblk + tb)
                o_ref[rows, :] = _gelu_tanh(groups[hf][tb] + d_row * u_ref[rows, :])

    _pipelined_block_transpose(rc // rt_out, load_y, store_tokens,
                               jnp.zeros((rt_out, LANES), F32), lane_blk_out)


def _ssm(u_tok, bb, cc, pw, a16, d, n_batch):
    rows, width = u_tok.shape
    rc = rows // CHUNK
    n_chunks = rc // n_batch
    assert GROUPS_PER_STEP == BLOCKS_PER_VREG and rc % 16 == 0
    gs = GROUPS_PER_STEP
    bw = gs * CHUNK_W
    sw = gs * 2 * SSM_STATE
    kern = functools.partial(_ssm_kernel, n_chunks=n_chunks, n_batch=n_batch)
    grp4 = lambda j: (j, 0, 0, 0)
    return pl.pallas_call(
        kern,
        grid=(SSM_GROUPS // gs,),
        in_specs=[
            pl.BlockSpec((rows, LANES), lambda j: (0, j)),
            pl.BlockSpec((gs,) + bb.shape[1:], grp4),
            pl.BlockSpec((gs,) + cc.shape[1:], grp4),
            pl.BlockSpec((gs,) + pw.shape[1:], grp4),
            pl.BlockSpec((2, sw), lambda j: (0, j)),
            pl.BlockSpec((1, LANES), lambda j: (0, j)),
        ],
        out_specs=pl.BlockSpec((rows, LANES), lambda j: (0, j)),
        out_shape=jax.ShapeDtypeStruct((rows, width), F32),
        scratch_shapes=[
            pltpu.VMEM((rc, bw), BF16),
            pltpu.VMEM((rc, bw), F32),
            pltpu.VMEM((gs, CHUNK_W, CHUNK_W), BF16),
            pltpu.VMEM((gs, CHUNK_W, 2 * SSM_STATE), BF16),
            pltpu.VMEM((gs, CHUNK_W, 2 * SSM_STATE), BF16),
            pltpu.VMEM((gs, CHUNK_W, 2 * SSM_STATE), BF16),
            pltpu.VMEM((rc, sw), F32),
            pltpu.VMEM((rc, sw), F32),
            pltpu.VMEM((rc, sw), F32),
        ],
        compiler_params=pltpu.CompilerParams(
            dimension_semantics=("parallel",), vmem_limit_bytes=VMEM_LIMIT),
        name="ssm",
    )(u_tok, bb, cc, pw, a16, d)


def _ssm_tables(lam_re, lam_im, log_step, b_re, b_im, c_re, c_im):
    depth = lam_re.shape[0]
    g, p, h, t = SSM_GROUPS, SSM_STATE, SSM_GROUP_CH, CHUNK
    lam = lax.complex(lam_re.astype(F32), lam_im.astype(F32))
    delta = jnp.exp(log_step.astype(F32))[..., None]
    ld = lam * delta
    lam_bar = jnp.exp(ld)
    b_bar = ((lam_bar - 1.0) / lam)[..., None] * lax.complex(b_re.astype(F32), b_im.astype(F32))
    c_c = lax.complex(c_re.astype(F32), c_im.astype(F32))
    tau = jnp.arange(t + 1, dtype=F32)
    powers = jnp.exp(ld[:, :, None, :] * tau[None, None, :, None])

    cat = lambda x, y: jnp.concatenate([x, y], axis=-1)
    br, bi = jnp.real(b_bar).swapaxes(-1, -2), jnp.imag(b_bar).swapaxes(-1, -2)
    bb = jnp.stack([cat(br, bi), cat(-bi, br), cat(bi, br), cat(br, -bi)], axis=2)
    cr, ci = jnp.real(c_c), jnp.imag(c_c)
    cc = jnp.stack([cat(cr, -ci), cat(-ci, -cr)], axis=2)
    pr, pi = jnp.real(powers), jnp.imag(powers)
    pw = jnp.stack([cat(pr, pr), cat(pi, pi)], axis=2)
    pw = jnp.pad(pw, ((0, 0), (0, 0), (0, 0), (0, 24 - (t + 1)), (0, 0)))
    ar, ai = pr[:, :, t], pi[:, :, t]
    a16 = jnp.stack([cat(ar, ar).reshape(depth, g * 2 * p),
                     cat(-ai, ai).reshape(depth, g * 2 * p)], axis=1)
    return bb, cc, pw, a16


def _outproj_kernel(a_ref, g_ref, x_ref, wglu_f32, bglu_ref, sg_ref, wout_f32, ng_ref,
                    o_ref, xn_ref, wglu_ref, wout_ref):
    @pl.when(pl.program_id(0) == 0)
    def _():
        wglu_ref[...] = wglu_f32[...].astype(BF16)
        wout_ref[...] = wout_f32[...].astype(BF16)

    g = g_ref[...]
    z = jnp.dot(g.astype(BF16), wglu_ref[...], preferred_element_type=F32) + bglu_ref[...]
    s = g * jax.nn.sigmoid(z)
    s_n = _rms(s, sg_ref[...]).astype(BF16)
    y = jnp.dot(a_ref[...], wout_ref[:ATTN_WIDTH, :], preferred_element_type=F32)
    y = y + jnp.dot(s_n, wout_ref[ATTN_WIDTH:, :], preferred_element_type=F32)
    h = x_ref[...] + y
    o_ref[...] = h
    xn_ref[...] = _rms(h, ng_ref[...]).astype(xn_ref.dtype)


def _outproj(attn_n, g_tok, x, w_glu_all, b_glu, ssm_gain, w_out_all, mlp_gain, layer):
    rows = x.shape[0]
    tm = _row_tile(rows, 384)
    row = lambda i: (i, 0)
    fixed = lambda i: (0, 0)
    return pl.pallas_call(
        _outproj_kernel,
        grid=(rows // tm,),
        in_specs=[
            pl.BlockSpec((tm, ATTN_WIDTH), row),
            pl.BlockSpec((tm, SSM_WIDTH), row),
            pl.BlockSpec((tm, D_MODEL), row),
            pl.BlockSpec((None, SSM_WIDTH, SSM_WIDTH), lambda i: (layer, 0, 0)),
            pl.BlockSpec((1, SSM_WIDTH), fixed),
            pl.BlockSpec((1, SSM_WIDTH), fixed),
            pl.BlockSpec((None, D_MODEL, D_MODEL), lambda i: (layer, 0, 0)),
            pl.BlockSpec((1, D_MODEL), fixed),
        ],
        out_specs=[pl.BlockSpec((tm, D_MODEL), row), pl.BlockSpec((tm, D_MODEL), row)],
        out_shape=[jax.ShapeDtypeStruct((rows, D_MODEL), F32),
                   jax.ShapeDtypeStruct((rows, D_MODEL), BF16)],
        scratch_shapes=[pltpu.VMEM((SSM_WIDTH, SSM_WIDTH), BF16), pltpu.VMEM((D_MODEL, D_MODEL), BF16)],
        compiler_params=pltpu.CompilerParams(
            dimension_semantics=("arbitrary",), vmem_limit_bytes=BIG_VMEM_LIMIT),
        name="outproj",
    )(attn_n, g_tok, x, w_glu_all, b_glu, ssm_gain, w_out_all, mlp_gain)


def _mlp_tile(xn_ref, x_ref, w_up, w_down, o_ref):
    a = jnp.dot(xn_ref[...], w_up, preferred_element_type=F32)
    a = jnp.square(jnp.maximum(a, 0.0)).astype(BF16)
    o_ref[...] += jnp.dot(a, w_down, preferred_element_type=F32)


def _mlp_first_kernel(xn_ref, x_ref, wu_ref, wd_ref, o_ref, wu_bf_ref, wd_bf_ref):
    @pl.when(pl.program_id(0) == 0)
    def _():
        o_ref[...] = x_ref[...]

    wu_bf_ref[...] = wu_ref[...].astype(BF16)
    wd_bf_ref[...] = wd_ref[...].astype(BF16)
    _mlp_tile(xn_ref, x_ref, wu_bf_ref[...], wd_bf_ref[...], o_ref)


def _mlp_rest_kernel(xn_ref, x_ref, wu_ref, wd_ref, first_hbm, o_ref, sem):
    i = pl.program_id(0)
    f = pl.program_id(1)

    @pl.when((i == 0) & (f == 0))
    def _():
        copy = pltpu.make_async_copy(first_hbm, o_ref, sem)
        copy.start()
        copy.wait()

    @pl.when(i > 0)
    def _():
        @pl.when(f == 0)
        def _():
            o_ref[...] = x_ref[...]

        _mlp_tile(xn_ref, x_ref, wu_ref[...], wd_ref[...], o_ref)


def _mlp(xn, x, w_up_all, w_down_all, layer):
    rows = x.shape[0]
    tm = _row_tile(rows, MLP_ROWS)
    tf_first, tf = MLP_FF_TILE_FIRST, MLP_FF_TILE
    first_rows = lambda f: (0, 0)
    first, w_up_bf, w_down_bf = pl.pallas_call(
        _mlp_first_kernel,
        grid=(D_FF // tf_first,),
        in_specs=[
            pl.BlockSpec((tm, D_MODEL), first_rows),
            pl.BlockSpec((tm, D_MODEL), first_rows),
            pl.BlockSpec((None, D_MODEL, tf_first), lambda f: (layer, 0, f)),
            pl.BlockSpec((None, tf_first, D_MODEL), lambda f: (layer, f, 0)),
        ],
        out_specs=[pl.BlockSpec((tm, D_MODEL), first_rows),
                   pl.BlockSpec((D_MODEL, tf_first), lambda f: (0, f)),
                   pl.BlockSpec((tf_first, D_MODEL), lambda f: (f, 0))],
        out_shape=[jax.ShapeDtypeStruct((tm, D_MODEL), F32),
                   jax.ShapeDtypeStruct((D_MODEL, D_FF), BF16),
                   jax.ShapeDtypeStruct((D_FF, D_MODEL), BF16)],
        compiler_params=pltpu.CompilerParams(
            dimension_semantics=("arbitrary",), vmem_limit_bytes=BIG_VMEM_LIMIT),
        name="mlp_first",
    )(xn, x, w_up_all, w_down_all)
    if rows == tm:
        return first
    ff = lambda i, f: jnp.where(i == 0, 0, f)
    return pl.pallas_call(
        _mlp_rest_kernel,
        grid=(rows // tm, D_FF // tf),
        in_specs=[
            pl.BlockSpec((tm, D_MODEL), lambda i, f: (jnp.maximum(i, 1), 0)),
            pl.BlockSpec((tm, D_MODEL), lambda i, f: (jnp.maximum(i, 1), 0)),
            pl.BlockSpec((D_MODEL, tf), lambda i, f: (0, ff(i, f))),
            pl.BlockSpec((tf, D_MODEL), lambda i, f: (ff(i, f), 0)),
            pl.BlockSpec(memory_space=pl.ANY),
        ],
        out_specs=pl.BlockSpec((tm, D_MODEL), lambda i, f: (i, 0)),
        out_shape=jax.ShapeDtypeStruct((rows, D_MODEL), F32),
        scratch_shapes=[pltpu.SemaphoreType.DMA(())],
        compiler_params=pltpu.CompilerParams(
            dimension_semantics=("arbitrary", "arbitrary"), vmem_limit_bytes=BIG_VMEM_LIMIT),
        name="mlp_rest",
    )(xn, x, w_up_bf, w_down_bf, first)


def kernel(x, meta_tokens, norm_mix_g, w_in, q_norm_g, k_norm_g, attn_sinks, ssm_lambda_re, ssm_lambda_im, ssm_log_step, ssm_b_re, ssm_b_im, ssm_c_re, ssm_c_im, ssm_d, w_glu, b_glu, attn_out_g, ssm_out_g, w_out, norm_mlp_g, w_up, w_down):
    b, seq, d = x.shape
    depth = w_in.shape[0]
    assert d == D_MODEL
    length = N_META + seq
    lp = -(-length // BLOCK) * BLOCK
    rows = b * lp

    meta = jnp.broadcast_to(meta_tokens.astype(x.dtype)[None], (b, N_META, d))
    pad = jnp.zeros((b, lp - length, d), x.dtype)
    h_res = jnp.concatenate([meta, x, pad], axis=1).reshape(rows, d)

    bb, cc, pw, a16 = _ssm_tables(
        ssm_lambda_re, ssm_lambda_im, ssm_log_step, ssm_b_re, ssm_b_im, ssm_c_re, ssm_c_im)

    head_id = jnp.arange(KV_WIDTH) // HEAD_DIM
    head_ones = (head_id[:, None] == head_id[None, :]).astype(BF16)
    slopes = jnp.exp2(-8.0 * jnp.arange(1, N_HEADS + 1, dtype=F32) / N_HEADS)
    attn_bias, attn_off = _attn_bias(slopes)

    row_vec = lambda v: v.astype(F32).reshape(1, -1)
    for l in range(depth):
        q_gain = jnp.tile(row_vec(q_norm_g[l]), (1, N_KV_HEADS)) * (1.0 / math.sqrt(HEAD_DIM))
        k_gain = jnp.tile(row_vec(k_norm_g[l]), (1, N_KV_HEADS))
        qkv, u_tok = _inproj(h_res, row_vec(norm_mix_g[l]), w_in, l, head_ones, q_gain, k_gain)
        sink_row = jnp.broadcast_to(
            attn_sinks[l].astype(F32).reshape(N_KV_HEADS, 1, KV_GROUP, 1),
            (N_KV_HEADS, 1, KV_GROUP, SUB)).reshape(N_KV_HEADS, 1, KV_GROUP * SUB)
        attn_n = _attention(qkv.reshape(b, lp, QKV_WIDTH), attn_bias, attn_off, sink_row,
                            row_vec(attn_out_g[l]))
        g_tok = _ssm(u_tok, bb[l], cc[l], pw[l], a16[l], row_vec(ssm_d[l]), b)
        h_res, xn = _outproj(attn_n.reshape(rows, ATTN_WIDTH), g_tok, h_res, w_glu,
                             row_vec(b_glu[l]), row_vec(ssm_out_g[l]), w_out,
                             row_vec(norm_mlp_g[l]), l)
        h_res = _mlp(xn, h_res, w_up, w_down, l)
    return h_res.reshape(b, lp, d)[:, N_META:length]
```

```python
import functools
import math

import jax
import jax.numpy as jnp
from jax import lax
from jax.experimental import pallas as pl
from jax.experimental.pallas import tpu as pltpu

D_MODEL = 2048
N_META = 16
HEAD_DIM = 64
ATTN_WIDTH = D_MODEL // 2
N_HEADS = ATTN_WIDTH // HEAD_DIM
N_KV_HEADS = N_HEADS // 4
KV_GROUP = N_HEADS // N_KV_HEADS
KV_WIDTH = N_KV_HEADS * HEAD_DIM
SSM_WIDTH = D_MODEL - ATTN_WIDTH
SSM_GROUP_CH = 16
SSM_GROUPS = SSM_WIDTH // SSM_GROUP_CH
SSM_STATE = 64
WINDOW = 128
BLOCK = 128
D_FF = 4 * D_MODEL
IN_WIDTH = ATTN_WIDTH + 2 * KV_WIDTH + SSM_WIDTH
NORM_EPS = 1e-6
NEG_INF = -1e30

CHUNK = 16
CHUNK_W = CHUNK * SSM_GROUP_CH
GROUPS_PER_STEP = 8
RELAYOUT_UNROLL = 7

QKV_WIDTH = ATTN_WIDTH + 2 * KV_WIDTH
COL_K = ATTN_WIDTH
COL_V = ATTN_WIDTH + KV_WIDTH
LANES = 128
BLOCKS_PER_VREG = LANES // SSM_GROUP_CH

VMEM_LIMIT = 48 * 1024 * 1024
MLP_ROWS = 768
MLP_FF_TILE = 1024
MLP_FF_TILE_FIRST = 512
BIG_VMEM_LIMIT = 56 * 1024 * 1024

F32 = jnp.float32
BF16 = jnp.bfloat16


def _row_tile(rows, target):
    k = rows // BLOCK
    best = 1
    for d in range(1, k + 1):
        if k % d == 0 and d * BLOCK <= target:
            best = d
    return best * BLOCK


def _rms(x, gain):
    ms = jnp.mean(x * x, axis=-1, keepdims=True)
    return x * lax.rsqrt(ms + NORM_EPS) * gain


def _head_norm(x, head_ones, gain):
    x2 = x * x
    hi = x2.astype(BF16)
    lo = (x2 - hi.astype(F32)).astype(BF16)
    ssq = (jnp.dot(hi, head_ones, preferred_element_type=F32)
           + jnp.dot(lo, head_ones, preferred_element_type=F32))
    return x * lax.rsqrt(ssq * (1.0 / HEAD_DIM) + NORM_EPS) * gain


SUB = 64
BAND = WINDOW + SUB
KEYS = BAND + N_META
TILE = 3 * BLOCK


def _attn_bias(slopes):
    j = jnp.arange(KEYS)[:, None]
    i = jnp.arange(SUB)[None, :]
    is_band = j < BAND
    dist = WINDOW + i - j
    sl = slopes.reshape(N_KV_HEADS, 1, KV_GROUP, 1)
    dist_f = dist.astype(F32)[None, :, None, :]
    band = jnp.where(((dist >= 0) & (dist < WINDOW))[None, :, None, :], -sl * dist_f, NEG_INF)
    meta = -sl * (i - (j - BAND)).astype(F32)[None, :, None, :]
    bias = jnp.where(is_band[None, :, None, :], band, meta)
    lanes = KV_GROUP * SUB
    off_coef = jnp.broadcast_to(-sl, (N_KV_HEADS, N_META, KV_GROUP, SUB))
    return bias.reshape(N_KV_HEADS, KEYS, lanes), off_coef.reshape(N_KV_HEADS, N_META, lanes)


def _project_tile(x_ref, g_ref, w_bf, ones_ref, qg_ref, kg_ref, qkv_out, u_ref):
    h = _rms(x_ref[...], g_ref[...]).astype(BF16)
    proj = jnp.dot(h, w_bf[...], preferred_element_type=F32)
    head_ones = ones_ref[...]
    for blk in range(ATTN_WIDTH // KV_WIDTH):
        cols = slice(blk * KV_WIDTH, (blk + 1) * KV_WIDTH)
        qkv_out[:, cols] = _head_norm(proj[:, cols], head_ones, qg_ref[...]).astype(BF16)
    qkv_out[:, COL_K:COL_V] = _head_norm(proj[:, COL_K:COL_V], head_ones, kg_ref[...]).astype(BF16)
    qkv_out[:, COL_V:QKV_WIDTH] = proj[:, COL_V:QKV_WIDTH].astype(BF16)
    u_ref[...] = proj[:, QKV_WIDTH:]


def _attend_tile(qkv, tail, meta, bias_ref, offc_ref, sink_ref, og_ref, o_ref, acc, tile_pos, masked):
    lanes = KV_GROUP * SUB
    ones = jnp.ones((KEYS, 8), BF16)
    for blk in range(TILE // BLOCK):
        chains = [(sb, kh) for sb in range(BLOCK // SUB) for kh in range(N_KV_HEADS)]
        scores, values = [], []
        for sb, kh in chains:
            row0 = blk * BLOCK + sb * SUB
            start = tile_pos + row0
            first = row0 - WINDOW

            def window(col):
                cols = slice(col + kh * HEAD_DIM, col + (kh + 1) * HEAD_DIM)
                tcols = slice(cols.start - COL_K, cols.stop - COL_K)
                if first >= 0:
                    band = qkv[first:first + BAND, cols]
                else:
                    band = jnp.concatenate([tail[WINDOW + first:WINDOW, tcols],
                                            qkv[0:BAND + first, cols]], axis=0)
                return jnp.concatenate([band, meta[:, tcols]], axis=0)

            k_win = window(COL_K)
            values.append(jnp.concatenate([window(COL_V), ones], axis=1))
            q_s = jnp.concatenate(
                [qkv[row0:row0 + SUB, (kh * KV_GROUP + g) * HEAD_DIM:(kh * KV_GROUP + g + 1) * HEAD_DIM]
                 for g in range(KV_GROUP)], axis=0)
            s = lax.dot_general(k_win, q_s, (((1,), (1,)), ((), ())),
                                preferred_element_type=F32)
            s = s + bias_ref[kh]
            s = jnp.concatenate([s[:BAND], s[BAND:] + offc_ref[kh] * start.astype(F32)], axis=0)
            if masked:
                j = lax.broadcasted_iota(jnp.int32, (KEYS, lanes), 0)
                i = lax.broadcasted_iota(jnp.int32, (KEYS, lanes), 1) % SUB
                ok = jnp.where(j < BAND, start - WINDOW + j - N_META, start + i - (j - BAND)) >= 0
                s = jnp.where(ok, s, NEG_INF)
            scores.append(s)
        maxes = [jnp.maximum(jnp.max(s, axis=0, keepdims=True), sink_ref[kh])
                 for s, (sb, kh) in zip(scores, chains)]
        probs = [jnp.exp(s - m).astype(BF16) for s, m in zip(scores, maxes)]
        for (sb, kh), p, m, v_ext in zip(chains, probs, maxes, values):
            o_ext = lax.dot_general(v_ext, p, (((0,), (0,)), ((), ())),
                                    preferred_element_type=F32)
            denom = o_ext[HEAD_DIM:HEAD_DIM + 1] + jnp.exp(sink_ref[kh] - m)
            o = o_ext[:HEAD_DIM] / denom
            for g in range(KV_GROUP):
                h = kh * KV_GROUP + g
                acc[h * HEAD_DIM:(h + 1) * HEAD_DIM, sb * SUB:(sb + 1) * SUB] = o[:, g * SUB:(g + 1) * SUB]
        a_t = acc[...]
        ms = jnp.mean(a_t * a_t, axis=0, keepdims=True)
        a_n = a_t * lax.rsqrt(ms + NORM_EPS)
        o_ref[blk * BLOCK:(blk + 1) * BLOCK, :] = (a_n.T * og_ref[...]).astype(o_ref.dtype)


def _inproj_attn_kernel(x_ref, g_ref, w_ref, ones_ref, qg_ref, kg_ref, bias_ref, offc_ref,
                        sink_ref, og_ref, u_ref, attn_ref, w_bf, ring, tail, meta, acc,
                        *, tiles_per_seq, n_tiles):
    s = pl.program_id(0)

    @pl.when(s == 0)
    def _():
        w_bf[...] = w_ref[...].astype(BF16)
        ring[...] = jnp.zeros_like(ring)
        tail[...] = jnp.zeros_like(tail)
        meta[...] = jnp.zeros_like(meta)

    t_attn = jnp.maximum(s - 1, 0)
    seq_tile = t_attn % tiles_per_seq
    seq_slot = (t_attn // tiles_per_seq) % 2
    cur = (s + 1) % 2
    nxt = s % 2

    def step(masked):
        _attend_tile(ring.at[cur], tail, meta.at[seq_slot], bias_ref, offc_ref, sink_ref, og_ref,
                     attn_ref, acc, seq_tile * TILE, masked)
        tail[...] = ring[cur, TILE - WINDOW:TILE, COL_K:QKV_WIDTH]
        _project_tile(x_ref, g_ref, w_bf, ones_ref, qg_ref, kg_ref, ring.at[nxt], u_ref)

    @pl.when(seq_tile != 0)
    def _():
        step(False)

    @pl.when(seq_tile == 0)
    def _():
        step(True)

    t_proj = jnp.minimum(s, n_tiles - 1)

    @pl.when(t_proj % tiles_per_seq == 0)
    def _():
        meta[(t_proj // tiles_per_seq) % 2] = ring[nxt, 0:N_META, COL_K:QKV_WIDTH]


def _inproj_attention(x, gain, w_all, layer, head_ones, q_gain, k_gain, bias, off_coef, sink_row,
                      out_gain, n_seq):
    rows = x.shape[0]
    n_tiles = rows // TILE
    tiles_per_seq = n_tiles // n_seq
    assert rows == n_seq * tiles_per_seq * TILE
    kern = functools.partial(_inproj_attn_kernel, tiles_per_seq=tiles_per_seq, n_tiles=n_tiles)
    fixed = lambda s: (0, 0)
    fixed3 = lambda s: (0, 0, 0)
    proj_tile = lambda s: (jnp.minimum(s, n_tiles - 1), 0)
    return pl.pallas_call(
        kern,
        grid=(n_tiles + 1,),
        in_specs=[
            pl.BlockSpec((TILE, D_MODEL), proj_tile),
            pl.BlockSpec((1, D_MODEL), fixed),
            pl.BlockSpec((None, D_MODEL, IN_WIDTH), lambda s: (layer, 0, 0)),
            pl.BlockSpec((KV_WIDTH, KV_WIDTH), fixed),
            pl.BlockSpec((1, KV_WIDTH), fixed),
            pl.BlockSpec((1, KV_WIDTH), fixed),
            pl.BlockSpec(bias.shape, fixed3),
            pl.BlockSpec(off_coef.shape, fixed3),
            pl.BlockSpec(sink_row.shape, fixed3),
            pl.BlockSpec((1, ATTN_WIDTH), fixed),
        ],
        out_specs=[pl.BlockSpec((TILE, SSM_WIDTH), proj_tile),
                   pl.BlockSpec((TILE, ATTN_WIDTH), lambda s: (jnp.maximum(s - 1, 0), 0))],
        out_shape=[jax.ShapeDtypeStruct((rows, SSM_WIDTH), F32),
                   jax.ShapeDtypeStruct((rows, ATTN_WIDTH), BF16)],
        scratch_shapes=[
            pltpu.VMEM((D_MODEL, IN_WIDTH), BF16),
            pltpu.VMEM((2, TILE, QKV_WIDTH), BF16),
            pltpu.VMEM((WINDOW, 2 * KV_WIDTH), BF16),
            pltpu.VMEM((2, N_META, 2 * KV_WIDTH), BF16),
            pltpu.VMEM((ATTN_WIDTH, BLOCK), F32),
        ],
        compiler_params=pltpu.CompilerParams(
            dimension_semantics=("arbitrary",), vmem_limit_bytes=BIG_VMEM_LIMIT),
        name="inproj_attn",
    )(x, gain, w_all, head_ones, q_gain, k_gain, bias, off_coef, sink_row, out_gain)


def _gelu_tanh(x):
    c = math.sqrt(2.0 / math.pi)
    return 0.5 * x * (1.0 + jnp.tanh(c * (x + 0.044715 * (x * x * x))))


def _block_transpose_stage(arrs, dist, lane_blk):
    nblk = BLOCKS_PER_VREG
    keep = (lane_blk & dist) == 0
    nxt = list(arrs)
    for i in range(nblk):
        if i & dist == 0:
            lo, hi = arrs[i], arrs[i + dist]
            nxt[i] = jnp.where(keep, lo, pltpu.roll(hi, dist * SSM_GROUP_CH, axis=1))
            nxt[i + dist] = jnp.where(keep, pltpu.roll(lo, (nblk - dist) * SSM_GROUP_CH, axis=1), hi)
    return nxt


def _pipelined_block_transpose(n_tiles, load, store, zero, lane_blk):
    def trip(r, carry):
        after1, after2 = carry
        fresh = load(jnp.minimum(r, n_tiles - 1))
        new1 = [_block_transpose_stage(g, 4, lane_blk) for g in fresh]
        new2 = [_block_transpose_stage(g, 2, lane_blk) for g in after1]
        done = [_block_transpose_stage(g, 1, lane_blk) for g in after2]
        store(jnp.maximum(r - 2, 0), done)
        return new1, new2

    groups = CHUNK // BLOCKS_PER_VREG
    init = [[zero] * BLOCKS_PER_VREG for _ in range(groups)]
    lax.fori_loop(0, n_tiles + 2, trip, (init, init), unroll=RELAYOUT_UNROLL)


def _ssm_kernel(u_ref, bb_ref, cc_ref, pw_ref, a_ref, d_ref, o_ref,
                lhs, yscr, m_scr, win_scr, wsw_scr, wct_scr, zin, zsw, st,
                *, n_chunks, n_batch):
    nblk = BLOCKS_PER_VREG
    sw = 2 * SSM_STATE
    rc = n_batch * n_chunks
    half = CHUNK // nblk

    def tok_rows(tile, tile_chunks, t):
        return pl.ds(tile * (tile_chunks * CHUNK) + t, tile_chunks, stride=CHUNK)

    lane256 = lax.broadcasted_iota(jnp.int32, (CHUNK, CHUNK_W), 1)
    for g in range(GROUPS_PER_STEP):
        bb, bbs, bsw, bsws = bb_ref[g, 0], bb_ref[g, 1], bb_ref[g, 2], bb_ref[g, 3]
        cc, ccs = cc_ref[g, 0], cc_ref[g, 1]
        c_pow = [pw_ref[g, 0, tau:tau + 1, :] * cc + pw_ref[g, 1, tau:tau + 1, :] * ccs
                 for tau in range(CHUNK + 1)]
        for t in range(CHUNK):
            rows = slice(t * CHUNK, (t + 1) * CHUNK)
            p_re = pw_ref[g, 0, CHUNK - 1 - t:CHUNK - t, :]
            p_im = pw_ref[g, 1, CHUNK - 1 - t:CHUNK - t, :]
            win_scr[g, rows, :] = (p_re * bb + p_im * bbs).astype(BF16)
            wsw_scr[g, rows, :] = (p_re * bsw + p_im * bsws).astype(BF16)
            wct_scr[g, rows, :] = c_pow[t + 1].astype(BF16)
        base = lax.dot_general(bb, jnp.concatenate(c_pow[:CHUNK], axis=0), (((1,), (1,)), ((), ())),
                               precision=lax.Precision.HIGHEST, preferred_element_type=F32)
        for sblk in range(CHUNK):
            row = base if sblk == 0 else jnp.where(
                lane256 >= sblk * SSM_GROUP_CH, pltpu.roll(base, sblk * SSM_GROUP_CH, axis=1), 0.0)
            m_scr[g, sblk * CHUNK:(sblk + 1) * CHUNK, :] = row.astype(BF16)

    rt_in = 16
    lane_blk_in = lax.broadcasted_iota(jnp.int32, (rt_in // 2, LANES), 1) // SSM_GROUP_CH

    def load_u(r):
        return [[pltpu.bitcast(u_ref[tok_rows(r, rt_in, hf * nblk + tb), :].astype(BF16), jnp.uint32)
                 for tb in range(nblk)] for hf in range(half)]

    def store_lhs(r, groups):
        rows = pl.ds(pl.multiple_of(r * rt_in, rt_in), rt_in)
        for hf in range(half):
            for g in range(nblk):
                col = g * CHUNK_W + hf * LANES
                lhs[rows, col:col + LANES] = pltpu.bitcast(groups[hf][g], BF16)

    _pipelined_block_transpose(rc // rt_in, load_u, store_lhs,
                               jnp.zeros((rt_in // 2, LANES), jnp.uint32), lane_blk_in)

    for g in range(GROUPS_PER_STEP):
        u_g = lhs[:, g * CHUNK_W:(g + 1) * CHUNK_W]
        zin[:, g * sw:(g + 1) * sw] = jnp.dot(u_g, win_scr[g], preferred_element_type=F32)
        zsw[:, g * sw:(g + 1) * sw] = jnp.dot(u_g, wsw_scr[g], preferred_element_type=F32)

    a1 = a_ref[0:1, :]
    a2 = a_ref[1:2, :]

    def step(c, carry):
        nxt = []
        for b in range(n_batch):
            s, x = carry[b]
            row = b * n_chunks + c
            st[pl.ds(row, 1), :] = s
            nxt.append((a1 * s + a2 * x + zin[pl.ds(row, 1), :],
                        a1 * x - a2 * s + zsw[pl.ds(row, 1), :]))
        return tuple(nxt)

    zero = jnp.zeros((1, GROUPS_PER_STEP * sw), F32)
    lax.fori_loop(0, n_chunks, step, tuple((zero, zero) for _ in range(n_batch)))

    for g in range(GROUPS_PER_STEP):
        cols = slice(g * CHUNK_W, (g + 1) * CHUNK_W)
        y = jnp.dot(lhs[:, cols], m_scr[g], preferred_element_type=F32)
        y = y + lax.dot_general(st[:, g * sw:(g + 1) * sw].astype(BF16), wct_scr[g],
                                (((1,), (1,)), ((), ())), preferred_element_type=F32)
        yscr[:, cols] = y

    rt_out = 8
    d_row = d_ref[...]
    lane_blk_out = lax.broadcasted_iota(jnp.int32, (rt_out, LANES), 1) // SSM_GROUP_CH

    def load_y(r):
        rows = pl.ds(pl.multiple_of(r * rt_out, rt_out), rt_out)
        return [[yscr[rows, g * CHUNK_W + hf * LANES:g * CHUNK_W + (hf + 1) * LANES]
                 for g in range(nblk)] for hf in range(half)]

    def store_tokens(r, groups):
        for hf in range(half):
            for tb in range(nblk):
                rows = tok_rows(r, rt_out, hf * nblk + tb)
                o_ref[rows, :] = _gelu_tanh(groups[hf][tb] + d_row * u_ref[rows, :])

    _pipelined_block_transpose(rc // rt_out, load_y, store_tokens,
                               jnp.zeros((rt_out, LANES), F32), lane_blk_out)


def _ssm(u_tok, bb, cc, pw, a16, d, n_batch):
    rows, width = u_tok.shape
    rc = rows // CHUNK
    n_chunks = rc // n_batch
    assert GROUPS_PER_STEP == BLOCKS_PER_VREG and rc % 16 == 0
    gs = GROUPS_PER_STEP
    bw = gs * CHUNK_W
    sw = gs * 2 * SSM_STATE
    kern = functools.partial(_ssm_kernel, n_chunks=n_chunks, n_batch=n_batch)
    grp4 = lambda j: (j, 0, 0, 0)
    return pl.pallas_call(
        kern,
        grid=(SSM_GROUPS // gs,),
        in_specs=[
            pl.BlockSpec((rows, LANES), lambda j: (0, j)),
            pl.BlockSpec((gs,) + bb.shape[1:], grp4),
            pl.BlockSpec((gs,) + cc.shape[1:], grp4),
            pl.BlockSpec((gs,) + pw.shape[1:], grp4),
            pl.BlockSpec((2, sw), lambda j: (0, j)),
            pl.BlockSpec((1, LANES), lambda j: (0, j)),
        ],
        out_specs=pl.BlockSpec((rows, LANES), lambda j: (0, j)),
        out_shape=jax.ShapeDtypeStruct((rows, width), F32),
        scratch_shapes=[
            pltpu.VMEM((rc, bw), BF16),
            pltpu.VMEM((rc, bw), F32),
            pltpu.VMEM((gs, CHUNK_W, CHUNK_W), BF16),
            pltpu.VMEM((gs, CHUNK_W, 2 * SSM_STATE), BF16),
            pltpu.VMEM((gs, CHUNK_W, 2 * SSM_STATE), BF16),
            pltpu.VMEM((gs, CHUNK_W, 2 * SSM_STATE), BF16),
            pltpu.VMEM((rc, sw), F32),
            pltpu.VMEM((rc, sw), F32),
            pltpu.VMEM((rc, sw), F32),
        ],
        compiler_params=pltpu.CompilerParams(
            dimension_semantics=("parallel",), vmem_limit_bytes=VMEM_LIMIT),
        name="ssm",
    )(u_tok, bb, cc, pw, a16, d)


def _ssm_tables(lam_re, lam_im, log_step, b_re, b_im, c_re, c_im):
    depth = lam_re.shape[0]
    g, p, h, t = SSM_GROUPS, SSM_STATE, SSM_GROUP_CH, CHUNK
    lam = lax.complex(lam_re.astype(F32), lam_im.astype(F32))
    delta = jnp.exp(log_step.astype(F32))[..., None]
    ld = lam * delta
    lam_bar = jnp.exp(ld)
    b_bar = ((lam_bar - 1.0) / lam)[..., None] * lax.complex(b_re.astype(F32), b_im.astype(F32))
    c_c = lax.complex(c_re.astype(F32), c_im.astype(F32))
    tau = jnp.arange(t + 1, dtype=F32)
    powers = jnp.exp(ld[:, :, None, :] * tau[None, None, :, None])

    cat = lambda x, y: jnp.concatenate([x, y], axis=-1)
    br, bi = jnp.real(b_bar).swapaxes(-1, -2), jnp.imag(b_bar).swapaxes(-1, -2)
    bb = jnp.stack([cat(br, bi), cat(-bi, br), cat(bi, br), cat(br, -bi)], axis=2)
    cr, ci = jnp.real(c_c), jnp.imag(c_c)
    cc = jnp.stack([cat(cr, -ci), cat(-ci, -cr)], axis=2)
    pr, pi = jnp.real(powers), jnp.imag(powers)
    pw = jnp.stack([cat(pr, pr), cat(pi, pi)], axis=2)
    pw = jnp.pad(pw, ((0, 0), (0, 0), (0, 0), (0, 24 - (t + 1)), (0, 0)))
    ar, ai = pr[:, :, t], pi[:, :, t]
    a16 = jnp.stack([cat(ar, ar).reshape(depth, g * 2 * p),
                     cat(-ai, ai).reshape(depth, g * 2 * p)], axis=1)
    return bb, cc, pw, a16


def _outproj_kernel(a_ref, g_ref, x_ref, wglu_f32, bglu_ref, sg_ref, wout_f32, ng_ref,
                    o_ref, xn_ref, wglu_ref, wout_ref):
    @pl.when(pl.program_id(0) == 0)
    def _():
        wglu_ref[...] = wglu_f32[...].astype(BF16)
        wout_ref[...] = wout_f32[...].astype(BF16)

    g = g_ref[...]
    z = jnp.dot(g.astype(BF16), wglu_ref[...], preferred_element_type=F32) + bglu_ref[...]
    s = g * jax.nn.sigmoid(z)
    s_n = _rms(s, sg_ref[...]).astype(BF16)
    y = jnp.dot(a_ref[...], wout_ref[:ATTN_WIDTH, :], preferred_element_type=F32)
    y = y + jnp.dot(s_n, wout_ref[ATTN_WIDTH:, :], preferred_element_type=F32)
    h = x_ref[...] + y
    o_ref[...] = h
    xn_ref[...] = _rms(h, ng_ref[...]).astype(xn_ref.dtype)


def _outproj(attn_n, g_tok, x, w_glu_all, b_glu, ssm_gain, w_out_all, mlp_gain, layer):
    rows = x.shape[0]
    tm = _row_tile(rows, 384)
    row = lambda i: (i, 0)
    fixed = lambda i: (0, 0)
    return pl.pallas_call(
        _outproj_kernel,
        grid=(rows // tm,),
        in_specs=[
            pl.BlockSpec((tm, ATTN_WIDTH), row),
            pl.BlockSpec((tm, SSM_WIDTH), row),
            pl.BlockSpec((tm, D_MODEL), row),
            pl.BlockSpec((None, SSM_WIDTH, SSM_WIDTH), lambda i: (layer, 0, 0)),
            pl.BlockSpec((1, SSM_WIDTH), fixed),
            pl.BlockSpec((1, SSM_WIDTH), fixed),
            pl.BlockSpec((None, D_MODEL, D_MODEL), lambda i: (layer, 0, 0)),
            pl.BlockSpec((1, D_MODEL), fixed),
        ],
        out_specs=[pl.BlockSpec((tm, D_MODEL), row), pl.BlockSpec((tm, D_MODEL), row)],
        out_shape=[jax.ShapeDtypeStruct((rows, D_MODEL), F32),
                   jax.ShapeDtypeStruct((rows, D_MODEL), BF16)],
        scratch_shapes=[pltpu.VMEM((SSM_WIDTH, SSM_WIDTH), BF16), pltpu.VMEM((D_MODEL, D_MODEL), BF16)],
        compiler_params=pltpu.CompilerParams(
            dimension_semantics=("arbitrary",), vmem_limit_bytes=BIG_VMEM_LIMIT),
        name="outproj",
    )(attn_n, g_tok, x, w_glu_all, b_glu, ssm_gain, w_out_all, mlp_gain)


def _mlp_tile(xn_ref, x_ref, w_up, w_down, o_ref):
    a = jnp.dot(xn_ref[...], w_up, preferred_element_type=F32)
    a = jnp.square(jnp.maximum(a, 0.0)).astype(BF16)
    o_ref[...] += jnp.dot(a, w_down, preferred_element_type=F32)


def _mlp_first_kernel(xn_ref, x_ref, wu_ref, wd_ref, o_ref, wu_bf_ref, wd_bf_ref):
    @pl.when(pl.program_id(0) == 0)
    def _():
        o_ref[...] = x_ref[...]

    wu_bf_ref[...] = wu_ref[...].astype(BF16)
    wd_bf_ref[...] = wd_ref[...].astype(BF16)
    _mlp_tile(xn_ref, x_ref, wu_bf_ref[...], wd_bf_ref[...], o_ref)


def _mlp_rest_kernel(xn_ref, x_ref, wu_ref, wd_ref, first_hbm, o_ref, sem):
    i = pl.program_id(0)
    f = pl.program_id(1)

    @pl.when((i == 0) & (f == 0))
    def _():
        copy = pltpu.make_async_copy(first_hbm, o_ref, sem)
        copy.start()
        copy.wait()

    @pl.when(i > 0)
    def _():
        @pl.when(f == 0)
        def _():
            o_ref[...] = x_ref[...]

        _mlp_tile(xn_ref, x_ref, wu_ref[...], wd_ref[...], o_ref)


def _mlp(xn, x, w_up_all, w_down_all, layer):
    rows = x.shape[0]
    tm = _row_tile(rows, MLP_ROWS)
    tf_first, tf = MLP_FF_TILE_FIRST, MLP_FF_TILE
    first_rows = lambda f: (0, 0)
    first, w_up_bf, w_down_bf = pl.pallas_call(
        _mlp_first_kernel,
        grid=(D_FF // tf_first,),
        in_specs=[
            pl.BlockSpec((tm, D_MODEL), first_rows),
            pl.BlockSpec((tm, D_MODEL), first_rows),
            pl.BlockSpec((None, D_MODEL, tf_first), lambda f: (layer, 0, f)),
            pl.BlockSpec((None, tf_first, D_MODEL), lambda f: (layer, f, 0)),
        ],
        out_specs=[pl.BlockSpec((tm, D_MODEL), first_rows),
                   pl.BlockSpec((D_MODEL, tf_first), lambda f: (0, f)),
                   pl.BlockSpec((tf_first, D_MODEL), lambda f: (f, 0))],
        out_shape=[jax.ShapeDtypeStruct((tm, D_MODEL), F32),
                   jax.ShapeDtypeStruct((D_MODEL, D_FF), BF16),
                   jax.ShapeDtypeStruct((D_FF, D_MODEL), BF16)],
        compiler_params=pltpu.CompilerParams(
            dimension_semantics=("arbitrary",), vmem_limit_bytes=BIG_VMEM_LIMIT),
        name="mlp_first",
    )(xn, x, w_up_all, w_down_all)
    if rows == tm:
        return first
    ff = lambda i, f: jnp.where(i == 0, 0, f)
    return pl.pallas_call(
        _mlp_rest_kernel,
        grid=(rows // tm, D_FF // tf),
        in_specs=[
            pl.BlockSpec((tm, D_MODEL), lambda i, f: (jnp.maximum(i, 1), 0)),
            pl.BlockSpec((tm, D_MODEL), lambda i, f: (jnp.maximum(i, 1), 0)),
            pl.BlockSpec((D_MODEL, tf), lambda i, f: (0, ff(i, f))),
            pl.BlockSpec((tf, D_MODEL), lambda i, f: (ff(i, f), 0)),
            pl.BlockSpec(memory_space=pl.ANY),
        ],
        out_specs=pl.BlockSpec((tm, D_MODEL), lambda i, f: (i, 0)),
        out_shape=jax.ShapeDtypeStruct((rows, D_MODEL), F32),
        scratch_shapes=[pltpu.SemaphoreType.DMA(())],
        compiler_params=pltpu.CompilerParams(
            dimension_semantics=("arbitrary", "arbitrary"), vmem_limit_bytes=BIG_VMEM_LIMIT),
        name="mlp_rest",
    )(xn, x, w_up_bf, w_down_bf, first)


def kernel(x, meta_tokens, norm_mix_g, w_in, q_norm_g, k_norm_g, attn_sinks, ssm_lambda_re, ssm_lambda_im, ssm_log_step, ssm_b_re, ssm_b_im, ssm_c_re, ssm_c_im, ssm_d, w_glu, b_glu, attn_out_g, ssm_out_g, w_out, norm_mlp_g, w_up, w_down):
    b, seq, d = x.shape
    depth = w_in.shape[0]
    assert d == D_MODEL
    length = N_META + seq
    lp = -(-length // TILE) * TILE
    rows = b * lp

    meta = jnp.broadcast_to(meta_tokens.astype(x.dtype)[None], (b, N_META, d))
    pad = jnp.zeros((b, lp - length, d), x.dtype)
    h_res = jnp.concatenate([meta, x, pad], axis=1).reshape(rows, d)

    bb, cc, pw, a16 = _ssm_tables(
        ssm_lambda_re, ssm_lambda_im, ssm_log_step, ssm_b_re, ssm_b_im, ssm_c_re, ssm_c_im)

    head_id = jnp.arange(KV_WIDTH) // HEAD_DIM
    head_ones = (head_id[:, None] == head_id[None, :]).astype(BF16)
    slopes = jnp.exp2(-8.0 * jnp.arange(1, N_HEADS + 1, dtype=F32) / N_HEADS)
    attn_bias, attn_off = _attn_bias(slopes)

    row_vec = lambda v: v.astype(F32).reshape(1, -1)
    for l in range(depth):
        q_gain = jnp.tile(row_vec(q_norm_g[l]), (1, N_KV_HEADS)) * (1.0 / math.sqrt(HEAD_DIM))
        k_gain = jnp.tile(row_vec(k_norm_g[l]), (1, N_KV_HEADS))
        sink_row = jnp.broadcast_to(
            attn_sinks[l].astype(F32).reshape(N_KV_HEADS, 1, KV_GROUP, 1),
            (N_KV_HEADS, 1, KV_GROUP, SUB)).reshape(N_KV_HEADS, 1, KV_GROUP * SUB)
        u_tok, attn_n = _inproj_attention(
            h_res, row_vec(norm_mix_g[l]), w_in, l, head_ones, q_gain, k_gain, attn_bias, attn_off,
            sink_row, row_vec(attn_out_g[l]), b)
        g_tok = _ssm(u_tok, bb[l], cc[l], pw[l], a16[l], row_vec(ssm_d[l]), b)
        h_res, xn = _outproj(attn_n, g_tok, h_res, w_glu,
                             row_vec(b_glu[l]), row_vec(ssm_out_g[l]), w_out,
                             row_vec(norm_mlp_g[l]), l)
        h_res = _mlp(xn, h_res, w_up, w_down, l)
    return h_res.reshape(b, lp, d)[:, N_META:length]
```

```python
import functools
import math

import jax
import jax.numpy as jnp
from jax import lax
from jax.experimental import pallas as pl
from jax.experimental.pallas import tpu as pltpu

D_MODEL = 2048
N_META = 16
HEAD_DIM = 64
ATTN_WIDTH = D_MODEL // 2
N_HEADS = ATTN_WIDTH // HEAD_DIM
N_KV_HEADS = N_HEADS // 4
KV_GROUP = N_HEADS // N_KV_HEADS
KV_WIDTH = N_KV_HEADS * HEAD_DIM
SSM_WIDTH = D_MODEL - ATTN_WIDTH
SSM_GROUP_CH = 16
SSM_GROUPS = SSM_WIDTH // SSM_GROUP_CH
SSM_STATE = 64
WINDOW = 128
BLOCK = 128
D_FF = 4 * D_MODEL
IN_WIDTH = ATTN_WIDTH + 2 * KV_WIDTH + SSM_WIDTH
NORM_EPS = 1e-6
NEG_INF = -1e30
LOG2E = math.log2(math.e)

CHUNK = 16
CHUNK_W = CHUNK * SSM_GROUP_CH
GROUPS_PER_STEP = 8
RELAYOUT_UNROLL = 7

QKV_WIDTH = ATTN_WIDTH + 2 * KV_WIDTH
COL_K = ATTN_WIDTH
COL_V = ATTN_WIDTH + KV_WIDTH
LANES = 128
BLOCKS_PER_VREG = LANES // SSM_GROUP_CH

VMEM_LIMIT = 48 * 1024 * 1024
MLP_ROWS = 768
MLP_FF_TILE = 1024
MLP_FF_TILE_FIRST = 512
BIG_VMEM_LIMIT = 56 * 1024 * 1024

F32 = jnp.float32
BF16 = jnp.bfloat16


def _row_tile(rows, target):
    k = rows // BLOCK
    best = 1
    for d in range(1, k + 1):
        if k % d == 0 and d * BLOCK <= target:
            best = d
    return best * BLOCK


def _rms(x, gain):
    ms = jnp.mean(x * x, axis=-1, keepdims=True)
    return x * lax.rsqrt(ms + NORM_EPS) * gain


def _head_norm(x, head_ones, gain):
    x2 = x * x
    hi = x2.astype(BF16)
    lo = (x2 - hi.astype(F32)).astype(BF16)
    ssq = (jnp.dot(hi, head_ones, preferred_element_type=F32)
           + jnp.dot(lo, head_ones, preferred_element_type=F32))
    return x * lax.rsqrt(ssq * (1.0 / HEAD_DIM) + NORM_EPS) * gain


SUB = 64
BAND = WINDOW + SUB
KEYS = BAND + N_META
TILE = 3 * BLOCK


def _attn_bias(slopes):
    j = jnp.arange(KEYS)[:, None]
    i = jnp.arange(SUB)[None, :]
    is_band = j < BAND
    dist = WINDOW + i - j
    sl = slopes.reshape(N_KV_HEADS, 1, KV_GROUP, 1)
    dist_f = dist.astype(F32)[None, :, None, :]
    band = jnp.where(((dist >= 0) & (dist < WINDOW))[None, :, None, :], -sl * dist_f, NEG_INF)
    meta = -sl * (i - (j - BAND)).astype(F32)[None, :, None, :]
    bias = jnp.where(is_band[None, :, None, :], band, meta) * LOG2E
    lanes = KV_GROUP * SUB
    off_coef = jnp.broadcast_to(-sl * LOG2E, (N_KV_HEADS, N_META, KV_GROUP, SUB))
    return bias.reshape(N_KV_HEADS, KEYS, lanes), off_coef.reshape(N_KV_HEADS, N_META, lanes)


def _project_tile(x_ref, g_ref, w_bf, ones_ref, qg_ref, kg_ref, qkv_out, u_ref):
    h = _rms(x_ref[...], g_ref[...]).astype(BF16)
    proj = jnp.dot(h, w_bf[...], preferred_element_type=F32)
    head_ones = ones_ref[...]
    for blk in range(ATTN_WIDTH // KV_WIDTH):
        cols = slice(blk * KV_WIDTH, (blk + 1) * KV_WIDTH)
        qkv_out[:, cols] = _head_norm(proj[:, cols], head_ones, qg_ref[...]).astype(BF16)
    qkv_out[:, COL_K:COL_V] = _head_norm(proj[:, COL_K:COL_V], head_ones, kg_ref[...]).astype(BF16)
    qkv_out[:, COL_V:QKV_WIDTH] = proj[:, COL_V:QKV_WIDTH].astype(BF16)
    u_ref[...] = proj[:, QKV_WIDTH:]


def _attend_tile(qkv, tail, meta, bias_ref, offc_ref, sink_ref, og_ref, o_ref, acc, tile_pos, masked):
    lanes = KV_GROUP * SUB
    ones = jnp.ones((KEYS, 8), BF16)
    for blk in range(TILE // BLOCK):
        chains = [(sb, kh) for sb in range(BLOCK // SUB) for kh in range(N_KV_HEADS)]
        scores, values = [], []
        for sb, kh in chains:
            row0 = blk * BLOCK + sb * SUB
            start = tile_pos + row0
            first = row0 - WINDOW

            def window(col):
                cols = slice(col + kh * HEAD_DIM, col + (kh + 1) * HEAD_DIM)
                tcols = slice(cols.start - COL_K, cols.stop - COL_K)
                if first >= 0:
                    band = qkv[first:first + BAND, cols]
                else:
                    band = jnp.concatenate([tail[WINDOW + first:WINDOW, tcols],
                                            qkv[0:BAND + first, cols]], axis=0)
                return jnp.concatenate([band, meta[:, tcols]], axis=0)

            k_win = window(COL_K)
            values.append(jnp.concatenate([window(COL_V), ones], axis=1))
            q_s = jnp.concatenate(
                [qkv[row0:row0 + SUB, (kh * KV_GROUP + g) * HEAD_DIM:(kh * KV_GROUP + g + 1) * HEAD_DIM]
                 for g in range(KV_GROUP)], axis=0)
            s = lax.dot_general(k_win, q_s, (((1,), (1,)), ((), ())),
                                preferred_element_type=F32)
            s = s + bias_ref[kh]
            s = jnp.concatenate([s[:BAND], s[BAND:] + offc_ref[kh] * start.astype(F32)], axis=0)
            if masked:
                j = lax.broadcasted_iota(jnp.int32, (KEYS, lanes), 0)
                i = lax.broadcasted_iota(jnp.int32, (KEYS, lanes), 1) % SUB
                ok = jnp.where(j < BAND, start - WINDOW + j - N_META, start + i - (j - BAND)) >= 0
                s = jnp.where(ok, s, NEG_INF)
            scores.append(s)
        maxes = [jnp.maximum(jnp.max(s, axis=0, keepdims=True), sink_ref[kh])
                 for s, (sb, kh) in zip(scores, chains)]
        probs = [jnp.exp2(s - m).astype(BF16) for s, m in zip(scores, maxes)]
        for (sb, kh), p, m, v_ext in zip(chains, probs, maxes, values):
            o_ext = lax.dot_general(v_ext, p, (((0,), (0,)), ((), ())),
                                    preferred_element_type=F32)
            denom = o_ext[HEAD_DIM:HEAD_DIM + 1] + jnp.exp2(sink_ref[kh] - m)
            o = o_ext[:HEAD_DIM] / denom
            for g in range(KV_GROUP):
                h = kh * KV_GROUP + g
                acc[h * HEAD_DIM:(h + 1) * HEAD_DIM, sb * SUB:(sb + 1) * SUB] = o[:, g * SUB:(g + 1) * SUB]
        a_t = acc[...]
        ms = jnp.mean(a_t * a_t, axis=0, keepdims=True)
        a_n = a_t * lax.rsqrt(ms + NORM_EPS)
        o_ref[blk * BLOCK:(blk + 1) * BLOCK, :] = (a_n.T * og_ref[...]).astype(o_ref.dtype)


def _inproj_attn_kernel(x_ref, g_ref, w_ref, ones_ref, qg_ref, kg_ref, bias_ref, offc_ref,
                        sink_ref, og_ref, u_ref, attn_ref, w_bf, ring, tail, meta, acc,
                        *, tiles_per_seq, n_tiles):
    s = pl.program_id(0)

    @pl.when(s == 0)
    def _():
        w_bf[...] = w_ref[...].astype(BF16)
        ring[...] = jnp.zeros_like(ring)
        tail[...] = jnp.zeros_like(tail)
        meta[...] = jnp.zeros_like(meta)

    t_attn = jnp.maximum(s - 1, 0)
    seq_tile = t_attn % tiles_per_seq
    seq_slot = (t_attn // tiles_per_seq) % 2
    cur = (s + 1) % 2
    nxt = s % 2

    def step(masked):
        _attend_tile(ring.at[cur], tail, meta.at[seq_slot], bias_ref, offc_ref, sink_ref, og_ref,
                     attn_ref, acc, seq_tile * TILE, masked)
        tail[...] = ring[cur, TILE - WINDOW:TILE, COL_K:QKV_WIDTH]
        _project_tile(x_ref, g_ref, w_bf, ones_ref, qg_ref, kg_ref, ring.at[nxt], u_ref)

    @pl.when(seq_tile != 0)
    def _():
        step(False)

    @pl.when(seq_tile == 0)
    def _():
        step(True)

    t_proj = jnp.minimum(s, n_tiles - 1)

    @pl.when(t_proj % tiles_per_seq == 0)
    def _():
        meta[(t_proj // tiles_per_seq) % 2] = ring[nxt, 0:N_META, COL_K:QKV_WIDTH]


def _inproj_attention(x, gain, w_all, layer, head_ones, q_gain, k_gain, bias, off_coef, sink_row,
                      out_gain, n_seq):
    rows = x.shape[0]
    n_tiles = rows // TILE
    tiles_per_seq = n_tiles // n_seq
    assert rows == n_seq * tiles_per_seq * TILE
    kern = functools.partial(_inproj_attn_kernel, tiles_per_seq=tiles_per_seq, n_tiles=n_tiles)
    fixed = lambda s: (0, 0)
    fixed3 = lambda s: (0, 0, 0)
    proj_tile = lambda s: (jnp.minimum(s, n_tiles - 1), 0)
    return pl.pallas_call(
        kern,
        grid=(n_tiles + 1,),
        in_specs=[
            pl.BlockSpec((TILE, D_MODEL), proj_tile),
            pl.BlockSpec((1, D_MODEL), fixed),
            pl.BlockSpec((None, D_MODEL, IN_WIDTH), lambda s: (layer, 0, 0)),
            pl.BlockSpec((KV_WIDTH, KV_WIDTH), fixed),
            pl.BlockSpec((1, KV_WIDTH), fixed),
            pl.BlockSpec((1, KV_WIDTH), fixed),
            pl.BlockSpec(bias.shape, fixed3),
            pl.BlockSpec(off_coef.shape, fixed3),
            pl.BlockSpec(sink_row.shape, fixed3),
            pl.BlockSpec((1, ATTN_WIDTH), fixed),
        ],
        out_specs=[pl.BlockSpec((TILE, SSM_WIDTH), proj_tile),
                   pl.BlockSpec((TILE, ATTN_WIDTH), lambda s: (jnp.maximum(s - 1, 0), 0))],
        out_shape=[jax.ShapeDtypeStruct((rows, SSM_WIDTH), F32),
                   jax.ShapeDtypeStruct((rows, ATTN_WIDTH), BF16)],
        scratch_shapes=[
            pltpu.VMEM((D_MODEL, IN_WIDTH), BF16),
            pltpu.VMEM((2, TILE, QKV_WIDTH), BF16),
            pltpu.VMEM((WINDOW, 2 * KV_WIDTH), BF16),
            pltpu.VMEM((2, N_META, 2 * KV_WIDTH), BF16),
            pltpu.VMEM((ATTN_WIDTH, BLOCK), F32),
        ],
        compiler_params=pltpu.CompilerParams(
            dimension_semantics=("arbitrary",), vmem_limit_bytes=BIG_VMEM_LIMIT),
        name="inproj_attn",
    )(x, gain, w_all, head_ones, q_gain, k_gain, bias, off_coef, sink_row, out_gain)


def _gelu_tanh(x):
    c = math.sqrt(2.0 / math.pi)
    return 0.5 * x * (1.0 + jnp.tanh(c * (x + 0.044715 * (x * x * x))))


def _block_transpose_stage(arrs, dist, lane_blk):
    nblk = BLOCKS_PER_VREG
    keep = (lane_blk & dist) == 0
    nxt = list(arrs)
    for i in range(nblk):
        if i & dist == 0:
            lo, hi = arrs[i], arrs[i + dist]
            nxt[i] = jnp.where(keep, lo, pltpu.roll(hi, dist * SSM_GROUP_CH, axis=1))
            nxt[i + dist] = jnp.where(keep, pltpu.roll(lo, (nblk - dist) * SSM_GROUP_CH, axis=1), hi)
    return nxt


def _pipelined_block_transpose(n_tiles, load, store, zero, lane_blk):
    def trip(r, carry):
        after1, after2 = carry
        fresh = load(jnp.minimum(r, n_tiles - 1))
        new1 = [_block_transpose_stage(g, 4, lane_blk) for g in fresh]
        new2 = [_block_transpose_stage(g, 2, lane_blk) for g in after1]
        done = [_block_transpose_stage(g, 1, lane_blk) for g in after2]
        store(jnp.maximum(r - 2, 0), done)
        return new1, new2

    groups = CHUNK // BLOCKS_PER_VREG
    init = [[zero] * BLOCKS_PER_VREG for _ in range(groups)]
    lax.fori_loop(0, n_tiles + 2, trip, (init, init), unroll=RELAYOUT_UNROLL)


def _ssm_kernel(u_ref, bb_ref, cc_ref, pw_ref, a_ref, d_ref, o_ref,
                lhs, yscr, m_scr, win_scr, wsw_scr, wct_scr, zin, zsw, st,
                *, n_chunks, n_batch):
    nblk = BLOCKS_PER_VREG
    sw = 2 * SSM_STATE
    rc = n_batch * n_chunks
    half = CHUNK // nblk

    def tok_rows(tile, tile_chunks, t):
        return pl.ds(tile * (tile_chunks * CHUNK) + t, tile_chunks, stride=CHUNK)

    lane256 = lax.broadcasted_iota(jnp.int32, (CHUNK, CHUNK_W), 1)
    for g in range(GROUPS_PER_STEP):
        bb, bbs, bsw, bsws = bb_ref[g, 0], bb_ref[g, 1], bb_ref[g, 2], bb_ref[g, 3]
        cc, ccs = cc_ref[g, 0], cc_ref[g, 1]
        c_pow = [pw_ref[g, 0, tau:tau + 1, :] * cc + pw_ref[g, 1, tau:tau + 1, :] * ccs
                 for tau in range(CHUNK + 1)]
        for t in range(CHUNK):
            rows = slice(t * CHUNK, (t + 1) * CHUNK)
            p_re = pw_ref[g, 0, CHUNK - 1 - t:CHUNK - t, :]
            p_im = pw_ref[g, 1, CHUNK - 1 - t:CHUNK - t, :]
            win_scr[g, rows, :] = (p_re * bb + p_im * bbs).astype(BF16)
            wsw_scr[g, rows, :] = (p_re * bsw + p_im * bsws).astype(BF16)
            wct_scr[g, rows, :] = c_pow[t + 1].astype(BF16)
        base = lax.dot_general(bb, jnp.concatenate(c_pow[:CHUNK], axis=0), (((1,), (1,)), ((), ())),
                               precision=lax.Precision.HIGHEST, preferred_element_type=F32)
        for sblk in range(CHUNK):
            row = base if sblk == 0 else jnp.where(
                lane256 >= sblk * SSM_GROUP_CH, pltpu.roll(base, sblk * SSM_GROUP_CH, axis=1), 0.0)
            m_scr[g, sblk * CHUNK:(sblk + 1) * CHUNK, :] = row.astype(BF16)

    rt_in = 16
    lane_blk_in = lax.broadcasted_iota(jnp.int32, (rt_in // 2, LANES), 1) // SSM_GROUP_CH

    def load_u(r):
        return [[pltpu.bitcast(u_ref[tok_rows(r, rt_in, hf * nblk + tb), :].astype(BF16), jnp.uint32)
                 for tb in range(nblk)] for hf in range(half)]

    def store_lhs(r, groups):
        rows = pl.ds(pl.multiple_of(r * rt_in, rt_in), rt_in)
        for hf in range(half):
            for g in range(nblk):
                col = g * CHUNK_W + hf * LANES
                lhs[rows, col:col + LANES] = pltpu.bitcast(groups[hf][g], BF16)

    _pipelined_block_transpose(rc // rt_in, load_u, store_lhs,
                               jnp.zeros((rt_in // 2, LANES), jnp.uint32), lane_blk_in)

    for g in range(GROUPS_PER_STEP):
        u_g = lhs[:, g * CHUNK_W:(g + 1) * CHUNK_W]
        zin[:, g * sw:(g + 1) * sw] = jnp.dot(u_g, win_scr[g], preferred_element_type=F32)
        zsw[:, g * sw:(g + 1) * sw] = jnp.dot(u_g, wsw_scr[g], preferred_element_type=F32)

    a1 = a_ref[0:1, :]
    a2 = a_ref[1:2, :]

    def step(c, carry):
        nxt = []
        for b in range(n_batch):
            s, x = carry[b]
            row = b * n_chunks + c
            st[pl.ds(row, 1), :] = s
            nxt.append((a1 * s + a2 * x + zin[pl.ds(row, 1), :],
                        a1 * x - a2 * s + zsw[pl.ds(row, 1), :]))
        return tuple(nxt)

    zero = jnp.zeros((1, GROUPS_PER_STEP * sw), F32)
    lax.fori_loop(0, n_chunks, step, tuple((zero, zero) for _ in range(n_batch)))

    for g in range(GROUPS_PER_STEP):
        cols = slice(g * CHUNK_W, (g + 1) * CHUNK_W)
        y = jnp.dot(lhs[:, cols], m_scr[g], preferred_element_type=F32)
        y = y + lax.dot_general(st[:, g * sw:(g + 1) * sw].astype(BF16), wct_scr[g],
                                (((1,), (1,)), ((), ())), preferred_element_type=F32)
        yscr[:, cols] = y.astype(BF16)

    d_row = d_ref[...]

    def load_y(r):
        rows = pl.ds(pl.multiple_of(r * rt_in, rt_in), rt_in)
        return [[pltpu.bitcast(yscr[rows, g * CHUNK_W + hf * LANES:g * CHUNK_W + (hf + 1) * LANES],
                               jnp.uint32)
                 for g in range(nblk)] for hf in range(half)]

    def store_tokens(r, groups):
        for hf in range(half):
            for tb in range(nblk):
                rows = tok_rows(r, rt_in, hf * nblk + tb)
                y = pltpu.bitcast(groups[hf][tb], BF16).astype(F32)
                o_ref[rows, :] = y + d_row * u_ref[rows, :]

    _pipelined_block_transpose(rc // rt_in, load_y, store_tokens,
                               jnp.zeros((rt_in // 2, LANES), jnp.uint32), lane_blk_in)


def _ssm(u_tok, bb, cc, pw, a16, d, n_batch):
    rows, width = u_tok.shape
    rc = rows // CHUNK
    n_chunks = rc // n_batch
    assert GROUPS_PER_STEP == BLOCKS_PER_VREG and rc % 16 == 0
    gs = GROUPS_PER_STEP
    bw = gs * CHUNK_W
    sw = gs * 2 * SSM_STATE
    kern = functools.partial(_ssm_kernel, n_chunks=n_chunks, n_batch=n_batch)
    grp4 = lambda j: (j, 0, 0, 0)
    return pl.pallas_call(
        kern,
        grid=(SSM_GROUPS // gs,),
        in_specs=[
            pl.BlockSpec((rows, LANES), lambda j: (0, j)),
            pl.BlockSpec((gs,) + bb.shape[1:], grp4),
            pl.BlockSpec((gs,) + cc.shape[1:], grp4),
            pl.BlockSpec((gs,) + pw.shape[1:], grp4),
            pl.BlockSpec((2, sw), lambda j: (0, j)),
            pl.BlockSpec((1, LANES), lambda j: (0, j)),
        ],
        out_specs=pl.BlockSpec((rows, LANES), lambda j: (0, j)),
        out_shape=jax.ShapeDtypeStruct((rows, width), F32),
        scratch_shapes=[
            pltpu.VMEM((rc, bw), BF16),
            pltpu.VMEM((rc, bw), BF16),
            pltpu.VMEM((gs, CHUNK_W, CHUNK_W), BF16),
            pltpu.VMEM((gs, CHUNK_W, 2 * SSM_STATE), BF16),
            pltpu.VMEM((gs, CHUNK_W, 2 * SSM_STATE), BF16),
            pltpu.VMEM((gs, CHUNK_W, 2 * SSM_STATE), BF16),
            pltpu.VMEM((rc, sw), F32),
            pltpu.VMEM((rc, sw), F32),
            pltpu.VMEM((rc, sw), F32),
        ],
        compiler_params=pltpu.CompilerParams(
            dimension_semantics=("parallel",), vmem_limit_bytes=VMEM_LIMIT),
        name="ssm",
    )(u_tok, bb, cc, pw, a16, d)


def _ssm_tables(lam_re, lam_im, log_step, b_re, b_im, c_re, c_im):
    depth = lam_re.shape[0]
    g, p, h, t = SSM_GROUPS, SSM_STATE, SSM_GROUP_CH, CHUNK
    lam = lax.complex(lam_re.astype(F32), lam_im.astype(F32))
    delta = jnp.exp(log_step.astype(F32))[..., None]
    ld = lam * delta
    lam_bar = jnp.exp(ld)
    b_bar = ((lam_bar - 1.0) / lam)[..., None] * lax.complex(b_re.astype(F32), b_im.astype(F32))
    c_c = lax.complex(c_re.astype(F32), c_im.astype(F32))
    tau = jnp.arange(t + 1, dtype=F32)
    powers = jnp.exp(ld[:, :, None, :] * tau[None, None, :, None])

    cat = lambda x, y: jnp.concatenate([x, y], axis=-1)
    br, bi = jnp.real(b_bar).swapaxes(-1, -2), jnp.imag(b_bar).swapaxes(-1, -2)
    bb = jnp.stack([cat(br, bi), cat(-bi, br), cat(bi, br), cat(br, -bi)], axis=2)
    cr, ci = jnp.real(c_c), jnp.imag(c_c)
    cc = jnp.stack([cat(cr, -ci), cat(-ci, -cr)], axis=2)
    pr, pi = jnp.real(powers), jnp.imag(powers)
    pw = jnp.stack([cat(pr, pr), cat(pi, pi)], axis=2)
    pw = jnp.pad(pw, ((0, 0), (0, 0), (0, 0), (0, 24 - (t + 1)), (0, 0)))
    ar, ai = pr[:, :, t], pi[:, :, t]
    a16 = jnp.stack([cat(ar, ar).reshape(depth, g * 2 * p),
                     cat(-ai, ai).reshape(depth, g * 2 * p)], axis=1)
    return bb, cc, pw, a16


def _outproj_kernel(a_ref, g_ref, x_ref, wglu_f32, bglu_ref, sg_ref, wout_f32, ng_ref,
                    o_ref, xn_ref, wglu_ref, wout_ref):
    @pl.when(pl.program_id(0) == 0)
    def _():
        wglu_ref[...] = wglu_f32[...].astype(BF16)
        wout_ref[...] = wout_f32[...].astype(BF16)

    g = _gelu_tanh(g_ref[...])
    z = jnp.dot(g.astype(BF16), wglu_ref[...], preferred_element_type=F32) + bglu_ref[...]
    s = g * jax.nn.sigmoid(z)
    s_n = _rms(s, sg_ref[...]).astype(BF16)
    y = jnp.dot(a_ref[...], wout_ref[:ATTN_WIDTH, :], preferred_element_type=F32)
    y = y + jnp.dot(s_n, wout_ref[ATTN_WIDTH:, :], preferred_element_type=F32)
    h = x_ref[...] + y
    o_ref[...] = h
    xn_ref[...] = _rms(h, ng_ref[...]).astype(xn_ref.dtype)


def _outproj(attn_n, g_tok, x, w_glu_all, b_glu, ssm_gain, w_out_all, mlp_gain, layer):
    rows = x.shape[0]
    tm = _row_tile(rows, 384)
    row = lambda i: (i, 0)
    fixed = lambda i: (0, 0)
    return pl.pallas_call(
        _outproj_kernel,
        grid=(rows // tm,),
        in_specs=[
            pl.BlockSpec((tm, ATTN_WIDTH), row),
            pl.BlockSpec((tm, SSM_WIDTH), row),
            pl.BlockSpec((tm, D_MODEL), row),
            pl.BlockSpec((None, SSM_WIDTH, SSM_WIDTH), lambda i: (layer, 0, 0)),
            pl.BlockSpec((1, SSM_WIDTH), fixed),
            pl.BlockSpec((1, SSM_WIDTH), fixed),
            pl.BlockSpec((None, D_MODEL, D_MODEL), lambda i: (layer, 0, 0)),
            pl.BlockSpec((1, D_MODEL), fixed),
        ],
        out_specs=[pl.BlockSpec((tm, D_MODEL), row), pl.BlockSpec((tm, D_MODEL), row)],
        out_shape=[jax.ShapeDtypeStruct((rows, D_MODEL), F32),
                   jax.ShapeDtypeStruct((rows, D_MODEL), BF16)],
        scratch_shapes=[pltpu.VMEM((SSM_WIDTH, SSM_WIDTH), BF16), pltpu.VMEM((D_MODEL, D_MODEL), BF16)],
        compiler_params=pltpu.CompilerParams(
            dimension_semantics=("arbitrary",), vmem_limit_bytes=BIG_VMEM_LIMIT),
        name="outproj",
    )(attn_n, g_tok, x, w_glu_all, b_glu, ssm_gain, w_out_all, mlp_gain)


def _mlp_tile(xn_ref, x_ref, w_up, w_down, o_ref):
    a = jnp.dot(xn_ref[...], w_up, preferred_element_type=F32)
    a = jnp.square(jnp.maximum(a, 0.0)).astype(BF16)
    o_ref[...] += jnp.dot(a, w_down, preferred_element_type=F32)


def _mlp_first_kernel(xn_ref, x_ref, wu_ref, wd_ref, o_ref, wu_bf_ref, wd_bf_ref):
    @pl.when(pl.program_id(0) == 0)
    def _():
        o_ref[...] = x_ref[...]

    wu_bf_ref[...] = wu_ref[...].astype(BF16)
    wd_bf_ref[...] = wd_ref[...].astype(BF16)
    _mlp_tile(xn_ref, x_ref, wu_bf_ref[...], wd_bf_ref[...], o_ref)


def _mlp_rest_kernel(xn_ref, x_ref, wu_ref, wd_ref, first_hbm, o_ref, sem):
    i = pl.program_id(0)
    f = pl.program_id(1)

    @pl.when((i == 0) & (f == 0))
    def _():
        copy = pltpu.make_async_copy(first_hbm, o_ref, sem)
        copy.start()
        copy.wait()

    @pl.when(i > 0)
    def _():
        @pl.when(f == 0)
        def _():
            o_ref[...] = x_ref[...]

        _mlp_tile(xn_ref, x_ref, wu_ref[...], wd_ref[...], o_ref)


def _mlp(xn, x, w_up_all, w_down_all, layer):
    rows = x.shape[0]
    tm = _row_tile(rows, MLP_ROWS)
    tf_first, tf = MLP_FF_TILE_FIRST, MLP_FF_TILE
    first_rows = lambda f: (0, 0)
    first, w_up_bf, w_down_bf = pl.pallas_call(
        _mlp_first_kernel,
        grid=(D_FF // tf_first,),
        in_specs=[
            pl.BlockSpec((tm, D_MODEL), first_rows),
            pl.BlockSpec((tm, D_MODEL), first_rows),
            pl.BlockSpec((None, D_MODEL, tf_first), lambda f: (layer, 0, f)),
            pl.BlockSpec((None, tf_first, D_MODEL), lambda f: (layer, f, 0)),
        ],
        out_specs=[pl.BlockSpec((tm, D_MODEL), first_rows),
                   pl.BlockSpec((D_MODEL, tf_first), lambda f: (0, f)),
                   pl.BlockSpec((tf_first, D_MODEL), lambda f: (f, 0))],
        out_shape=[jax.ShapeDtypeStruct((tm, D_MODEL), F32),
                   jax.ShapeDtypeStruct((D_MODEL, D_FF), BF16),
                   jax.ShapeDtypeStruct((D_FF, D_MODEL), BF16)],
        compiler_params=pltpu.CompilerParams(
            dimension_semantics=("arbitrary",), vmem_limit_bytes=BIG_VMEM_LIMIT),
        name="mlp_first",
    )(xn, x, w_up_all, w_down_all)
    if rows == tm:
        return first
    ff = lambda i, f: jnp.where(i == 0, 0, f)
    return pl.pallas_call(
        _mlp_rest_kernel,
        grid=(rows // tm, D_FF // tf),
        in_specs=[
            pl.BlockSpec((tm, D_MODEL), lambda i, f: (jnp.maximum(i, 1), 0)),
            pl.BlockSpec((tm, D_MODEL), lambda i, f: (jnp.maximum(i, 1), 0)),
            pl.BlockSpec((D_MODEL, tf), lambda i, f: (0, ff(i, f))),
            pl.BlockSpec((tf, D_MODEL), lambda i, f: (ff(i, f), 0)),
            pl.BlockSpec(memory_space=pl.ANY),
        ],
        out_specs=pl.BlockSpec((tm, D_MODEL), lambda i, f: (i, 0)),
        out_shape=jax.ShapeDtypeStruct((rows, D_MODEL), F32),
        scratch_shapes=[pltpu.SemaphoreType.DMA(())],
        compiler_params=pltpu.CompilerParams(
            dimension_semantics=("arbitrary", "arbitrary"), vmem_limit_bytes=BIG_VMEM_LIMIT),
        name="mlp_rest",
    )(xn, x, w_up_bf, w_down_bf, first)


def kernel(x, meta_tokens, norm_mix_g, w_in, q_norm_g, k_norm_g, attn_sinks, ssm_lambda_re, ssm_lambda_im, ssm_log_step, ssm_b_re, ssm_b_im, ssm_c_re, ssm_c_im, ssm_d, w_glu, b_glu, attn_out_g, ssm_out_g, w_out, norm_mlp_g, w_up, w_down):
    b, seq, d = x.shape
    depth = w_in.shape[0]
    assert d == D_MODEL
    length = N_META + seq
    lp = -(-length // TILE) * TILE
    rows = b * lp

    meta = jnp.broadcast_to(meta_tokens.astype(x.dtype)[None], (b, N_META, d))
    pad = jnp.zeros((b, lp - length, d), x.dtype)
    h_res = jnp.concatenate([meta, x, pad], axis=1).reshape(rows, d)

    bb, cc, pw, a16 = _ssm_tables(
        ssm_lambda_re, ssm_lambda_im, ssm_log_step, ssm_b_re, ssm_b_im, ssm_c_re, ssm_c_im)

    head_id = jnp.arange(KV_WIDTH) // HEAD_DIM
    head_ones = (head_id[:, None] == head_id[None, :]).astype(BF16)
    slopes = jnp.exp2(-8.0 * jnp.arange(1, N_HEADS + 1, dtype=F32) / N_HEADS)
    attn_bias, attn_off = _attn_bias(slopes)

    row_vec = lambda v: v.astype(F32).reshape(1, -1)
    for l in range(depth):
        q_gain = jnp.tile(row_vec(q_norm_g[l]), (1, N_KV_HEADS)) * (LOG2E / math.sqrt(HEAD_DIM))
        k_gain = jnp.tile(row_vec(k_norm_g[l]), (1, N_KV_HEADS))
        sink_row = jnp.broadcast_to(
            (attn_sinks[l].astype(F32) * LOG2E).reshape(N_KV_HEADS, 1, KV_GROUP, 1),
            (N_KV_HEADS, 1, KV_GROUP, SUB)).reshape(N_KV_HEADS, 1, KV_GROUP * SUB)
        u_tok, attn_n = _inproj_attention(
            h_res, row_vec(norm_mix_g[l]), w_in, l, head_ones, q_gain, k_gain, attn_bias, attn_off,
            sink_row, row_vec(attn_out_g[l]), b)
        g_tok = _ssm(u_tok, bb[l], cc[l], pw[l], a16[l], row_vec(ssm_d[l]), b)
        h_res, xn = _outproj(attn_n, g_tok, h_res, w_glu,
                             row_vec(b_glu[l]), row_vec(ssm_out_g[l]), w_out,
                             row_vec(norm_mlp_g[l]), l)
        h_res = _mlp(xn, h_res, w_up, w_down, l)
    return h_res.reshape(b, lp, d)[:, N_META:length]
```

```python
import functools
import math

import jax
import jax.numpy as jnp
from jax import lax
from jax.experimental import pallas as pl
from jax.experimental.pallas import tpu as pltpu

D_MODEL = 2048
N_META = 16
HEAD_DIM = 64
ATTN_WIDTH = D_MODEL // 2
N_HEADS = ATTN_WIDTH // HEAD_DIM
N_KV_HEADS = N_HEADS // 4
KV_GROUP = N_HEADS // N_KV_HEADS
KV_WIDTH = N_KV_HEADS * HEAD_DIM
SSM_WIDTH = D_MODEL - ATTN_WIDTH
SSM_GROUP_CH = 16
SSM_GROUPS = SSM_WIDTH // SSM_GROUP_CH
SSM_STATE = 64
WINDOW = 128
BLOCK = 128
D_FF = 4 * D_MODEL
IN_WIDTH = ATTN_WIDTH + 2 * KV_WIDTH + SSM_WIDTH
NORM_EPS = 1e-6
NEG_INF = -1e30
LOG2E = math.log2(math.e)

CHUNK = 16
CHUNK_W = CHUNK * SSM_GROUP_CH
GROUPS_PER_STEP = 8
RELAYOUT_UNROLL = True

QKV_WIDTH = ATTN_WIDTH + 2 * KV_WIDTH
COL_K = ATTN_WIDTH
COL_V = ATTN_WIDTH + KV_WIDTH
LANES = 128
BLOCKS_PER_VREG = LANES // SSM_GROUP_CH

VMEM_LIMIT = 48 * 1024 * 1024
MLP_ROWS = 768
MLP_FF_TILE = 1024
MLP_FF_TILE_FIRST = 512
BIG_VMEM_LIMIT = 56 * 1024 * 1024

F32 = jnp.float32
BF16 = jnp.bfloat16


def _row_tile(rows, target):
    k = rows // BLOCK
    best = 1
    for d in range(1, k + 1):
        if k % d == 0 and d * BLOCK <= target:
            best = d
    return best * BLOCK


def _rms(x, gain):
    ms = jnp.mean(x * x, axis=-1, keepdims=True)
    return x * lax.rsqrt(ms + NORM_EPS) * gain


def _head_norm(x, head_ones, gain):
    x2 = x * x
    hi = x2.astype(BF16)
    lo = (x2 - hi.astype(F32)).astype(BF16)
    ssq = (jnp.dot(hi, head_ones, preferred_element_type=F32)
           + jnp.dot(lo, head_ones, preferred_element_type=F32))
    return x * lax.rsqrt(ssq * (1.0 / HEAD_DIM) + NORM_EPS) * gain


SUB = 64
BAND = WINDOW + SUB
KEYS = BAND + N_META
TILE = 3 * BLOCK


def _attn_bias(slopes):
    j = jnp.arange(KEYS)[:, None]
    i = jnp.arange(SUB)[None, :]
    is_band = j < BAND
    dist = WINDOW + i - j
    sl = slopes.reshape(N_KV_HEADS, 1, KV_GROUP, 1)
    dist_f = dist.astype(F32)[None, :, None, :]
    band = jnp.where(((dist >= 0) & (dist < WINDOW))[None, :, None, :], -sl * dist_f, NEG_INF)
    meta = -sl * (i - (j - BAND)).astype(F32)[None, :, None, :]
    bias = jnp.where(is_band[None, :, None, :], band, meta) * LOG2E
    lanes = KV_GROUP * SUB
    off_coef = jnp.broadcast_to(-sl * LOG2E, (N_KV_HEADS, N_META, KV_GROUP, SUB))
    return bias.reshape(N_KV_HEADS, KEYS, lanes), off_coef.reshape(N_KV_HEADS, N_META, lanes)


def _project_tile(x_ref, g_ref, w_bf, ones_ref, qg_ref, kg_ref, qkv_out, u_ref):
    h = _rms(x_ref[...], g_ref[...]).astype(BF16)
    proj = jnp.dot(h, w_bf[...], preferred_element_type=F32)
    head_ones = ones_ref[...]
    for blk in range(ATTN_WIDTH // KV_WIDTH):
        cols = slice(blk * KV_WIDTH, (blk + 1) * KV_WIDTH)
        qkv_out[:, cols] = _head_norm(proj[:, cols], head_ones, qg_ref[...]).astype(BF16)
    qkv_out[:, COL_K:COL_V] = _head_norm(proj[:, COL_K:COL_V], head_ones, kg_ref[...]).astype(BF16)
    qkv_out[:, COL_V:QKV_WIDTH] = proj[:, COL_V:QKV_WIDTH].astype(BF16)
    u_ref[...] = proj[:, QKV_WIDTH:]


def _attend_tile(qkv, tail, meta, bias_ref, offc_ref, sink_ref, og_ref, o_ref, acc, tile_pos, masked):
    lanes = KV_GROUP * SUB
    ones = jnp.ones((KEYS, 8), BF16)
    for blk in range(TILE // BLOCK):
        chains = [(sb, kh) for sb in range(BLOCK // SUB) for kh in range(N_KV_HEADS)]
        scores, values = [], []
        for sb, kh in chains:
            row0 = blk * BLOCK + sb * SUB
            start = tile_pos + row0
            first = row0 - WINDOW

            def window(col):
                cols = slice(col + kh * HEAD_DIM, col + (kh + 1) * HEAD_DIM)
                tcols = slice(cols.start - COL_K, cols.stop - COL_K)
                if first >= 0:
                    band = qkv[first:first + BAND, cols]
                else:
                    band = jnp.concatenate([tail[WINDOW + first:WINDOW, tcols],
                                            qkv[0:BAND + first, cols]], axis=0)
                return jnp.concatenate([band, meta[:, tcols]], axis=0)

            k_win = window(COL_K)
            values.append(jnp.concatenate([window(COL_V), ones], axis=1))
            q_s = jnp.concatenate(
                [qkv[row0:row0 + SUB, (kh * KV_GROUP + g) * HEAD_DIM:(kh * KV_GROUP + g + 1) * HEAD_DIM]
                 for g in range(KV_GROUP)], axis=0)
            s = lax.dot_general(k_win, q_s, (((1,), (1,)), ((), ())),
                                preferred_element_type=F32)
            s = s + bias_ref[kh]
            s = jnp.concatenate([s[:BAND], s[BAND:] + offc_ref[kh] * start.astype(F32)], axis=0)
            if masked:
                j = lax.broadcasted_iota(jnp.int32, (KEYS, lanes), 0)
                i = lax.broadcasted_iota(jnp.int32, (KEYS, lanes), 1) % SUB
                ok = jnp.where(j < BAND, start - WINDOW + j - N_META, start + i - (j - BAND)) >= 0
                s = jnp.where(ok, s, NEG_INF)
            scores.append(s)
        maxes = [jnp.maximum(jnp.max(s, axis=0, keepdims=True), sink_ref[kh])
                 for s, (sb, kh) in zip(scores, chains)]
        probs = [jnp.exp2(s - m).astype(BF16) for s, m in zip(scores, maxes)]
        for (sb, kh), p, m, v_ext in zip(chains, probs, maxes, values):
            o_ext = lax.dot_general(v_ext, p, (((0,), (0,)), ((), ())),
                                    preferred_element_type=F32)
            denom = o_ext[HEAD_DIM:HEAD_DIM + 1] + jnp.exp2(sink_ref[kh] - m)
            o = o_ext[:HEAD_DIM] / denom
            for g in range(KV_GROUP):
                h = kh * KV_GROUP + g
                acc[h * HEAD_DIM:(h + 1) * HEAD_DIM, sb * SUB:(sb + 1) * SUB] = o[:, g * SUB:(g + 1) * SUB]
        a_t = acc[...]
        ms = jnp.mean(a_t * a_t, axis=0, keepdims=True)
        a_n = a_t * lax.rsqrt(ms + NORM_EPS)
        o_ref[blk * BLOCK:(blk + 1) * BLOCK, :] = (a_n.T * og_ref[...]).astype(o_ref.dtype)


def _inproj_attn_kernel(x_ref, g_ref, w_ref, ones_ref, qg_ref, kg_ref, bias_ref, offc_ref,
                        sink_ref, og_ref, u_ref, attn_ref, w_bf, ring, tail, meta, acc,
                        *, tiles_per_seq, n_tiles):
    s = pl.program_id(0)

    @pl.when(s == 0)
    def _():
        w_bf[...] = w_ref[...].astype(BF16)
        ring[...] = jnp.zeros_like(ring)
        tail[...] = jnp.zeros_like(tail)
        meta[...] = jnp.zeros_like(meta)

    t_attn = jnp.maximum(s - 1, 0)
    seq_tile = t_attn % tiles_per_seq
    seq_slot = (t_attn // tiles_per_seq) % 2
    cur = (s + 1) % 2
    nxt = s % 2

    def step(masked):
        _attend_tile(ring.at[cur], tail, meta.at[seq_slot], bias_ref, offc_ref, sink_ref, og_ref,
                     attn_ref, acc, seq_tile * TILE, masked)
        tail[...] = ring[cur, TILE - WINDOW:TILE, COL_K:QKV_WIDTH]
        _project_tile(x_ref, g_ref, w_bf, ones_ref, qg_ref, kg_ref, ring.at[nxt], u_ref)

    @pl.when(seq_tile != 0)
    def _():
        step(False)

    @pl.when(seq_tile == 0)
    def _():
        step(True)

    t_proj = jnp.minimum(s, n_tiles - 1)

    @pl.when(t_proj % tiles_per_seq == 0)
    def _():
        meta[(t_proj // tiles_per_seq) % 2] = ring[nxt, 0:N_META, COL_K:QKV_WIDTH]


def _inproj_attention(x, gain, w_all, layer, head_ones, q_gain, k_gain, bias, off_coef, sink_row,
                      out_gain, n_seq):
    rows = x.shape[0]
    n_tiles = rows // TILE
    tiles_per_seq = n_tiles // n_seq
    assert rows == n_seq * tiles_per_seq * TILE
    kern = functools.partial(_inproj_attn_kernel, tiles_per_seq=tiles_per_seq, n_tiles=n_tiles)
    fixed = lambda s: (0, 0)
    fixed3 = lambda s: (0, 0, 0)
    proj_tile = lambda s: (jnp.minimum(s, n_tiles - 1), 0)
    return pl.pallas_call(
        kern,
        grid=(n_tiles + 1,),
        in_specs=[
            pl.BlockSpec((TILE, D_MODEL), proj_tile),
            pl.BlockSpec((1, D_MODEL), fixed),
            pl.BlockSpec((None, D_MODEL, IN_WIDTH), lambda s: (layer, 0, 0)),
            pl.BlockSpec((KV_WIDTH, KV_WIDTH), fixed),
            pl.BlockSpec((1, KV_WIDTH), fixed),
            pl.BlockSpec((1, KV_WIDTH), fixed),
            pl.BlockSpec(bias.shape, fixed3),
            pl.BlockSpec(off_coef.shape, fixed3),
            pl.BlockSpec(sink_row.shape, fixed3),
            pl.BlockSpec((1, ATTN_WIDTH), fixed),
        ],
        out_specs=[pl.BlockSpec((TILE, SSM_WIDTH), proj_tile),
                   pl.BlockSpec((TILE, ATTN_WIDTH), lambda s: (jnp.maximum(s - 1, 0), 0))],
        out_shape=[jax.ShapeDtypeStruct((rows, SSM_WIDTH), F32),
                   jax.ShapeDtypeStruct((rows, ATTN_WIDTH), BF16)],
        scratch_shapes=[
            pltpu.VMEM((D_MODEL, IN_WIDTH), BF16),
            pltpu.VMEM((2, TILE, QKV_WIDTH), BF16),
            pltpu.VMEM((WINDOW, 2 * KV_WIDTH), BF16),
            pltpu.VMEM((2, N_META, 2 * KV_WIDTH), BF16),
            pltpu.VMEM((ATTN_WIDTH, BLOCK), F32),
        ],
        compiler_params=pltpu.CompilerParams(
            dimension_semantics=("arbitrary",), vmem_limit_bytes=BIG_VMEM_LIMIT),
        name="inproj_attn",
    )(x, gain, w_all, head_ones, q_gain, k_gain, bias, off_coef, sink_row, out_gain)


def _gelu_tanh(x):
    c = math.sqrt(2.0 / math.pi)
    return 0.5 * x * (1.0 + jnp.tanh(c * (x + 0.044715 * (x * x * x))))


def _block_transpose_stage(arrs, dist, lane_blk):
    nblk = BLOCKS_PER_VREG
    keep = (lane_blk & dist) == 0
    nxt = list(arrs)
    for i in range(nblk):
        if i & dist == 0:
            lo, hi = arrs[i], arrs[i + dist]
            nxt[i] = jnp.where(keep, lo, pltpu.roll(hi, dist * SSM_GROUP_CH, axis=1))
            nxt[i + dist] = jnp.where(keep, pltpu.roll(lo, (nblk - dist) * SSM_GROUP_CH, axis=1), hi)
    return nxt


def _pipelined_block_transpose(n_tiles, load, store, zero, lane_blk):
    def trip(r, carry):
        after1, after2 = carry
        fresh = load(jnp.minimum(r, n_tiles - 1))
        new1 = [_block_transpose_stage(g, 4, lane_blk) for g in fresh]
        new2 = [_block_transpose_stage(g, 2, lane_blk) for g in after1]
        done = [_block_transpose_stage(g, 1, lane_blk) for g in after2]
        store(jnp.maximum(r - 2, 0), done)
        return new1, new2

    groups = CHUNK // BLOCKS_PER_VREG
    init = [[zero] * BLOCKS_PER_VREG for _ in range(groups)]
    lax.fori_loop(0, n_tiles + 2, trip, (init, init), unroll=RELAYOUT_UNROLL)


def _ssm_kernel(u_ref, bb_ref, cc_ref, pw_ref, a_ref, d_ref, o_ref,
                lhs, yscr, m_scr, win_scr, wsw_scr, wct_scr, zin, zsw, st,
                *, n_chunks, n_batch):
    nblk = BLOCKS_PER_VREG
    sw = 2 * SSM_STATE
    rc = n_batch * n_chunks
    half = CHUNK // nblk

    def tok_rows(tile, tile_chunks, t):
        return pl.ds(tile * (tile_chunks * CHUNK) + t, tile_chunks, stride=CHUNK)

    lane256 = lax.broadcasted_iota(jnp.int32, (CHUNK, CHUNK_W), 1)
    for g in range(GROUPS_PER_STEP):
        bb, bbs, bsw, bsws = bb_ref[g, 0], bb_ref[g, 1], bb_ref[g, 2], bb_ref[g, 3]
        cc, ccs = cc_ref[g, 0], cc_ref[g, 1]
        c_pow = [pw_ref[g, 0, tau:tau + 1, :] * cc + pw_ref[g, 1, tau:tau + 1, :] * ccs
                 for tau in range(CHUNK + 1)]
        for t in range(CHUNK):
            rows = slice(t * CHUNK, (t + 1) * CHUNK)
            p_re = pw_ref[g, 0, CHUNK - 1 - t:CHUNK - t, :]
            p_im = pw_ref[g, 1, CHUNK - 1 - t:CHUNK - t, :]
            win_scr[g, rows, :] = (p_re * bb + p_im * bbs).astype(BF16)
            wsw_scr[g, rows, :] = (p_re * bsw + p_im * bsws).astype(BF16)
            wct_scr[g, rows, :] = c_pow[t + 1].astype(BF16)
        base = lax.dot_general(bb, jnp.concatenate(c_pow[:CHUNK], axis=0), (((1,), (1,)), ((), ())),
                               precision=lax.Precision.HIGHEST, preferred_element_type=F32)
        for sblk in range(CHUNK):
            row = base if sblk == 0 else jnp.where(
                lane256 >= sblk * SSM_GROUP_CH, pltpu.roll(base, sblk * SSM_GROUP_CH, axis=1), 0.0)
            m_scr[g, sblk * CHUNK:(sblk + 1) * CHUNK, :] = row.astype(BF16)

    rt_in = 16
    lane_blk_in = lax.broadcasted_iota(jnp.int32, (rt_in // 2, LANES), 1) // SSM_GROUP_CH

    def load_u(r):
        return [[pltpu.bitcast(u_ref[tok_rows(r, rt_in, hf * nblk + tb), :].astype(BF16), jnp.uint32)
                 for tb in range(nblk)] for hf in range(half)]

    def store_lhs(r, groups):
        rows = pl.ds(pl.multiple_of(r * rt_in, rt_in), rt_in)
        for hf in range(half):
            for g in range(nblk):
                col = g * CHUNK_W + hf * LANES
                lhs[rows, col:col + LANES] = pltpu.bitcast(groups[hf][g], BF16)

    _pipelined_block_transpose(rc // rt_in, load_u, store_lhs,
                               jnp.zeros((rt_in // 2, LANES), jnp.uint32), lane_blk_in)

    for g in range(GROUPS_PER_STEP):
        u_g = lhs[:, g * CHUNK_W:(g + 1) * CHUNK_W]
        zin[:, g * sw:(g + 1) * sw] = jnp.dot(u_g, win_scr[g], preferred_element_type=F32)
        zsw[:, g * sw:(g + 1) * sw] = jnp.dot(u_g, wsw_scr[g], preferred_element_type=F32)

    a1 = a_ref[0:1, :]
    a2 = a_ref[1:2, :]

    def step(c, carry):
        nxt = []
        for b in range(n_batch):
            s, x = carry[b]
            row = b * n_chunks + c
            st[pl.ds(row, 1), :] = s
            nxt.append((a1 * s + a2 * x + zin[pl.ds(row, 1), :],
                        a1 * x - a2 * s + zsw[pl.ds(row, 1), :]))
        return tuple(nxt)

    zero = jnp.zeros((1, GROUPS_PER_STEP * sw), F32)
    lax.fori_loop(0, n_chunks, step, tuple((zero, zero) for _ in range(n_batch)))

    for g in range(GROUPS_PER_STEP):
        cols = slice(g * CHUNK_W, (g + 1) * CHUNK_W)
        y = jnp.dot(lhs[:, cols], m_scr[g], preferred_element_type=F32)
        y = y + lax.dot_general(st[:, g * sw:(g + 1) * sw].astype(BF16), wct_scr[g],
                                (((1,), (1,)), ((), ())), preferred_element_type=F32)
        yscr[:, cols] = y.astype(BF16)

    d_row = d_ref[...]

    def load_y(r):
        rows = pl.ds(pl.multiple_of(r * rt_in, rt_in), rt_in)
        return [[pltpu.bitcast(yscr[rows, g * CHUNK_W + hf * LANES:g * CHUNK_W + (hf + 1) * LANES],
                               jnp.uint32)
                 for g in range(nblk)] for hf in range(half)]

    def store_tokens(r, groups):
        for hf in range(half):
            for tb in range(nblk):
                rows = tok_rows(r, rt_in, hf * nblk + tb)
                y = pltpu.bitcast(groups[hf][tb], BF16).astype(F32)
                o_ref[rows, :] = y + d_row * u_ref[rows, :]

    _pipelined_block_transpose(rc // rt_in, load_y, store_tokens,
                               jnp.zeros((rt_in // 2, LANES), jnp.uint32), lane_blk_in)


def _ssm(u_tok, bb, cc, pw, a16, d, n_batch):
    rows, width = u_tok.shape
    rc = rows // CHUNK
    n_chunks = rc // n_batch
    assert GROUPS_PER_STEP == BLOCKS_PER_VREG and rc % 16 == 0
    gs = GROUPS_PER_STEP
    bw = gs * CHUNK_W
    sw = gs * 2 * SSM_STATE
    kern = functools.partial(_ssm_kernel, n_chunks=n_chunks, n_batch=n_batch)
    grp4 = lambda j: (j, 0, 0, 0)
    return pl.pallas_call(
        kern,
        grid=(SSM_GROUPS // gs,),
        in_specs=[
            pl.BlockSpec((rows, LANES), lambda j: (0, j)),
            pl.BlockSpec((gs,) + bb.shape[1:], grp4),
            pl.BlockSpec((gs,) + cc.shape[1:], grp4),
            pl.BlockSpec((gs,) + pw.shape[1:], grp4),
            pl.BlockSpec((2, sw), lambda j: (0, j)),
            pl.BlockSpec((1, LANES), lambda j: (0, j)),
        ],
        out_specs=pl.BlockSpec((rows, LANES), lambda j: (0, j)),
        out_shape=jax.ShapeDtypeStruct((rows, width), F32),
        scratch_shapes=[
            pltpu.VMEM((rc, bw), BF16),
            pltpu.VMEM((rc, bw), BF16),
            pltpu.VMEM((gs, CHUNK_W, CHUNK_W), BF16),
            pltpu.VMEM((gs, CHUNK_W, 2 * SSM_STATE), BF16),
            pltpu.VMEM((gs, CHUNK_W, 2 * SSM_STATE), BF16),
            pltpu.VMEM((gs, CHUNK_W, 2 * SSM_STATE), BF16),
            pltpu.VMEM((rc, sw), F32),
            pltpu.VMEM((rc, sw), F32),
            pltpu.VMEM((rc, sw), F32),
        ],
        compiler_params=pltpu.CompilerParams(
            dimension_semantics=("parallel",), vmem_limit_bytes=VMEM_LIMIT),
        name="ssm",
    )(u_tok, bb, cc, pw, a16, d)


def _ssm_tables(lam_re, lam_im, log_step, b_re, b_im, c_re, c_im):
    depth = lam_re.shape[0]
    g, p, h, t = SSM_GROUPS, SSM_STATE, SSM_GROUP_CH, CHUNK
    lam = lax.complex(lam_re.astype(F32), lam_im.astype(F32))
    delta = jnp.exp(log_step.astype(F32))[..., None]
    ld = lam * delta
    lam_bar = jnp.exp(ld)
    b_bar = ((lam_bar - 1.0) / lam)[..., None] * lax.complex(b_re.astype(F32), b_im.astype(F32))
    c_c = lax.complex(c_re.astype(F32), c_im.astype(F32))
    pows = [jnp.ones_like(lam_bar), lam_bar]
    for _ in range(t - 1):
        pows.append(pows[-1] * lam_bar)
    powers = jnp.stack(pows, axis=2)

    cat = lambda x, y: jnp.concatenate([x, y], axis=-1)
    br, bi = jnp.real(b_bar).swapaxes(-1, -2), jnp.imag(b_bar).swapaxes(-1, -2)
    bb = jnp.stack([cat(br, bi), cat(-bi, br), cat(bi, br), cat(br, -bi)], axis=2)
    cr, ci = jnp.real(c_c), jnp.imag(c_c)
    cc = jnp.stack([cat(cr, -ci), cat(-ci, -cr)], axis=2)
    pr, pi = jnp.real(powers), jnp.imag(powers)
    pw = jnp.stack([cat(pr, pr), cat(pi, pi)], axis=2)
    pw = jnp.pad(pw, ((0, 0), (0, 0), (0, 0), (0, 24 - (t + 1)), (0, 0)))
    ar, ai = pr[:, :, t], pi[:, :, t]
    a16 = jnp.stack([cat(ar, ar).reshape(depth, g * 2 * p),
                     cat(-ai, ai).reshape(depth, g * 2 * p)], axis=1)
    return bb, cc, pw, a16


def _outproj_kernel(a_ref, g_ref, x_ref, wglu_f32, bglu_ref, sg_ref, wout_f32, ng_ref,
                    o_ref, xn_ref, wglu_ref, wout_ref):
    @pl.when(pl.program_id(0) == 0)
    def _():
        wglu_ref[...] = wglu_f32[...].astype(BF16)
        wout_ref[...] = wout_f32[...].astype(BF16)

    g = _gelu_tanh(g_ref[...])
    z = jnp.dot(g.astype(BF16), wglu_ref[...], preferred_element_type=F32) + bglu_ref[...]
    s = g * jax.nn.sigmoid(z)
    s_n = _rms(s, sg_ref[...]).astype(BF16)
    y = jnp.dot(a_ref[...], wout_ref[:ATTN_WIDTH, :], preferred_element_type=F32)
    y = y + jnp.dot(s_n, wout_ref[ATTN_WIDTH:, :], preferred_element_type=F32)
    h = x_ref[...] + y
    o_ref[...] = h
    xn_ref[...] = _rms(h, ng_ref[...]).astype(xn_ref.dtype)


def _outproj(attn_n, g_tok, x, w_glu_all, b_glu, ssm_gain, w_out_all, mlp_gain, layer):
    rows = x.shape[0]
    tm = _row_tile(rows, 384)
    row = lambda i: (i, 0)
    fixed = lambda i: (0, 0)
    return pl.pallas_call(
        _outproj_kernel,
        grid=(rows // tm,),
        in_specs=[
            pl.BlockSpec((tm, ATTN_WIDTH), row),
            pl.BlockSpec((tm, SSM_WIDTH), row),
            pl.BlockSpec((tm, D_MODEL), row),
            pl.BlockSpec((None, SSM_WIDTH, SSM_WIDTH), lambda i: (layer, 0, 0)),
            pl.BlockSpec((1, SSM_WIDTH), fixed),
            pl.BlockSpec((1, SSM_WIDTH), fixed),
            pl.BlockSpec((None, D_MODEL, D_MODEL), lambda i: (layer, 0, 0)),
            pl.BlockSpec((1, D_MODEL), fixed),
        ],
        out_specs=[pl.BlockSpec((tm, D_MODEL), row), pl.BlockSpec((tm, D_MODEL), row)],
        out_shape=[jax.ShapeDtypeStruct((rows, D_MODEL), F32),
                   jax.ShapeDtypeStruct((rows, D_MODEL), BF16)],
        scratch_shapes=[pltpu.VMEM((SSM_WIDTH, SSM_WIDTH), BF16), pltpu.VMEM((D_MODEL, D_MODEL), BF16)],
        compiler_params=pltpu.CompilerParams(
            dimension_semantics=("arbitrary",), vmem_limit_bytes=BIG_VMEM_LIMIT),
        name="outproj",
    )(attn_n, g_tok, x, w_glu_all, b_glu, ssm_gain, w_out_all, mlp_gain)


def _mlp_tile(xn_ref, x_ref, w_up, w_down, o_ref):
    a = jnp.dot(xn_ref[...], w_up, preferred_element_type=F32)
    a = jnp.square(jnp.maximum(a, 0.0)).astype(BF16)
    o_ref[...] += jnp.dot(a, w_down, preferred_element_type=F32)


def _mlp_first_kernel(xn_ref, x_ref, wu_ref, wd_ref, o_ref, wu_bf_ref, wd_bf_ref):
    @pl.when(pl.program_id(0) == 0)
    def _():
        o_ref[...] = x_ref[...]

    wu_bf_ref[...] = wu_ref[...].astype(BF16)
    wd_bf_ref[...] = wd_ref[...].astype(BF16)
    _mlp_tile(xn_ref, x_ref, wu_bf_ref[...], wd_bf_ref[...], o_ref)


def _mlp_rest_kernel(xn_ref, x_ref, wu_ref, wd_ref, first_hbm, o_ref, sem):
    i = pl.program_id(0)
    f = pl.program_id(1)

    @pl.when((i == 0) & (f == 0))
    def _():
        copy = pltpu.make_async_copy(first_hbm, o_ref, sem)
        copy.start()
        copy.wait()

    @pl.when(i > 0)
    def _():
        @pl.when(f == 0)
        def _():
            o_ref[...] = x_ref[...]

        _mlp_tile(xn_ref, x_ref, wu_ref[...], wd_ref[...], o_ref)


def _mlp(xn, x, w_up_all, w_down_all, layer):
    rows = x.shape[0]
    tm = _row_tile(rows, MLP_ROWS)
    tf_first, tf = MLP_FF_TILE_FIRST, MLP_FF_TILE
    first_rows = lambda f: (0, 0)
    first, w_up_bf, w_down_bf = pl.pallas_call(
        _mlp_first_kernel,
        grid=(D_FF // tf_first,),
        in_specs=[
            pl.BlockSpec((tm, D_MODEL), first_rows),
            pl.BlockSpec((tm, D_MODEL), first_rows),
            pl.BlockSpec((None, D_MODEL, tf_first), lambda f: (layer, 0, f)),
            pl.BlockSpec((None, tf_first, D_MODEL), lambda f: (layer, f, 0)),
        ],
        out_specs=[pl.BlockSpec((tm, D_MODEL), first_rows),
                   pl.BlockSpec((D_MODEL, tf_first), lambda f: (0, f)),
                   pl.BlockSpec((tf_first, D_MODEL), lambda f: (f, 0))],
        out_shape=[jax.ShapeDtypeStruct((tm, D_MODEL), F32),
                   jax.ShapeDtypeStruct((D_MODEL, D_FF), BF16),
                   jax.ShapeDtypeStruct((D_FF, D_MODEL), BF16)],
        compiler_params=pltpu.CompilerParams(
            dimension_semantics=("arbitrary",), vmem_limit_bytes=BIG_VMEM_LIMIT),
        name="mlp_first",
    )(xn, x, w_up_all, w_down_all)
    if rows == tm:
        return first
    ff = lambda i, f: jnp.where(i == 0, 0, f)
    return pl.pallas_call(
        _mlp_rest_kernel,
        grid=(rows // tm, D_FF // tf),
        in_specs=[
            pl.BlockSpec((tm, D_MODEL), lambda i, f: (jnp.maximum(i, 1), 0)),
            pl.BlockSpec((tm, D_MODEL), lambda i, f: (jnp.maximum(i, 1), 0)),
            pl.BlockSpec((D_MODEL, tf), lambda i, f: (0, ff(i, f))),
            pl.BlockSpec((tf, D_MODEL), lambda i, f: (ff(i, f), 0)),
            pl.BlockSpec(memory_space=pl.ANY),
        ],
        out_specs=pl.BlockSpec((tm, D_MODEL), lambda i, f: (i, 0)),
        out_shape=jax.ShapeDtypeStruct((rows, D_MODEL), F32),
        scratch_shapes=[pltpu.SemaphoreType.DMA(())],
        compiler_params=pltpu.CompilerParams(
            dimension_semantics=("arbitrary", "arbitrary"), vmem_limit_bytes=BIG_VMEM_LIMIT),
        name="mlp_rest",
    )(xn, x, w_up_bf, w_down_bf, first)


def kernel(x, meta_tokens, norm_mix_g, w_in, q_norm_g, k_norm_g, attn_sinks, ssm_lambda_re, ssm_lambda_im, ssm_log_step, ssm_b_re, ssm_b_im, ssm_c_re, ssm_c_im, ssm_d, w_glu, b_glu, attn_out_g, ssm_out_g, w_out, norm_mlp_g, w_up, w_down):
    b, seq, d = x.shape
    depth = w_in.shape[0]
    assert d == D_MODEL
    length = N_META + seq
    lp = -(-length // TILE) * TILE
    rows = b * lp

    meta = jnp.broadcast_to(meta_tokens.astype(x.dtype)[None], (b, N_META, d))
    pad = jnp.zeros((b, lp - length, d), x.dtype)
    h_res = jnp.concatenate([meta, x, pad], axis=1).reshape(rows, d)

    bb, cc, pw, a16 = _ssm_tables(
        ssm_lambda_re, ssm_lambda_im, ssm_log_step, ssm_b_re, ssm_b_im, ssm_c_re, ssm_c_im)

    head_id = jnp.arange(KV_WIDTH) // HEAD_DIM
    head_ones = (head_id[:, None] == head_id[None, :]).astype(BF16)
    slopes = jnp.exp2(-8.0 * jnp.arange(1, N_HEADS + 1, dtype=F32) / N_HEADS)
    attn_bias, attn_off = _attn_bias(slopes)

    row_vec = lambda v: v.astype(F32).reshape(1, -1)
    for l in range(depth):
        q_gain = jnp.tile(row_vec(q_norm_g[l]), (1, N_KV_HEADS)) * (LOG2E / math.sqrt(HEAD_DIM))
        k_gain = jnp.tile(row_vec(k_norm_g[l]), (1, N_KV_HEADS))
        sink_row = jnp.broadcast_to(
            (attn_sinks[l].astype(F32) * LOG2E).reshape(N_KV_HEADS, 1, KV_GROUP, 1),
            (N_KV_HEADS, 1, KV_GROUP, SUB)).reshape(N_KV_HEADS, 1, KV_GROUP * SUB)
        u_tok, attn_n = _inproj_attention(
            h_res, row_vec(norm_mix_g[l]), w_in, l, head_ones, q_gain, k_gain, attn_bias, attn_off,
            sink_row, row_vec(attn_out_g[l]), b)
        g_tok = _ssm(u_tok, bb[l], cc[l], pw[l], a16[l], row_vec(ssm_d[l]), b)
        h_res, xn = _outproj(attn_n, g_tok, h_res, w_glu,
                             row_vec(b_glu[l]), row_vec(ssm_out_g[l]), w_out,
                             row_vec(norm_mlp_g[l]), l)
        h_res = _mlp(xn, h_res, w_up, w_down, l)
    return h_res.reshape(b, lp, d)[:, N_META:length]
```

```python
import functools
import math

import jax
import jax.numpy as jnp
from jax import lax
from jax.experimental import pallas as pl
from jax.experimental.pallas import tpu as pltpu

D_MODEL = 2048
N_META = 16
HEAD_DIM = 64
ATTN_WIDTH = D_MODEL // 2
N_HEADS = ATTN_WIDTH // HEAD_DIM
N_KV_HEADS = N_HEADS // 4
KV_GROUP = N_HEADS // N_KV_HEADS
KV_WIDTH = N_KV_HEADS * HEAD_DIM
SSM_WIDTH = D_MODEL - ATTN_WIDTH
SSM_GROUP_CH = 16
SSM_GROUPS = SSM_WIDTH // SSM_GROUP_CH
SSM_STATE = 64
WINDOW = 128
BLOCK = 128
D_FF = 4 * D_MODEL
IN_WIDTH = ATTN_WIDTH + 2 * KV_WIDTH + SSM_WIDTH
NORM_EPS = 1e-6
NEG_INF = -1e30
LOG2E = math.log2(math.e)

CHUNK = 16
CHUNK_W = CHUNK * SSM_GROUP_CH
GROUPS_PER_STEP = 8
SCAN_UNROLL = 8
RELAYOUT_UNROLL = True

QKV_WIDTH = ATTN_WIDTH + 2 * KV_WIDTH
COL_K = ATTN_WIDTH
COL_V = ATTN_WIDTH + KV_WIDTH
LANES = 128
BLOCKS_PER_VREG = LANES // SSM_GROUP_CH

VMEM_LIMIT = 48 * 1024 * 1024
MLP_ROWS = 768
MLP_FF_TILE = 1024
MLP_TILES_FIRST = 2
MLP_FF_TILE_FIRST = 512
BIG_VMEM_LIMIT = 56 * 1024 * 1024

F32 = jnp.float32
BF16 = jnp.bfloat16


def _row_tile(rows, target):
    k = rows // BLOCK
    best = 1
    for d in range(1, k + 1):
        if k % d == 0 and d * BLOCK <= target:
            best = d
    return best * BLOCK


def _rms(x, gain):
    ms = jnp.mean(x * x, axis=-1, keepdims=True)
    return x * lax.rsqrt(ms + NORM_EPS) * gain


def _head_norm(x, head_ones, gain):
    x2 = x * x
    hi = x2.astype(BF16)
    lo = (x2 - hi.astype(F32)).astype(BF16)
    ssq = (jnp.dot(hi, head_ones, preferred_element_type=F32)
           + jnp.dot(lo, head_ones, preferred_element_type=F32))
    return x * lax.rsqrt(ssq * (1.0 / HEAD_DIM) + NORM_EPS) * gain


SUB = 64
BAND = WINDOW + SUB
KEYS = BAND + N_META
TILE = 3 * BLOCK


def _attn_bias(slopes):
    j = jnp.arange(KEYS)[:, None]
    i = jnp.arange(SUB)[None, :]
    is_band = j < BAND
    dist = WINDOW + i - j
    sl = slopes.reshape(N_KV_HEADS, 1, KV_GROUP, 1)
    dist_f = dist.astype(F32)[None, :, None, :]
    band = jnp.where(((dist >= 0) & (dist < WINDOW))[None, :, None, :], -sl * dist_f, NEG_INF)
    meta = -sl * (i - (j - BAND)).astype(F32)[None, :, None, :]
    bias = jnp.where(is_band[None, :, None, :], band, meta) * LOG2E
    lanes = KV_GROUP * SUB
    off_coef = jnp.broadcast_to(-sl * LOG2E, (N_KV_HEADS, N_META, KV_GROUP, SUB))
    return bias.reshape(N_KV_HEADS, KEYS, lanes), off_coef.reshape(N_KV_HEADS, N_META, lanes)


def _project_tile(x_ref, g_ref, w_bf, ones_ref, qg_ref, kg_ref, qkv_out, u_ref):
    h = _rms(x_ref[...], g_ref[...]).astype(BF16)
    proj = jnp.dot(h, w_bf[...], preferred_element_type=F32)
    head_ones = ones_ref[...]
    for blk in range(ATTN_WIDTH // KV_WIDTH):
        cols = slice(blk * KV_WIDTH, (blk + 1) * KV_WIDTH)
        qkv_out[:, cols] = _head_norm(proj[:, cols], head_ones, qg_ref[...]).astype(BF16)
    qkv_out[:, COL_K:COL_V] = _head_norm(proj[:, COL_K:COL_V], head_ones, kg_ref[...]).astype(BF16)
    qkv_out[:, COL_V:QKV_WIDTH] = proj[:, COL_V:QKV_WIDTH].astype(BF16)
    u_ref[...] = proj[:, QKV_WIDTH:]


def _attend_tile(qkv, tail, meta, bias_ref, offc_ref, sink_ref, og_ref, o_ref, acc, tile_pos, masked):
    lanes = KV_GROUP * SUB
    ones = jnp.ones((KEYS, 8), BF16)
    for blk in range(TILE // BLOCK):
        chains = [(sb, kh) for sb in range(BLOCK // SUB) for kh in range(N_KV_HEADS)]
        scores, values = [], []
        for sb, kh in chains:
            row0 = blk * BLOCK + sb * SUB
            start = tile_pos + row0
            first = row0 - WINDOW

            def window(col):
                cols = slice(col + kh * HEAD_DIM, col + (kh + 1) * HEAD_DIM)
                tcols = slice(cols.start - COL_K, cols.stop - COL_K)
                if first >= 0:
                    band = qkv[first:first + BAND, cols]
                else:
                    band = jnp.concatenate([tail[WINDOW + first:WINDOW, tcols],
                                            qkv[0:BAND + first, cols]], axis=0)
                return jnp.concatenate([band, meta[:, tcols]], axis=0)

            k_win = window(COL_K)
            values.append(jnp.concatenate([window(COL_V), ones], axis=1))
            q_s = jnp.concatenate(
                [qkv[row0:row0 + SUB, (kh * KV_GROUP + g) * HEAD_DIM:(kh * KV_GROUP + g + 1) * HEAD_DIM]
                 for g in range(KV_GROUP)], axis=0)
            s = lax.dot_general(k_win, q_s, (((1,), (1,)), ((), ())),
                                preferred_element_type=F32)
            s = s + bias_ref[kh]
            s = jnp.concatenate([s[:BAND], s[BAND:] + offc_ref[kh] * start.astype(F32)], axis=0)
            if masked:
                j = lax.broadcasted_iota(jnp.int32, (KEYS, lanes), 0)
                i = lax.broadcasted_iota(jnp.int32, (KEYS, lanes), 1) % SUB
                ok = jnp.where(j < BAND, start - WINDOW + j - N_META, start + i - (j - BAND)) >= 0
                s = jnp.where(ok, s, NEG_INF)
            scores.append(s)
        maxes = [jnp.maximum(jnp.max(s, axis=0, keepdims=True), sink_ref[kh])
                 for s, (sb, kh) in zip(scores, chains)]
        probs = [jnp.exp2(s - m).astype(BF16) for s, m in zip(scores, maxes)]
        for (sb, kh), p, m, v_ext in zip(chains, probs, maxes, values):
            o_ext = lax.dot_general(v_ext, p, (((0,), (0,)), ((), ())),
                                    preferred_element_type=F32)
            denom = o_ext[HEAD_DIM:HEAD_DIM + 1] + jnp.exp2(sink_ref[kh] - m)
            o = o_ext[:HEAD_DIM] / denom
            for g in range(KV_GROUP):
                h = kh * KV_GROUP + g
                acc[h * HEAD_DIM:(h + 1) * HEAD_DIM, sb * SUB:(sb + 1) * SUB] = o[:, g * SUB:(g + 1) * SUB]
        a_t = acc[...]
        ms = jnp.mean(a_t * a_t, axis=0, keepdims=True)
        a_n = a_t * lax.rsqrt(ms + NORM_EPS)
        o_ref[blk * BLOCK:(blk + 1) * BLOCK, :] = (a_n.T * og_ref[...]).astype(o_ref.dtype)


def _inproj_attn_kernel(x_ref, g_ref, w_ref, ones_ref, qg_ref, kg_ref, bias_ref, offc_ref,
                        sink_ref, og_ref, u_ref, attn_ref, w_bf, ring, tail, meta, acc,
                        *, tiles_per_seq, n_tiles):
    s = pl.program_id(0)

    @pl.when(s == 0)
    def _():
        w_bf[...] = w_ref[...].astype(BF16)
        ring[...] = jnp.zeros_like(ring)
        tail[...] = jnp.zeros_like(tail)
        meta[...] = jnp.zeros_like(meta)

    t_attn = jnp.maximum(s - 1, 0)
    seq_tile = t_attn % tiles_per_seq
    seq_slot = (t_attn // tiles_per_seq) % 2
    cur = (s + 1) % 2
    nxt = s % 2

    def step(masked):
        _attend_tile(ring.at[cur], tail, meta.at[seq_slot], bias_ref, offc_ref, sink_ref, og_ref,
                     attn_ref, acc, seq_tile * TILE, masked)
        tail[...] = ring[cur, TILE - WINDOW:TILE, COL_K:QKV_WIDTH]
        _project_tile(x_ref, g_ref, w_bf, ones_ref, qg_ref, kg_ref, ring.at[nxt], u_ref)

    @pl.when(seq_tile != 0)
    def _():
        step(False)

    @pl.when(seq_tile == 0)
    def _():
        step(True)

    t_proj = jnp.minimum(s, n_tiles - 1)

    @pl.when(t_proj % tiles_per_seq == 0)
    def _():
        meta[(t_proj // tiles_per_seq) % 2] = ring[nxt, 0:N_META, COL_K:QKV_WIDTH]


def _inproj_attention(x, gain, w_all, layer, head_ones, q_gain, k_gain, bias, off_coef, sink_row,
                      out_gain, n_seq):
    rows = x.shape[0]
    n_tiles = rows // TILE
    tiles_per_seq = n_tiles // n_seq
    assert rows == n_seq * tiles_per_seq * TILE
    kern = functools.partial(_inproj_attn_kernel, tiles_per_seq=tiles_per_seq, n_tiles=n_tiles)
    fixed = lambda s: (0, 0)
    fixed3 = lambda s: (0, 0, 0)
    proj_tile = lambda s: (jnp.minimum(s, n_tiles - 1), 0)
    return pl.pallas_call(
        kern,
        grid=(n_tiles + 1,),
        in_specs=[
            pl.BlockSpec((TILE, D_MODEL), proj_tile),
            pl.BlockSpec((1, D_MODEL), fixed),
            pl.BlockSpec((None, D_MODEL, IN_WIDTH), lambda s: (layer, 0, 0)),
            pl.BlockSpec((KV_WIDTH, KV_WIDTH), fixed),
            pl.BlockSpec((1, KV_WIDTH), fixed),
            pl.BlockSpec((1, KV_WIDTH), fixed),
            pl.BlockSpec(bias.shape, fixed3),
            pl.BlockSpec(off_coef.shape, fixed3),
            pl.BlockSpec(sink_row.shape, fixed3),
            pl.BlockSpec((1, ATTN_WIDTH), fixed),
        ],
        out_specs=[pl.BlockSpec((TILE, SSM_WIDTH), proj_tile),
                   pl.BlockSpec((TILE, ATTN_WIDTH), lambda s: (jnp.maximum(s - 1, 0), 0))],
        out_shape=[jax.ShapeDtypeStruct((rows, SSM_WIDTH), F32),
                   jax.ShapeDtypeStruct((rows, ATTN_WIDTH), BF16)],
        scratch_shapes=[
            pltpu.VMEM((D_MODEL, IN_WIDTH), BF16),
            pltpu.VMEM((2, TILE, QKV_WIDTH), BF16),
            pltpu.VMEM((WINDOW, 2 * KV_WIDTH), BF16),
            pltpu.VMEM((2, N_META, 2 * KV_WIDTH), BF16),
            pltpu.VMEM((ATTN_WIDTH, BLOCK), F32),
        ],
        compiler_params=pltpu.CompilerParams(
            dimension_semantics=("arbitrary",), vmem_limit_bytes=BIG_VMEM_LIMIT),
        name="inproj_attn",
    )(x, gain, w_all, head_ones, q_gain, k_gain, bias, off_coef, sink_row, out_gain)


def _gelu_tanh(x):
    c = math.sqrt(2.0 / math.pi)
    return 0.5 * x * (1.0 + jnp.tanh(c * (x + 0.044715 * (x * x * x))))


def _block_transpose_stage(arrs, dist, lane_blk):
    nblk = BLOCKS_PER_VREG
    keep = (lane_blk & dist) == 0
    nxt = list(arrs)
    for i in range(nblk):
        if i & dist == 0:
            lo, hi = arrs[i], arrs[i + dist]
            nxt[i] = jnp.where(keep, lo, pltpu.roll(hi, dist * SSM_GROUP_CH, axis=1))
            nxt[i + dist] = jnp.where(keep, pltpu.roll(lo, (nblk - dist) * SSM_GROUP_CH, axis=1), hi)
    return nxt


def _pipelined_block_transpose(n_tiles, load, store, zero, lane_blk):
    def trip(r, carry):
        after1, after2 = carry
        fresh = load(jnp.minimum(r, n_tiles - 1))
        new1 = [_block_transpose_stage(g, 4, lane_blk) for g in fresh]
        new2 = [_block_transpose_stage(g, 2, lane_blk) for g in after1]
        done = [_block_transpose_stage(g, 1, lane_blk) for g in after2]
        store(jnp.maximum(r - 2, 0), done)
        return new1, new2

    groups = CHUNK // BLOCKS_PER_VREG
    init = [[zero] * BLOCKS_PER_VREG for _ in range(groups)]
    lax.fori_loop(0, n_tiles + 2, trip, (init, init), unroll=RELAYOUT_UNROLL)


def _ssm_kernel(u_ref, bb_ref, cc_ref, pw_ref, a_ref, d_ref, o_ref,
                lhs, yscr, m_scr, win_scr, wsw_scr, wct_scr, zin, zsw, st,
                *, n_chunks, n_batch):
    nblk = BLOCKS_PER_VREG
    sw = 2 * SSM_STATE
    rc = n_batch * n_chunks
    half = CHUNK // nblk

    def tok_rows(tile, tile_chunks, t):
        return pl.ds(tile * (tile_chunks * CHUNK) + t, tile_chunks, stride=CHUNK)

    lane256 = lax.broadcasted_iota(jnp.int32, (CHUNK, CHUNK_W), 1)
    for g in range(GROUPS_PER_STEP):
        bb, bbs, bsw, bsws = bb_ref[g, 0], bb_ref[g, 1], bb_ref[g, 2], bb_ref[g, 3]
        cc, ccs = cc_ref[g, 0], cc_ref[g, 1]
        c_pow = [pw_ref[g, 0, tau:tau + 1, :] * cc + pw_ref[g, 1, tau:tau + 1, :] * ccs
                 for tau in range(CHUNK + 1)]
        for t in range(CHUNK):
            rows = slice(t * CHUNK, (t + 1) * CHUNK)
            p_re = pw_ref[g, 0, CHUNK - 1 - t:CHUNK - t, :]
            p_im = pw_ref[g, 1, CHUNK - 1 - t:CHUNK - t, :]
            win_scr[g, rows, :] = (p_re * bb + p_im * bbs).astype(BF16)
            wsw_scr[g, rows, :] = (p_re * bsw + p_im * bsws).astype(BF16)
            wct_scr[g, rows, :] = c_pow[t + 1].astype(BF16)
        base = lax.dot_general(bb, jnp.concatenate(c_pow[:CHUNK], axis=0), (((1,), (1,)), ((), ())),
                               precision=lax.Precision.HIGHEST, preferred_element_type=F32)
        for sblk in range(CHUNK):
            row = base if sblk == 0 else jnp.where(
                lane256 >= sblk * SSM_GROUP_CH, pltpu.roll(base, sblk * SSM_GROUP_CH, axis=1), 0.0)
            m_scr[g, sblk * CHUNK:(sblk + 1) * CHUNK, :] = row.astype(BF16)

    rt_in = 16
    lane_blk_in = lax.broadcasted_iota(jnp.int32, (rt_in // 2, LANES), 1) // SSM_GROUP_CH

    def load_u(r):
        return [[pltpu.bitcast(u_ref[tok_rows(r, rt_in, hf * nblk + tb), :].astype(BF16), jnp.uint32)
                 for tb in range(nblk)] for hf in range(half)]

    def store_lhs(r, groups):
        rows = pl.ds(pl.multiple_of(r * rt_in, rt_in), rt_in)
        for hf in range(half):
            for g in range(nblk):
                col = g * CHUNK_W + hf * LANES
                lhs[rows, col:col + LANES] = pltpu.bitcast(groups[hf][g], BF16)

    _pipelined_block_transpose(rc // rt_in, load_u, store_lhs,
                               jnp.zeros((rt_in // 2, LANES), jnp.uint32), lane_blk_in)

    for g in range(GROUPS_PER_STEP):
        u_g = lhs[:, g * CHUNK_W:(g + 1) * CHUNK_W]
        zin[:, g * sw:(g + 1) * sw] = jnp.dot(u_g, win_scr[g], preferred_element_type=F32)
        zsw[:, g * sw:(g + 1) * sw] = jnp.dot(u_g, wsw_scr[g], preferred_element_type=F32)

    a1 = a_ref[0:1, :]
    a2 = a_ref[1:2, :]

    def step(c, carry):
        nxt = []
        for b in range(n_batch):
            s, x = carry[b]
            row = b * n_chunks + c
            st[pl.ds(row, 1), :] = s
            nxt.append((a1 * s + a2 * x + zin[pl.ds(row, 1), :],
                        a1 * x - a2 * s + zsw[pl.ds(row, 1), :]))
        return tuple(nxt)

    zero = jnp.zeros((1, GROUPS_PER_STEP * sw), F32)
    lax.fori_loop(0, n_chunks, step, tuple((zero, zero) for _ in range(n_batch)),
                  unroll=SCAN_UNROLL)

    for g in range(GROUPS_PER_STEP):
        cols = slice(g * CHUNK_W, (g + 1) * CHUNK_W)
        y = jnp.dot(lhs[:, cols], m_scr[g], preferred_element_type=F32)
        y = y + lax.dot_general(st[:, g * sw:(g + 1) * sw].astype(BF16), wct_scr[g],
                                (((1,), (1,)), ((), ())), preferred_element_type=F32)
        yscr[:, cols] = y.astype(BF16)

    d_row = d_ref[...]

    def load_y(r):
        rows = pl.ds(pl.multiple_of(r * rt_in, rt_in), rt_in)
        return [[pltpu.bitcast(yscr[rows, g * CHUNK_W + hf * LANES:g * CHUNK_W + (hf + 1) * LANES],
                               jnp.uint32)
                 for g in range(nblk)] for hf in range(half)]

    def store_tokens(r, groups):
        for hf in range(half):
            for tb in range(nblk):
                rows = tok_rows(r, rt_in, hf * nblk + tb)
                y = pltpu.bitcast(groups[hf][tb], BF16).astype(F32)
                o_ref[rows, :] = y + d_row * u_ref[rows, :]

    _pipelined_block_transpose(rc // rt_in, load_y, store_tokens,
                               jnp.zeros((rt_in // 2, LANES), jnp.uint32), lane_blk_in)


def _ssm(u_tok, bb, cc, pw, a16, d, n_batch):
    rows, width = u_tok.shape
    rc = rows // CHUNK
    n_chunks = rc // n_batch
    assert GROUPS_PER_STEP == BLOCKS_PER_VREG and rc % 16 == 0
    gs = GROUPS_PER_STEP
    bw = gs * CHUNK_W
    sw = gs * 2 * SSM_STATE
    kern = functools.partial(_ssm_kernel, n_chunks=n_chunks, n_batch=n_batch)
    grp4 = lambda j: (j, 0, 0, 0)
    return pl.pallas_call(
        kern,
        grid=(SSM_GROUPS // gs,),
        in_specs=[
            pl.BlockSpec((rows, LANES), lambda j: (0, j)),
            pl.BlockSpec((gs,) + bb.shape[1:], grp4),
            pl.BlockSpec((gs,) + cc.shape[1:], grp4),
            pl.BlockSpec((gs,) + pw.shape[1:], grp4),
            pl.BlockSpec((2, sw), lambda j: (0, j)),
            pl.BlockSpec((1, LANES), lambda j: (0, j)),
        ],
        out_specs=pl.BlockSpec((rows, LANES), lambda j: (0, j)),
        out_shape=jax.ShapeDtypeStruct((rows, width), F32),
        scratch_shapes=[
            pltpu.VMEM((rc, bw), BF16),
            pltpu.VMEM((rc, bw), BF16),
            pltpu.VMEM((gs, CHUNK_W, CHUNK_W), BF16),
            pltpu.VMEM((gs, CHUNK_W, 2 * SSM_STATE), BF16),
            pltpu.VMEM((gs, CHUNK_W, 2 * SSM_STATE), BF16),
            pltpu.VMEM((gs, CHUNK_W, 2 * SSM_STATE), BF16),
            pltpu.VMEM((rc, sw), F32),
            pltpu.VMEM((rc, sw), F32),
            pltpu.VMEM((rc, sw), F32),
        ],
        compiler_params=pltpu.CompilerParams(
            dimension_semantics=("parallel",), vmem_limit_bytes=VMEM_LIMIT),
        name="ssm",
    )(u_tok, bb, cc, pw, a16, d)


def _ssm_tables(lam_re, lam_im, log_step, b_re, b_im, c_re, c_im):
    depth = lam_re.shape[0]
    g, p, h, t = SSM_GROUPS, SSM_STATE, SSM_GROUP_CH, CHUNK
    lam = lax.complex(lam_re.astype(F32), lam_im.astype(F32))
    delta = jnp.exp(log_step.astype(F32))[..., None]
    ld = lam * delta
    lam_bar = jnp.exp(ld)
    b_bar = ((lam_bar - 1.0) / lam)[..., None] * lax.complex(b_re.astype(F32), b_im.astype(F32))
    c_c = lax.complex(c_re.astype(F32), c_im.astype(F32))
    pows = [jnp.ones_like(lam_bar), lam_bar]
    for _ in range(t - 1):
        pows.append(pows[-1] * lam_bar)
    powers = jnp.stack(pows, axis=2)

    cat = lambda x, y: jnp.concatenate([x, y], axis=-1)
    br, bi = jnp.real(b_bar).swapaxes(-1, -2), jnp.imag(b_bar).swapaxes(-1, -2)
    bb = jnp.stack([cat(br, bi), cat(-bi, br), cat(bi, br), cat(br, -bi)], axis=2)
    cr, ci = jnp.real(c_c), jnp.imag(c_c)
    cc = jnp.stack([cat(cr, -ci), cat(-ci, -cr)], axis=2)
    pr, pi = jnp.real(powers), jnp.imag(powers)
    pw = jnp.stack([cat(pr, pr), cat(pi, pi)], axis=2)
    pw = jnp.pad(pw, ((0, 0), (0, 0), (0, 0), (0, 24 - (t + 1)), (0, 0)))
    ar, ai = pr[:, :, t], pi[:, :, t]
    a16 = jnp.stack([cat(ar, ar).reshape(depth, g * 2 * p),
                     cat(-ai, ai).reshape(depth, g * 2 * p)], axis=1)
    return bb, cc, pw, a16


def _outproj_kernel(a_ref, g_ref, x_ref, wglu_f32, bglu_ref, sg_ref, wout_f32, ng_ref,
                    o_ref, xn_ref, wglu_ref, wout_ref):
    @pl.when(pl.program_id(0) == 0)
    def _():
        wglu_ref[...] = wglu_f32[...].astype(BF16)
        wout_ref[...] = wout_f32[...].astype(BF16)

    g = _gelu_tanh(g_ref[...])
    z = jnp.dot(g.astype(BF16), wglu_ref[...], preferred_element_type=F32) + bglu_ref[...]
    s = g * jax.nn.sigmoid(z)
    s_n = _rms(s, sg_ref[...]).astype(BF16)
    y = jnp.dot(a_ref[...], wout_ref[:ATTN_WIDTH, :], preferred_element_type=F32)
    y = y + jnp.dot(s_n, wout_ref[ATTN_WIDTH:, :], preferred_element_type=F32)
    h = x_ref[...] + y
    o_ref[...] = h
    xn_ref[...] = _rms(h, ng_ref[...]).astype(xn_ref.dtype)


def _outproj(attn_n, g_tok, x, w_glu_all, b_glu, ssm_gain, w_out_all, mlp_gain, layer):
    rows = x.shape[0]
    tm = _row_tile(rows, 384)
    row = lambda i: (i, 0)
    fixed = lambda i: (0, 0)
    return pl.pallas_call(
        _outproj_kernel,
        grid=(rows // tm,),
        in_specs=[
            pl.BlockSpec((tm, ATTN_WIDTH), row),
            pl.BlockSpec((tm, SSM_WIDTH), row),
            pl.BlockSpec((tm, D_MODEL), row),
            pl.BlockSpec((None, SSM_WIDTH, SSM_WIDTH), lambda i: (layer, 0, 0)),
            pl.BlockSpec((1, SSM_WIDTH), fixed),
            pl.BlockSpec((1, SSM_WIDTH), fixed),
            pl.BlockSpec((None, D_MODEL, D_MODEL), lambda i: (layer, 0, 0)),
            pl.BlockSpec((1, D_MODEL), fixed),
        ],
        out_specs=[pl.BlockSpec((tm, D_MODEL), row), pl.BlockSpec((tm, D_MODEL), row)],
        out_shape=[jax.ShapeDtypeStruct((rows, D_MODEL), F32),
                   jax.ShapeDtypeStruct((rows, D_MODEL), BF16)],
        scratch_shapes=[pltpu.VMEM((SSM_WIDTH, SSM_WIDTH), BF16), pltpu.VMEM((D_MODEL, D_MODEL), BF16)],
        compiler_params=pltpu.CompilerParams(
            dimension_semantics=("arbitrary",), vmem_limit_bytes=BIG_VMEM_LIMIT),
        name="outproj",
    )(attn_n, g_tok, x, w_glu_all, b_glu, ssm_gain, w_out_all, mlp_gain)


def _mlp_tile(xn, w_up, w_down, o_ref, rows):
    a = jnp.dot(xn, w_up, preferred_element_type=F32)
    a = jnp.square(jnp.maximum(a, 0.0)).astype(BF16)
    o_ref[rows, :] += jnp.dot(a, w_down, preferred_element_type=F32)


def _mlp_first_kernel(xn_ref, x_hbm, wu_ref, wd_ref, o_ref, wu_bf_ref, wd_bf_ref, sem, *, tm):
    @pl.when(pl.program_id(0) == 0)
    def _():
        copy = pltpu.make_async_copy(x_hbm.at[pl.ds(0, o_ref.shape[0])], o_ref, sem)
        copy.start()
        copy.wait()

    wu_bf_ref[...] = wu_ref[...].astype(BF16)
    wd_bf_ref[...] = wd_ref[...].astype(BF16)
    for t in range(o_ref.shape[0] // tm):
        rows = slice(t * tm, (t + 1) * tm)
        _mlp_tile(xn_ref[rows, :], wu_bf_ref[...], wd_bf_ref[...], o_ref, rows)


def _mlp_rest_kernel(xn_ref, x_ref, wu_ref, wd_ref, first_hbm, o_ref, sem, *, skip):
    i = pl.program_id(0)
    f = pl.program_id(1)
    tm = o_ref.shape[0]

    @pl.when((i < skip) & (f == 0))
    def _():
        copy = pltpu.make_async_copy(first_hbm.at[pl.ds(pl.multiple_of(i * tm, tm), tm)], o_ref, sem)
        copy.start()
        copy.wait()

    @pl.when(i >= skip)
    def _():
        @pl.when(f == 0)
        def _():
            o_ref[...] = x_ref[...]

        _mlp_tile(xn_ref[...], wu_ref[...], wd_ref[...], o_ref, slice(None))


def _mlp(xn, x, w_up_all, w_down_all, layer):
    rows = x.shape[0]
    tm = _row_tile(rows, MLP_ROWS)
    skip = min(MLP_TILES_FIRST, rows // tm)
    tm_first = skip * tm
    tf_first, tf = MLP_FF_TILE_FIRST, MLP_FF_TILE
    first_rows = lambda f: (0, 0)
    first, w_up_bf, w_down_bf = pl.pallas_call(
        functools.partial(_mlp_first_kernel, tm=tm),
        grid=(D_FF // tf_first,),
        in_specs=[
            pl.BlockSpec((tm_first, D_MODEL), first_rows),
            pl.BlockSpec(memory_space=pl.ANY),
            pl.BlockSpec((None, D_MODEL, tf_first), lambda f: (layer, 0, f)),
            pl.BlockSpec((None, tf_first, D_MODEL), lambda f: (layer, f, 0)),
        ],
        out_specs=[pl.BlockSpec((tm_first, D_MODEL), first_rows),
                   pl.BlockSpec((D_MODEL, tf_first), lambda f: (0, f)),
                   pl.BlockSpec((tf_first, D_MODEL), lambda f: (f, 0))],
        out_shape=[jax.ShapeDtypeStruct((tm_first, D_MODEL), F32),
                   jax.ShapeDtypeStruct((D_MODEL, D_FF), BF16),
                   jax.ShapeDtypeStruct((D_FF, D_MODEL), BF16)],
        scratch_shapes=[pltpu.SemaphoreType.DMA(())],
        compiler_params=pltpu.CompilerParams(
            dimension_semantics=("arbitrary",), vmem_limit_bytes=BIG_VMEM_LIMIT),
        name="mlp_first",
    )(xn, x, w_up_all, w_down_all)
    if rows == tm_first:
        return first
    ff = lambda i, f: jnp.where(i < skip, 0, f)
    return pl.pallas_call(
        functools.partial(_mlp_rest_kernel, skip=skip),
        grid=(rows // tm, D_FF // tf),
        in_specs=[
            pl.BlockSpec((tm, D_MODEL), lambda i, f: (jnp.maximum(i, skip), 0)),
            pl.BlockSpec((tm, D_MODEL), lambda i, f: (jnp.maximum(i, skip), 0)),
            pl.BlockSpec((D_MODEL, tf), lambda i, f: (0, ff(i, f))),
            pl.BlockSpec((tf, D_MODEL), lambda i, f: (ff(i, f), 0)),
            pl.BlockSpec(memory_space=pl.ANY),
        ],
        out_specs=pl.BlockSpec((tm, D_MODEL), lambda i, f: (i, 0)),
        out_shape=jax.ShapeDtypeStruct((rows, D_MODEL), F32),
        scratch_shapes=[pltpu.SemaphoreType.DMA(())],
        compiler_params=pltpu.CompilerParams(
            dimension_semantics=("arbitrary", "arbitrary"), vmem_limit_bytes=BIG_VMEM_LIMIT),
        name="mlp_rest",
    )(xn, x, w_up_bf, w_down_bf, first)


def kernel(x, meta_tokens, norm_mix_g, w_in, q_norm_g, k_norm_g, attn_sinks, ssm_lambda_re, ssm_lambda_im, ssm_log_step, ssm_b_re, ssm_b_im, ssm_c_re, ssm_c_im, ssm_d, w_glu, b_glu, attn_out_g, ssm_out_g, w_out, norm_mlp_g, w_up, w_down):
    b, seq, d = x.shape
    depth = w_in.shape[0]
    assert d == D_MODEL
    length = N_META + seq
    lp = -(-length // TILE) * TILE
    rows = b * lp

    meta = jnp.broadcast_to(meta_tokens.astype(x.dtype)[None], (b, N_META, d))
    pad = jnp.zeros((b, lp - length, d), x.dtype)
    h_res = jnp.concatenate([meta, x, pad], axis=1).reshape(rows, d)

    bb, cc, pw, a16 = _ssm_tables(
        ssm_lambda_re, ssm_lambda_im, ssm_log_step, ssm_b_re, ssm_b_im, ssm_c_re, ssm_c_im)

    head_id = jnp.arange(KV_WIDTH) // HEAD_DIM
    head_ones = (head_id[:, None] == head_id[None, :]).astype(BF16)
    slopes = jnp.exp2(-8.0 * jnp.arange(1, N_HEADS + 1, dtype=F32) / N_HEADS)
    attn_bias, attn_off = _attn_bias(slopes)

    row_vec = lambda v: v.astype(F32).reshape(1, -1)
    for l in range(depth):
        q_gain = jnp.tile(row_vec(q_norm_g[l]), (1, N_KV_HEADS)) * (LOG2E / math.sqrt(HEAD_DIM))
        k_gain = jnp.tile(row_vec(k_norm_g[l]), (1, N_KV_HEADS))
        sink_row = jnp.broadcast_to(
            (attn_sinks[l].astype(F32) * LOG2E).reshape(N_KV_HEADS, 1, KV_GROUP, 1),
            (N_KV_HEADS, 1, KV_GROUP, SUB)).reshape(N_KV_HEADS, 1, KV_GROUP * SUB)
        u_tok, attn_n = _inproj_attention(
            h_res, row_vec(norm_mix_g[l]), w_in, l, head_ones, q_gain, k_gain, attn_bias, attn_off,
            sink_row, row_vec(attn_out_g[l]), b)
        g_tok = _ssm(u_tok, bb[l], cc[l], pw[l], a16[l], row_vec(ssm_d[l]), b)
        h_res, xn = _outproj(attn_n, g_tok, h_res, w_glu,
                             row_vec(b_glu[l]), row_vec(ssm_out_g[l]), w_out,
                             row_vec(norm_mlp_g[l]), l)
        h_res = _mlp(xn, h_res, w_up, w_down, l)
    return h_res.reshape(b, lp, d)[:, N_META:length]
```

```python
import functools
import math

import jax
import jax.numpy as jnp
from jax import lax
from jax.experimental import pallas as pl
from jax.experimental.pallas import tpu as pltpu

D_MODEL = 2048
N_META = 16
HEAD_DIM = 64
ATTN_WIDTH = D_MODEL // 2
N_HEADS = ATTN_WIDTH // HEAD_DIM
N_KV_HEADS = N_HEADS // 4
KV_GROUP = N_HEADS // N_KV_HEADS
KV_WIDTH = N_KV_HEADS * HEAD_DIM
SSM_WIDTH = D_MODEL - ATTN_WIDTH
SSM_GROUP_CH = 16
SSM_GROUPS = SSM_WIDTH // SSM_GROUP_CH
SSM_STATE = 64
WINDOW = 128
BLOCK = 128
D_FF = 4 * D_MODEL
IN_WIDTH = ATTN_WIDTH + 2 * KV_WIDTH + SSM_WIDTH
NORM_EPS = 1e-6
NEG_INF = -1e30
LOG2E = math.log2(math.e)

CHUNK = 16
CHUNK_W = CHUNK * SSM_GROUP_CH
GROUPS_PER_STEP = 8
SCAN_UNROLL = 8
RELAYOUT_UNROLL = True

QKV_WIDTH = ATTN_WIDTH + 2 * KV_WIDTH
COL_K = ATTN_WIDTH
COL_V = ATTN_WIDTH + KV_WIDTH
LANES = 128
BLOCKS_PER_VREG = LANES // SSM_GROUP_CH

VMEM_LIMIT = 48 * 1024 * 1024
MLP_ROWS = 768
MLP_FF_TILE = 1024
MLP_TILES_FIRST = 2
MLP_FF_TILE_FIRST = 512
BIG_VMEM_LIMIT = 56 * 1024 * 1024

F32 = jnp.float32
BF16 = jnp.bfloat16


def _row_tile(rows, target):
    k = rows // BLOCK
    best = 1
    for d in range(1, k + 1):
        if k % d == 0 and d * BLOCK <= target:
            best = d
    return best * BLOCK


def _rms(x, gain):
    ms = jnp.mean(x * x, axis=-1, keepdims=True)
    return x * lax.rsqrt(ms + NORM_EPS) * gain


def _head_norm(x, head_ones, gain):
    x2 = x * x
    hi = x2.astype(BF16)
    lo = (x2 - hi.astype(F32)).astype(BF16)
    ssq = (jnp.dot(hi, head_ones, preferred_element_type=F32)
           + jnp.dot(lo, head_ones, preferred_element_type=F32))
    return x * lax.rsqrt(ssq * (1.0 / HEAD_DIM) + NORM_EPS) * gain


SUB = 64
BAND = WINDOW + SUB
KEYS = BAND + N_META
TILE = 3 * BLOCK


def _attn_bias(slopes):
    j = jnp.arange(KEYS)[:, None]
    i = jnp.arange(SUB)[None, :]
    is_band = j < BAND
    dist = WINDOW + i - j
    sl = slopes.reshape(N_KV_HEADS, 1, KV_GROUP, 1)
    dist_f = dist.astype(F32)[None, :, None, :]
    band = jnp.where(((dist >= 0) & (dist < WINDOW))[None, :, None, :], -sl * dist_f, NEG_INF)
    meta = -sl * (i - (j - BAND)).astype(F32)[None, :, None, :]
    bias = jnp.where(is_band[None, :, None, :], band, meta) * LOG2E
    lanes = KV_GROUP * SUB
    off_coef = jnp.broadcast_to(-sl * LOG2E, (N_KV_HEADS, N_META, KV_GROUP, SUB))
    return bias.reshape(N_KV_HEADS, KEYS, lanes), off_coef.reshape(N_KV_HEADS, N_META, lanes)


def _project_tile(x_ref, g_ref, w_bf, ones_ref, qg_ref, kg_ref, qkv_out, u_ref):
    h = _rms(x_ref[...], g_ref[...]).astype(BF16)
    proj = jnp.dot(h, w_bf[...], preferred_element_type=F32)
    head_ones = ones_ref[...]
    for blk in range(ATTN_WIDTH // KV_WIDTH):
        cols = slice(blk * KV_WIDTH, (blk + 1) * KV_WIDTH)
        qkv_out[:, cols] = _head_norm(proj[:, cols], head_ones, qg_ref[...]).astype(BF16)
    qkv_out[:, COL_K:COL_V] = _head_norm(proj[:, COL_K:COL_V], head_ones, kg_ref[...]).astype(BF16)
    qkv_out[:, COL_V:QKV_WIDTH] = proj[:, COL_V:QKV_WIDTH].astype(BF16)
    u_ref[...] = proj[:, QKV_WIDTH:]


def _attend_tile(qkv, tail, meta, bias_ref, offc_ref, sink_ref, og_ref, o_ref, acc, tile_pos, masked):
    lanes = KV_GROUP * SUB
    ones = jnp.ones((KEYS, 8), BF16)
    for blk in range(TILE // BLOCK):
        chains = [(sb, kh) for sb in range(BLOCK // SUB) for kh in range(N_KV_HEADS)]
        scores, values = [], []
        for sb, kh in chains:
            row0 = blk * BLOCK + sb * SUB
            start = tile_pos + row0
            first = row0 - WINDOW

            def window(col):
                cols = slice(col + kh * HEAD_DIM, col + (kh + 1) * HEAD_DIM)
                tcols = slice(cols.start - COL_K, cols.stop - COL_K)
                if first >= 0:
                    band = qkv[first:first + BAND, cols]
                else:
                    band = jnp.concatenate([tail[WINDOW + first:WINDOW, tcols],
                                            qkv[0:BAND + first, cols]], axis=0)
                return jnp.concatenate([band, meta[:, tcols]], axis=0)

            k_win = window(COL_K)
            values.append(jnp.concatenate([window(COL_V), ones], axis=1))
            q_s = jnp.concatenate(
                [qkv[row0:row0 + SUB, (kh * KV_GROUP + g) * HEAD_DIM:(kh * KV_GROUP + g + 1) * HEAD_DIM]
                 for g in range(KV_GROUP)], axis=0)
            s = lax.dot_general(k_win, q_s, (((1,), (1,)), ((), ())),
                                preferred_element_type=F32)
            s = s + bias_ref[kh]
            s = jnp.concatenate([s[:BAND], s[BAND:] + offc_ref[kh] * start.astype(F32)], axis=0)
            if masked:
                j = lax.broadcasted_iota(jnp.int32, (KEYS, lanes), 0)
                i = lax.broadcasted_iota(jnp.int32, (KEYS, lanes), 1) % SUB
                ok = jnp.where(j < BAND, start - WINDOW + j - N_META, start + i - (j - BAND)) >= 0
                s = jnp.where(ok, s, NEG_INF)
            scores.append(s)
        maxes = [jnp.maximum(jnp.max(s, axis=0, keepdims=True), sink_ref[kh])
                 for s, (sb, kh) in zip(scores, chains)]
        probs = [jnp.exp2(s - m).astype(BF16) for s, m in zip(scores, maxes)]
        for (sb, kh), p, m, v_ext in zip(chains, probs, maxes, values):
            o_ext = lax.dot_general(v_ext, p, (((0,), (0,)), ((), ())),
                                    preferred_element_type=F32)
            denom = o_ext[HEAD_DIM:HEAD_DIM + 1] + jnp.exp2(sink_ref[kh] - m)
            o = o_ext[:HEAD_DIM] / denom
            for g in range(KV_GROUP):
                h = kh * KV_GROUP + g
                acc[h * HEAD_DIM:(h + 1) * HEAD_DIM, sb * SUB:(sb + 1) * SUB] = o[:, g * SUB:(g + 1) * SUB]
        a_t = acc[...]
        ms = jnp.mean(a_t * a_t, axis=0, keepdims=True)
        a_n = a_t * lax.rsqrt(ms + NORM_EPS)
        o_ref[blk * BLOCK:(blk + 1) * BLOCK, :] = (a_n.T * og_ref[...]).astype(o_ref.dtype)


def _inproj_attn_kernel(x_ref, g_ref, w_ref, ones_ref, qg_ref, kg_ref, bias_ref, offc_ref,
                        sink_ref, og_ref, u_ref, attn_ref, w_bf, ring, tail, meta, acc,
                        *, tiles_per_seq, n_tiles):
    s = pl.program_id(0)

    @pl.when(s == 0)
    def _():
        w_bf[...] = w_ref[...].astype(BF16)
        ring[...] = jnp.zeros_like(ring)
        tail[...] = jnp.zeros_like(tail)
        meta[...] = jnp.zeros_like(meta)

    t_attn = jnp.maximum(s - 1, 0)
    seq_tile = t_attn % tiles_per_seq
    seq_slot = (t_attn // tiles_per_seq) % 2
    cur = (s + 1) % 2
    nxt = s % 2

    def step(masked):
        _attend_tile(ring.at[cur], tail, meta.at[seq_slot], bias_ref, offc_ref, sink_ref, og_ref,
                     attn_ref, acc, seq_tile * TILE, masked)
        tail[...] = ring[cur, TILE - WINDOW:TILE, COL_K:QKV_WIDTH]
        _project_tile(x_ref, g_ref, w_bf, ones_ref, qg_ref, kg_ref, ring.at[nxt], u_ref)

    @pl.when(seq_tile != 0)
    def _():
        step(False)

    @pl.when(seq_tile == 0)
    def _():
        step(True)

    t_proj = jnp.minimum(s, n_tiles - 1)

    @pl.when(t_proj % tiles_per_seq == 0)
    def _():
        meta[(t_proj // tiles_per_seq) % 2] = ring[nxt, 0:N_META, COL_K:QKV_WIDTH]


def _inproj_attention(x, gain, w_all, layer, head_ones, q_gain, k_gain, bias, off_coef, sink_row,
                      out_gain, n_seq):
    rows = x.shape[0]
    n_tiles = rows // TILE
    tiles_per_seq = n_tiles // n_seq
    assert rows == n_seq * tiles_per_seq * TILE
    kern = functools.partial(_inproj_attn_kernel, tiles_per_seq=tiles_per_seq, n_tiles=n_tiles)
    fixed = lambda s: (0, 0)
    fixed3 = lambda s: (0, 0, 0)
    proj_tile = lambda s: (jnp.minimum(s, n_tiles - 1), 0)
    return pl.pallas_call(
        kern,
        grid=(n_tiles + 1,),
        in_specs=[
            pl.BlockSpec((TILE, D_MODEL), proj_tile),
            pl.BlockSpec((1, D_MODEL), fixed),
            pl.BlockSpec((None, D_MODEL, IN_WIDTH), lambda s: (layer, 0, 0)),
            pl.BlockSpec((KV_WIDTH, KV_WIDTH), fixed),
            pl.BlockSpec((1, KV_WIDTH), fixed),
            pl.BlockSpec((1, KV_WIDTH), fixed),
            pl.BlockSpec(bias.shape, fixed3),
            pl.BlockSpec(off_coef.shape, fixed3),
            pl.BlockSpec(sink_row.shape, fixed3),
            pl.BlockSpec((1, ATTN_WIDTH), fixed),
        ],
        out_specs=[pl.BlockSpec((TILE, SSM_WIDTH), proj_tile),
                   pl.BlockSpec((TILE, ATTN_WIDTH), lambda s: (jnp.maximum(s - 1, 0), 0))],
        out_shape=[jax.ShapeDtypeStruct((rows, SSM_WIDTH), F32),
                   jax.ShapeDtypeStruct((rows, ATTN_WIDTH), BF16)],
        scratch_shapes=[
            pltpu.VMEM((D_MODEL, IN_WIDTH), BF16),
            pltpu.VMEM((2, TILE, QKV_WIDTH), BF16),
            pltpu.VMEM((WINDOW, 2 * KV_WIDTH), BF16),
            pltpu.VMEM((2, N_META, 2 * KV_WIDTH), BF16),
            pltpu.VMEM((ATTN_WIDTH, BLOCK), F32),
        ],
        compiler_params=pltpu.CompilerParams(
            dimension_semantics=("arbitrary",), vmem_limit_bytes=BIG_VMEM_LIMIT),
        name="inproj_attn",
    )(x, gain, w_all, head_ones, q_gain, k_gain, bias, off_coef, sink_row, out_gain)


def _gelu_tanh(x):
    c = math.sqrt(2.0 / math.pi)
    return 0.5 * x * (1.0 + jnp.tanh(c * (x + 0.044715 * (x * x * x))))


def _block_transpose_stage(arrs, dist, lane_blk):
    nblk = BLOCKS_PER_VREG
    keep = (lane_blk & dist) == 0
    nxt = list(arrs)
    for i in range(nblk):
        if i & dist == 0:
            lo, hi = arrs[i], arrs[i + dist]
            nxt[i] = jnp.where(keep, lo, pltpu.roll(hi, dist * SSM_GROUP_CH, axis=1))
            nxt[i + dist] = jnp.where(keep, pltpu.roll(lo, (nblk - dist) * SSM_GROUP_CH, axis=1), hi)
    return nxt


def _pipelined_block_transpose(n_tiles, load, store, zero, lane_blk):
    def trip(r, carry):
        after1, after2 = carry
        fresh = load(jnp.minimum(r, n_tiles - 1))
        new1 = [_block_transpose_stage(g, 4, lane_blk) for g in fresh]
        new2 = [_block_transpose_stage(g, 2, lane_blk) for g in after1]
        done = [_block_transpose_stage(g, 1, lane_blk) for g in after2]
        store(jnp.maximum(r - 2, 0), done)
        return new1, new2

    groups = CHUNK // BLOCKS_PER_VREG
    init = [[zero] * BLOCKS_PER_VREG for _ in range(groups)]
    lax.fori_loop(0, n_tiles + 2, trip, (init, init), unroll=RELAYOUT_UNROLL)


def _ssm_kernel(u_ref, bb_ref, cc_ref, pw_ref, a_ref, d_ref, o_ref,
                lhs, yscr, m_scr, win_scr, wsw_scr, wct_scr, zin, zsw, st,
                *, n_chunks, n_batch):
    nblk = BLOCKS_PER_VREG
    sw = 2 * SSM_STATE
    rc = n_batch * n_chunks
    half = CHUNK // nblk

    def tok_rows(tile, tile_chunks, t):
        return pl.ds(tile * (tile_chunks * CHUNK) + t, tile_chunks, stride=CHUNK)

    lane256 = lax.broadcasted_iota(jnp.int32, (CHUNK, CHUNK_W), 1)
    for g in range(GROUPS_PER_STEP):
        bb, bbs, bsw, bsws = bb_ref[g, 0], bb_ref[g, 1], bb_ref[g, 2], bb_ref[g, 3]
        cc, ccs = cc_ref[g, 0], cc_ref[g, 1]
        c_pow = [pw_ref[g, 0, tau:tau + 1, :] * cc + pw_ref[g, 1, tau:tau + 1, :] * ccs
                 for tau in range(CHUNK + 1)]
        for t in range(CHUNK):
            rows = slice(t * CHUNK, (t + 1) * CHUNK)
            p_re = pw_ref[g, 0, CHUNK - 1 - t:CHUNK - t, :]
            p_im = pw_ref[g, 1, CHUNK - 1 - t:CHUNK - t, :]
            win_scr[g, rows, :] = (p_re * bb + p_im * bbs).astype(BF16)
            wsw_scr[g, rows, :] = (p_re * bsw + p_im * bsws).astype(BF16)
            wct_scr[g, rows, :] = c_pow[t + 1].astype(BF16)
        base = lax.dot_general(bb, jnp.concatenate(c_pow[:CHUNK], axis=0), (((1,), (1,)), ((), ())),
                               precision=lax.Precision.HIGHEST, preferred_element_type=F32)
        for sblk in range(CHUNK):
            row = base if sblk == 0 else jnp.where(
                lane256 >= sblk * SSM_GROUP_CH, pltpu.roll(base, sblk * SSM_GROUP_CH, axis=1), 0.0)
            m_scr[g, sblk * CHUNK:(sblk + 1) * CHUNK, :] = row.astype(BF16)

    rt_in = 16
    lane_blk_in = lax.broadcasted_iota(jnp.int32, (rt_in // 2, LANES), 1) // SSM_GROUP_CH

    def load_u(r):
        return [[pltpu.bitcast(u_ref[tok_rows(r, rt_in, hf * nblk + tb), :].astype(BF16), jnp.uint32)
                 for tb in range(nblk)] for hf in range(half)]

    def store_lhs(r, groups):
        rows = pl.ds(pl.multiple_of(r * rt_in, rt_in), rt_in)
        for hf in range(half):
            for g in range(nblk):
                col = g * CHUNK_W + hf * LANES
                lhs[rows, col:col + LANES] = pltpu.bitcast(groups[hf][g], BF16)

    _pipelined_block_transpose(rc // rt_in, load_u, store_lhs,
                               jnp.zeros((rt_in // 2, LANES), jnp.uint32), lane_blk_in)

    for g in range(GROUPS_PER_STEP):
        u_g = lhs[:, g * CHUNK_W:(g + 1) * CHUNK_W]
        zin[:, g * sw:(g + 1) * sw] = jnp.dot(u_g, win_scr[g], preferred_element_type=F32)
        zsw[:, g * sw:(g + 1) * sw] = jnp.dot(u_g, wsw_scr[g], preferred_element_type=F32)

    a1 = a_ref[0:1, :]
    a2 = a_ref[1:2, :]

    def step(c, carry):
        nxt = []
        for b in range(n_batch):
            s, x = carry[b]
            row = b * n_chunks + c
            st[pl.ds(row, 1), :] = s
            nxt.append((a1 * s + a2 * x + zin[pl.ds(row, 1), :],
                        a1 * x - a2 * s + zsw[pl.ds(row, 1), :]))
        return tuple(nxt)

    zero = jnp.zeros((1, GROUPS_PER_STEP * sw), F32)
    lax.fori_loop(0, n_chunks, step, tuple((zero, zero) for _ in range(n_batch)),
                  unroll=SCAN_UNROLL)

    for g in range(GROUPS_PER_STEP):
        cols = slice(g * CHUNK_W, (g + 1) * CHUNK_W)
        y = jnp.dot(lhs[:, cols], m_scr[g], preferred_element_type=F32)
        y = y + lax.dot_general(st[:, g * sw:(g + 1) * sw].astype(BF16), wct_scr[g],
                                (((1,), (1,)), ((), ())), preferred_element_type=F32)
        yscr[:, cols] = y.astype(BF16)

    d_row = d_ref[...]

    def load_y(r):
        rows = pl.ds(pl.multiple_of(r * rt_in, rt_in), rt_in)
        return [[pltpu.bitcast(yscr[rows, g * CHUNK_W + hf * LANES:g * CHUNK_W + (hf + 1) * LANES],
                               jnp.uint32)
                 for g in range(nblk)] for hf in range(half)]

    def store_tokens(r, groups):
        for hf in range(half):
            for tb in range(nblk):
                rows = tok_rows(r, rt_in, hf * nblk + tb)
                y = pltpu.bitcast(groups[hf][tb], BF16).astype(F32)
                o_ref[rows, :] = y + d_row * u_ref[rows, :]

    _pipelined_block_transpose(rc // rt_in, load_y, store_tokens,
                               jnp.zeros((rt_in // 2, LANES), jnp.uint32), lane_blk_in)


def _ssm(u_tok, bb, cc, pw, a16, d, n_batch):
    rows, width = u_tok.shape
    rc = rows // CHUNK
    n_chunks = rc // n_batch
    assert GROUPS_PER_STEP == BLOCKS_PER_VREG and rc % 16 == 0
    gs = GROUPS_PER_STEP
    bw = gs * CHUNK_W
    sw = gs * 2 * SSM_STATE
    kern = functools.partial(_ssm_kernel, n_chunks=n_chunks, n_batch=n_batch)
    grp4 = lambda j: (j, 0, 0, 0)
    return pl.pallas_call(
        kern,
        grid=(SSM_GROUPS // gs,),
        in_specs=[
            pl.BlockSpec((rows, LANES), lambda j: (0, j)),
            pl.BlockSpec((gs,) + bb.shape[1:], grp4),
            pl.BlockSpec((gs,) + cc.shape[1:], grp4),
            pl.BlockSpec((gs,) + pw.shape[1:], grp4),
            pl.BlockSpec((2, sw), lambda j: (0, j)),
            pl.BlockSpec((1, LANES), lambda j: (0, j)),
        ],
        out_specs=pl.BlockSpec((rows, LANES), lambda j: (0, j)),
        out_shape=jax.ShapeDtypeStruct((rows, width), F32),
        scratch_shapes=[
            pltpu.VMEM((rc, bw), BF16),
            pltpu.VMEM((rc, bw), BF16),
            pltpu.VMEM((gs, CHUNK_W, CHUNK_W), BF16),
            pltpu.VMEM((gs, CHUNK_W, 2 * SSM_STATE), BF16),
            pltpu.VMEM((gs, CHUNK_W, 2 * SSM_STATE), BF16),
            pltpu.VMEM((gs, CHUNK_W, 2 * SSM_STATE), BF16),
            pltpu.VMEM((rc, sw), F32),
            pltpu.VMEM((rc, sw), F32),
            pltpu.VMEM((rc, sw), F32),
        ],
        compiler_params=pltpu.CompilerParams(
            dimension_semantics=("parallel",), vmem_limit_bytes=VMEM_LIMIT),
        name="ssm",
    )(u_tok, bb, cc, pw, a16, d)


def _ssm_tables(lam_re, lam_im, log_step, b_re, b_im, c_re, c_im):
    depth = lam_re.shape[0]
    g, p, h, t = SSM_GROUPS, SSM_STATE, SSM_GROUP_CH, CHUNK
    lam = lax.complex(lam_re.astype(F32), lam_im.astype(F32))
    delta = jnp.exp(log_step.astype(F32))[..., None]
    ld = lam * delta
    lam_bar = jnp.exp(ld)
    b_bar = ((lam_bar - 1.0) / lam)[..., None] * lax.complex(b_re.astype(F32), b_im.astype(F32))
    c_c = lax.complex(c_re.astype(F32), c_im.astype(F32))
    pows = [jnp.ones_like(lam_bar), lam_bar]
    for _ in range(t - 1):
        pows.append(pows[-1] * lam_bar)
    powers = jnp.stack(pows, axis=2)

    cat = lambda x, y: jnp.concatenate([x, y], axis=-1)
    br, bi = jnp.real(b_bar).swapaxes(-1, -2), jnp.imag(b_bar).swapaxes(-1, -2)
    bb = jnp.stack([cat(br, bi), cat(-bi, br), cat(bi, br), cat(br, -bi)], axis=2)
    cr, ci = jnp.real(c_c), jnp.imag(c_c)
    cc = jnp.stack([cat(cr, -ci), cat(-ci, -cr)], axis=2)
    pr, pi = jnp.real(powers), jnp.imag(powers)
    pw = jnp.stack([cat(pr, pr), cat(pi, pi)], axis=2)
    pw = jnp.pad(pw, ((0, 0), (0, 0), (0, 0), (0, 24 - (t + 1)), (0, 0)))
    ar, ai = pr[:, :, t], pi[:, :, t]
    a16 = jnp.stack([cat(ar, ar).reshape(depth, g * 2 * p),
                     cat(-ai, ai).reshape(depth, g * 2 * p)], axis=1)
    return bb, cc, pw, a16


def _outproj_kernel(a_ref, g_ref, x_ref, wglu_f32, bglu_ref, sg_ref, wout_f32, ng_ref,
                    o_ref, xn_ref, wglu_ref, wout_ref):
    @pl.when(pl.program_id(0) == 0)
    def _():
        wglu_ref[...] = wglu_f32[...].astype(BF16)
        wout_ref[...] = wout_f32[...].astype(BF16)

    g = _gelu_tanh(g_ref[...])
    z = jnp.dot(g.astype(BF16), wglu_ref[...], preferred_element_type=F32) + bglu_ref[...]
    s = g * jax.nn.sigmoid(z)
    s_n = _rms(s, sg_ref[...]).astype(BF16)
    y = jnp.dot(a_ref[...], wout_ref[:ATTN_WIDTH, :], preferred_element_type=F32)
    y = y + jnp.dot(s_n, wout_ref[ATTN_WIDTH:, :], preferred_element_type=F32)
    h = x_ref[...] + y
    o_ref[...] = h
    xn_ref[...] = _rms(h, ng_ref[...]).astype(xn_ref.dtype)


def _outproj(attn_n, g_tok, x, w_glu_all, b_glu, ssm_gain, w_out_all, mlp_gain, layer):
    rows = x.shape[0]
    tm = _row_tile(rows, 384)
    row = lambda i: (i, 0)
    fixed = lambda i: (0, 0)
    return pl.pallas_call(
        _outproj_kernel,
        grid=(rows // tm,),
        in_specs=[
            pl.BlockSpec((tm, ATTN_WIDTH), row),
            pl.BlockSpec((tm, SSM_WIDTH), row),
            pl.BlockSpec((tm, D_MODEL), row),
            pl.BlockSpec((None, SSM_WIDTH, SSM_WIDTH), lambda i: (layer, 0, 0)),
            pl.BlockSpec((1, SSM_WIDTH), fixed),
            pl.BlockSpec((1, SSM_WIDTH), fixed),
            pl.BlockSpec((None, D_MODEL, D_MODEL), lambda i: (layer, 0, 0)),
            pl.BlockSpec((1, D_MODEL), fixed),
        ],
        out_specs=[pl.BlockSpec((tm, D_MODEL), row), pl.BlockSpec((tm, D_MODEL), row)],
        out_shape=[jax.ShapeDtypeStruct((rows, D_MODEL), F32),
                   jax.ShapeDtypeStruct((rows, D_MODEL), BF16)],
        scratch_shapes=[pltpu.VMEM((SSM_WIDTH, SSM_WIDTH), BF16), pltpu.VMEM((D_MODEL, D_MODEL), BF16)],
        compiler_params=pltpu.CompilerParams(
            dimension_semantics=("arbitrary",), vmem_limit_bytes=BIG_VMEM_LIMIT),
        name="outproj",
    )(attn_n, g_tok, x, w_glu_all, b_glu, ssm_gain, w_out_all, mlp_gain)


def _mlp_tile(xn, w_up, w_down, o_ref, rows):
    a = jnp.dot(xn, w_up, preferred_element_type=F32)
    a = jnp.square(jnp.maximum(a, 0.0)).astype(BF16)
    o_ref[rows, :] += jnp.dot(a, w_down, preferred_element_type=F32)


def _mlp_first_kernel(xn_ref, x_hbm, wu_ref, wd_ref, o_ref, wu_bf_ref, wd_bf_ref, sem, *, tm):
    @pl.when(pl.program_id(0) == 0)
    def _():
        copy = pltpu.make_async_copy(x_hbm.at[pl.ds(0, o_ref.shape[0])], o_ref, sem)
        copy.start()
        copy.wait()

    wu_bf_ref[...] = wu_ref[...].astype(BF16)
    wd_bf_ref[...] = wd_ref[...].astype(BF16)
    for t in range(o_ref.shape[0] // tm):
        rows = slice(t * tm, (t + 1) * tm)
        _mlp_tile(xn_ref[rows, :], wu_bf_ref[...], wd_bf_ref[...], o_ref, rows)


def _mlp_rest_kernel(xn_ref, x_ref, wu_ref, wd_ref, first_hbm, o_ref, a_even, a_odd, sem,
                     *, skip, n_ff):
    i = pl.program_id(0)
    f = pl.program_id(1)
    tm = o_ref.shape[0]

    def up(dst):
        a = jnp.dot(xn_ref[...], wu_ref[...], preferred_element_type=F32)
        dst[...] = jnp.square(jnp.maximum(a, 0.0)).astype(BF16)

    def down(src):
        o_ref[...] += jnp.dot(src[...], wd_ref[...], preferred_element_type=F32)

    @pl.when((i < skip) & (f == 0))
    def _():
        copy = pltpu.make_async_copy(first_hbm.at[pl.ds(pl.multiple_of(i * tm, tm), tm)], o_ref, sem)
        copy.start()
        copy.wait()

    @pl.when(i >= skip)
    def _():
        @pl.when(f == 0)
        def _():
            o_ref[...] = x_ref[...]
            up(a_even)

        for parity, (new, old) in ((0, (a_even, a_odd)), (1, (a_odd, a_even))):
            @pl.when((f > 0) & (f < n_ff) & (f % 2 == parity))
            def _(new=new, old=old):
                up(new)
                down(old)

        @pl.when(f == n_ff)
        def _():
            down(a_odd if (n_ff - 1) % 2 else a_even)


def _mlp(xn, x, w_up_all, w_down_all, layer):
    rows = x.shape[0]
    tm = _row_tile(rows, MLP_ROWS)
    skip = min(MLP_TILES_FIRST, rows // tm)
    tm_first = skip * tm
    tf_first, tf = MLP_FF_TILE_FIRST, MLP_FF_TILE
    first_rows = lambda f: (0, 0)
    first, w_up_bf, w_down_bf = pl.pallas_call(
        functools.partial(_mlp_first_kernel, tm=tm),
        grid=(D_FF // tf_first,),
        in_specs=[
            pl.BlockSpec((tm_first, D_MODEL), first_rows),
            pl.BlockSpec(memory_space=pl.ANY),
            pl.BlockSpec((None, D_MODEL, tf_first), lambda f: (layer, 0, f)),
            pl.BlockSpec((None, tf_first, D_MODEL), lambda f: (layer, f, 0)),
        ],
        out_specs=[pl.BlockSpec((tm_first, D_MODEL), first_rows),
                   pl.BlockSpec((D_MODEL, tf_first), lambda f: (0, f)),
                   pl.BlockSpec((tf_first, D_MODEL), lambda f: (f, 0))],
        out_shape=[jax.ShapeDtypeStruct((tm_first, D_MODEL), F32),
                   jax.ShapeDtypeStruct((D_MODEL, D_FF), BF16),
                   jax.ShapeDtypeStruct((D_FF, D_MODEL), BF16)],
        scratch_shapes=[pltpu.SemaphoreType.DMA(())],
        compiler_params=pltpu.CompilerParams(
            dimension_semantics=("arbitrary",), vmem_limit_bytes=BIG_VMEM_LIMIT),
        name="mlp_first",
    )(xn, x, w_up_all, w_down_all)
    if rows == tm_first:
        return first
    n_ff = D_FF // tf
    up_tile = lambda i, f: jnp.where(i < skip, 0, jnp.minimum(f, n_ff - 1))
    down_tile = lambda i, f: jnp.where(i < skip, 0, jnp.maximum(f - 1, 0))
    return pl.pallas_call(
        functools.partial(_mlp_rest_kernel, skip=skip, n_ff=n_ff),
        grid=(rows // tm, n_ff + 1),
        in_specs=[
            pl.BlockSpec((tm, D_MODEL), lambda i, f: (jnp.maximum(i, skip), 0)),
            pl.BlockSpec((tm, D_MODEL), lambda i, f: (jnp.maximum(i, skip), 0)),
            pl.BlockSpec((D_MODEL, tf), lambda i, f: (0, up_tile(i, f))),
            pl.BlockSpec((tf, D_MODEL), lambda i, f: (down_tile(i, f), 0)),
            pl.BlockSpec(memory_space=pl.ANY),
        ],
        out_specs=pl.BlockSpec((tm, D_MODEL), lambda i, f: (i, 0)),
        out_shape=jax.ShapeDtypeStruct((rows, D_MODEL), F32),
        scratch_shapes=[pltpu.VMEM((tm, tf), BF16), pltpu.VMEM((tm, tf), BF16),
                        pltpu.SemaphoreType.DMA(())],
        compiler_params=pltpu.CompilerParams(
            dimension_semantics=("arbitrary", "arbitrary"), vmem_limit_bytes=BIG_VMEM_LIMIT),
        name="mlp_rest",
    )(xn, x, w_up_bf, w_down_bf, first)


def kernel(x, meta_tokens, norm_mix_g, w_in, q_norm_g, k_norm_g, attn_sinks, ssm_lambda_re, ssm_lambda_im, ssm_log_step, ssm_b_re, ssm_b_im, ssm_c_re, ssm_c_im, ssm_d, w_glu, b_glu, attn_out_g, ssm_out_g, w_out, norm_mlp_g, w_up, w_down):
    b, seq, d = x.shape
    depth = w_in.shape[0]
    assert d == D_MODEL
    length = N_META + seq
    lp = -(-length // TILE) * TILE
    rows = b * lp

    meta = jnp.broadcast_to(meta_tokens.astype(x.dtype)[None], (b, N_META, d))
    pad = jnp.zeros((b, lp - length, d), x.dtype)
    h_res = jnp.concatenate([meta, x, pad], axis=1).reshape(rows, d)

    bb, cc, pw, a16 = _ssm_tables(
        ssm_lambda_re, ssm_lambda_im, ssm_log_step, ssm_b_re, ssm_b_im, ssm_c_re, ssm_c_im)

    head_id = jnp.arange(KV_WIDTH) // HEAD_DIM
    head_ones = (head_id[:, None] == head_id[None, :]).astype(BF16)
    slopes = jnp.exp2(-8.0 * jnp.arange(1, N_HEADS + 1, dtype=F32) / N_HEADS)
    attn_bias, attn_off = _attn_bias(slopes)

    row_vec = lambda v: v.astype(F32).reshape(1, -1)
    for l in range(depth):
        q_gain = jnp.tile(row_vec(q_norm_g[l]), (1, N_KV_HEADS)) * (LOG2E / math.sqrt(HEAD_DIM))
        k_gain = jnp.tile(row_vec(k_norm_g[l]), (1, N_KV_HEADS))
        sink_row = jnp.broadcast_to(
            (attn_sinks[l].astype(F32) * LOG2E).reshape(N_KV_HEADS, 1, KV_GROUP, 1),
            (N_KV_HEADS, 1, KV_GROUP, SUB)).reshape(N_KV_HEADS, 1, KV_GROUP * SUB)
        u_tok, attn_n = _inproj_attention(
            h_res, row_vec(norm_mix_g[l]), w_in, l, head_ones, q_gain, k_gain, attn_bias, attn_off,
            sink_row, row_vec(attn_out_g[l]), b)
        g_tok = _ssm(u_tok, bb[l], cc[l], pw[l], a16[l], row_vec(ssm_d[l]), b)
        h_res, xn = _outproj(attn_n, g_tok, h_res, w_glu,
                             row_vec(b_glu[l]), row_vec(ssm_out_g[l]), w_out,
                             row_vec(norm_mlp_g[l]), l)
        h_res = _mlp(xn, h_res, w_up, w_down, l)
    return h_res.reshape(b, lp, d)[:, N_META:length]
```

```python
import functools
import math

import jax
import jax.numpy as jnp
from jax import lax
from jax.experimental import pallas as pl
from jax.experimental.pallas import tpu as pltpu

D_MODEL = 2048
N_META = 16
HEAD_DIM = 64
ATTN_WIDTH = D_MODEL // 2
N_HEADS = ATTN_WIDTH // HEAD_DIM
N_KV_HEADS = N_HEADS // 4
KV_GROUP = N_HEADS // N_KV_HEADS
KV_WIDTH = N_KV_HEADS * HEAD_DIM
SSM_WIDTH = D_MODEL - ATTN_WIDTH
SSM_GROUP_CH = 16
SSM_GROUPS = SSM_WIDTH // SSM_GROUP_CH
SSM_STATE = 64
WINDOW = 128
BLOCK = 128
D_FF = 4 * D_MODEL
IN_WIDTH = ATTN_WIDTH + 2 * KV_WIDTH + SSM_WIDTH
NORM_EPS = 1e-6
NEG_INF = -1e30
LOG2E = math.log2(math.e)

CHUNK = 16
CHUNK_W = CHUNK * SSM_GROUP_CH
GROUPS_PER_STEP = 8
SCAN_UNROLL = 8
RELAYOUT_UNROLL = True

QKV_WIDTH = ATTN_WIDTH + 2 * KV_WIDTH
COL_K = ATTN_WIDTH
COL_V = ATTN_WIDTH + KV_WIDTH
LANES = 128
BLOCKS_PER_VREG = LANES // SSM_GROUP_CH

VMEM_LIMIT = 48 * 1024 * 1024
MLP_ROWS = 768
MLP_FF_TILE = 1024
MLP_TILES_FIRST = 2
MLP_FF_TILE_FIRST = 512
BIG_VMEM_LIMIT = 56 * 1024 * 1024

F32 = jnp.float32
BF16 = jnp.bfloat16


def _row_tile(rows, target):
    k = rows // BLOCK
    best = 1
    for d in range(1, k + 1):
        if k % d == 0 and d * BLOCK <= target:
            best = d
    return best * BLOCK


def _rms(x, gain):
    ms = jnp.mean(x * x, axis=-1, keepdims=True)
    return x * lax.rsqrt(ms + NORM_EPS) * gain


def _head_norm(x, head_ones, gain):
    x2 = x * x
    hi = x2.astype(BF16)
    lo = (x2 - hi.astype(F32)).astype(BF16)
    ssq = (jnp.dot(hi, head_ones, preferred_element_type=F32)
           + jnp.dot(lo, head_ones, preferred_element_type=F32))
    return x * lax.rsqrt(ssq * (1.0 / HEAD_DIM) + NORM_EPS) * gain


SUB = 64
BAND = WINDOW + SUB
KEYS = BAND + N_META
TILE = 3 * BLOCK


def _attn_bias(slopes):
    j = jnp.arange(KEYS)[:, None]
    i = jnp.arange(SUB)[None, :]
    is_band = j < BAND
    dist = WINDOW + i - j
    sl = slopes.reshape(N_KV_HEADS, 1, KV_GROUP, 1)
    dist_f = dist.astype(F32)[None, :, None, :]
    band = jnp.where(((dist >= 0) & (dist < WINDOW))[None, :, None, :], -sl * dist_f, NEG_INF)
    meta = -sl * (i - (j - BAND)).astype(F32)[None, :, None, :]
    bias = jnp.where(is_band[None, :, None, :], band, meta) * LOG2E
    lanes = KV_GROUP * SUB
    off_coef = jnp.broadcast_to(-sl * LOG2E, (N_KV_HEADS, N_META, KV_GROUP, SUB))
    return bias.reshape(N_KV_HEADS, KEYS, lanes), off_coef.reshape(N_KV_HEADS, N_META, lanes)


def _project_tile(x_ref, g_ref, w_bf, ones_ref, qg_ref, kg_ref, qkv_out, u_ref):
    h = _rms(x_ref[...], g_ref[...]).astype(BF16)
    proj = jnp.dot(h, w_bf[...], preferred_element_type=F32)
    head_ones = ones_ref[...]
    for blk in range(ATTN_WIDTH // KV_WIDTH):
        cols = slice(blk * KV_WIDTH, (blk + 1) * KV_WIDTH)
        qkv_out[:, cols] = _head_norm(proj[:, cols], head_ones, qg_ref[...]).astype(BF16)
    qkv_out[:, COL_K:COL_V] = _head_norm(proj[:, COL_K:COL_V], head_ones, kg_ref[...]).astype(BF16)
    qkv_out[:, COL_V:QKV_WIDTH] = proj[:, COL_V:QKV_WIDTH].astype(BF16)
    u_ref[...] = proj[:, QKV_WIDTH:]


def _attend_tile(qkv, tail, meta, bias_ref, offc_ref, sink_ref, og_ref, o_ref, acc, tile_pos, masked):
    lanes = KV_GROUP * SUB
    ones = jnp.ones((KEYS, 8), BF16)
    for blk in range(TILE // BLOCK):
        chains = [(sb, kh) for sb in range(BLOCK // SUB) for kh in range(N_KV_HEADS)]
        scores, values = [], []
        for sb, kh in chains:
            row0 = blk * BLOCK + sb * SUB
            start = tile_pos + row0
            first = row0 - WINDOW

            def window(col):
                cols = slice(col + kh * HEAD_DIM, col + (kh + 1) * HEAD_DIM)
                tcols = slice(cols.start - COL_K, cols.stop - COL_K)
                if first >= 0:
                    band = qkv[first:first + BAND, cols]
                else:
                    band = jnp.concatenate([tail[WINDOW + first:WINDOW, tcols],
                                            qkv[0:BAND + first, cols]], axis=0)
                return jnp.concatenate([band, meta[:, tcols]], axis=0)

            k_win = window(COL_K)
            values.append(jnp.concatenate([window(COL_V), ones], axis=1))
            q_s = jnp.concatenate(
                [qkv[row0:row0 + SUB, (kh * KV_GROUP + g) * HEAD_DIM:(kh * KV_GROUP + g + 1) * HEAD_DIM]
                 for g in range(KV_GROUP)], axis=0)
            s = lax.dot_general(k_win, q_s, (((1,), (1,)), ((), ())),
                                preferred_element_type=F32)
            s = s + bias_ref[kh]
            s = jnp.concatenate([s[:BAND], s[BAND:] + offc_ref[kh] * start.astype(F32)], axis=0)
            if masked:
                j = lax.broadcasted_iota(jnp.int32, (KEYS, lanes), 0)
                i = lax.broadcasted_iota(jnp.int32, (KEYS, lanes), 1) % SUB
                ok = jnp.where(j < BAND, start - WINDOW + j - N_META, start + i - (j - BAND)) >= 0
                s = jnp.where(ok, s, NEG_INF)
            scores.append(s)
        maxes = [jnp.maximum(jnp.max(s, axis=0, keepdims=True), sink_ref[kh])
                 for s, (sb, kh) in zip(scores, chains)]
        probs = [jnp.exp2(s - m).astype(BF16) for s, m in zip(scores, maxes)]
        for (sb, kh), p, m, v_ext in zip(chains, probs, maxes, values):
            o_ext = lax.dot_general(v_ext, p, (((0,), (0,)), ((), ())),
                                    preferred_element_type=F32)
            denom = o_ext[HEAD_DIM:HEAD_DIM + 1] + jnp.exp2(sink_ref[kh] - m)
            o = o_ext[:HEAD_DIM] / denom
            for g in range(KV_GROUP):
                h = kh * KV_GROUP + g
                acc[h * HEAD_DIM:(h + 1) * HEAD_DIM, sb * SUB:(sb + 1) * SUB] = o[:, g * SUB:(g + 1) * SUB]
        a_t = acc[...]
        ms = jnp.mean(a_t * a_t, axis=0, keepdims=True)
        a_n = a_t * lax.rsqrt(ms + NORM_EPS)
        o_ref[blk * BLOCK:(blk + 1) * BLOCK, :] = (a_n.T * og_ref[...]).astype(o_ref.dtype)


def _inproj_attn_kernel(x_ref, g_ref, w_ref, ones_ref, qg_ref, kg_ref, bias_ref, offc_ref,
                        sink_ref, og_ref, u_ref, attn_ref, w_bf, ring, tail, meta, acc,
                        *, tiles_per_seq, n_tiles):
    s = pl.program_id(0)

    @pl.when(s == 0)
    def _():
        w_bf[...] = w_ref[...].astype(BF16)
        ring[...] = jnp.zeros_like(ring)
        tail[...] = jnp.zeros_like(tail)
        meta[...] = jnp.zeros_like(meta)

    t_attn = jnp.maximum(s - 1, 0)
    seq_tile = t_attn % tiles_per_seq
    seq_slot = (t_attn // tiles_per_seq) % 2
    cur = (s + 1) % 2
    nxt = s % 2

    def step(masked):
        _attend_tile(ring.at[cur], tail, meta.at[seq_slot], bias_ref, offc_ref, sink_ref, og_ref,
                     attn_ref, acc, seq_tile * TILE, masked)
        tail[...] = ring[cur, TILE - WINDOW:TILE, COL_K:QKV_WIDTH]
        _project_tile(x_ref, g_ref, w_bf, ones_ref, qg_ref, kg_ref, ring.at[nxt], u_ref)

    @pl.when(seq_tile != 0)
    def _():
        step(False)

    @pl.when(seq_tile == 0)
    def _():
        step(True)

    t_proj = jnp.minimum(s, n_tiles - 1)

    @pl.when(t_proj % tiles_per_seq == 0)
    def _():
        meta[(t_proj // tiles_per_seq) % 2] = ring[nxt, 0:N_META, COL_K:QKV_WIDTH]


def _inproj_attention(x, gain, w_all, layer, head_ones, q_gain, k_gain, bias, off_coef, sink_row,
                      out_gain, n_seq):
    rows = x.shape[0]
    n_tiles = rows // TILE
    tiles_per_seq = n_tiles // n_seq
    assert rows == n_seq * tiles_per_seq * TILE
    kern = functools.partial(_inproj_attn_kernel, tiles_per_seq=tiles_per_seq, n_tiles=n_tiles)
    fixed = lambda s: (0, 0)
    fixed3 = lambda s: (0, 0, 0)
    proj_tile = lambda s: (jnp.minimum(s, n_tiles - 1), 0)
    return pl.pallas_call(
        kern,
        grid=(n_tiles + 1,),
        in_specs=[
            pl.BlockSpec((TILE, D_MODEL), proj_tile),
            pl.BlockSpec((1, D_MODEL), fixed),
            pl.BlockSpec((None, D_MODEL, IN_WIDTH), lambda s: (layer, 0, 0)),
            pl.BlockSpec((KV_WIDTH, KV_WIDTH), fixed),
            pl.BlockSpec((1, KV_WIDTH), fixed),
            pl.BlockSpec((1, KV_WIDTH), fixed),
            pl.BlockSpec(bias.shape, fixed3),
            pl.BlockSpec(off_coef.shape, fixed3),
            pl.BlockSpec(sink_row.shape, fixed3),
            pl.BlockSpec((1, ATTN_WIDTH), fixed),
        ],
        out_specs=[pl.BlockSpec((TILE, SSM_WIDTH), proj_tile),
                   pl.BlockSpec((TILE, ATTN_WIDTH), lambda s: (jnp.maximum(s - 1, 0), 0))],
        out_shape=[jax.ShapeDtypeStruct((rows, SSM_WIDTH), F32),
                   jax.ShapeDtypeStruct((rows, ATTN_WIDTH), BF16)],
        scratch_shapes=[
            pltpu.VMEM((D_MODEL, IN_WIDTH), BF16),
            pltpu.VMEM((2, TILE, QKV_WIDTH), BF16),
            pltpu.VMEM((WINDOW, 2 * KV_WIDTH), BF16),
            pltpu.VMEM((2, N_META, 2 * KV_WIDTH), BF16),
            pltpu.VMEM((ATTN_WIDTH, BLOCK), F32),
        ],
        compiler_params=pltpu.CompilerParams(
            dimension_semantics=("arbitrary",), vmem_limit_bytes=BIG_VMEM_LIMIT),
        name="inproj_attn",
    )(x, gain, w_all, head_ones, q_gain, k_gain, bias, off_coef, sink_row, out_gain)


def _gelu_tanh(x):
    c = math.sqrt(2.0 / math.pi)
    return 0.5 * x * (1.0 + jnp.tanh(c * (x + 0.044715 * (x * x * x))))


def _block_transpose_stage(arrs, dist, lane_blk):
    nblk = BLOCKS_PER_VREG
    keep = (lane_blk & dist) == 0
    nxt = list(arrs)
    for i in range(nblk):
        if i & dist == 0:
            lo, hi = arrs[i], arrs[i + dist]
            nxt[i] = jnp.where(keep, lo, pltpu.roll(hi, dist * SSM_GROUP_CH, axis=1))
            nxt[i + dist] = jnp.where(keep, pltpu.roll(lo, (nblk - dist) * SSM_GROUP_CH, axis=1), hi)
    return nxt


def _pipelined_block_transpose(n_tiles, load, store, zero, lane_blk):
    def trip(r, carry):
        after1, after2 = carry
        fresh = load(jnp.minimum(r, n_tiles - 1))
        new1 = [_block_transpose_stage(g, 4, lane_blk) for g in fresh]
        new2 = [_block_transpose_stage(g, 2, lane_blk) for g in after1]
        done = [_block_transpose_stage(g, 1, lane_blk) for g in after2]
        store(jnp.maximum(r - 2, 0), done)
        return new1, new2

    groups = CHUNK // BLOCKS_PER_VREG
    init = [[zero] * BLOCKS_PER_VREG for _ in range(groups)]
    lax.fori_loop(0, n_tiles + 2, trip, (init, init), unroll=RELAYOUT_UNROLL)


def _ssm_kernel(u_ref, bb_ref, cc_ref, pw_ref, a_ref, d_ref, o_ref,
                lhs, yscr, m_scr, win_scr, wsw_scr, wct_scr, zin, zsw, st,
                *, n_chunks, n_batch):
    nblk = BLOCKS_PER_VREG
    sw = 2 * SSM_STATE
    rc = n_batch * n_chunks
    half = CHUNK // nblk

    def tok_rows(tile, tile_chunks, t):
        return pl.ds(tile * (tile_chunks * CHUNK) + t, tile_chunks, stride=CHUNK)

    lane256 = lax.broadcasted_iota(jnp.int32, (CHUNK, CHUNK_W), 1)
    for g in range(GROUPS_PER_STEP):
        bb, bbs, bsw, bsws = bb_ref[g, 0], bb_ref[g, 1], bb_ref[g, 2], bb_ref[g, 3]
        cc, ccs = cc_ref[g, 0], cc_ref[g, 1]
        c_pow = [pw_ref[g, 0, tau:tau + 1, :] * cc + pw_ref[g, 1, tau:tau + 1, :] * ccs
                 for tau in range(CHUNK + 1)]
        for t in range(CHUNK):
            rows = slice(t * CHUNK, (t + 1) * CHUNK)
            p_re = pw_ref[g, 0, CHUNK - 1 - t:CHUNK - t, :]
            p_im = pw_ref[g, 1, CHUNK - 1 - t:CHUNK - t, :]
            win_scr[g, rows, :] = (p_re * bb + p_im * bbs).astype(BF16)
            wsw_scr[g, rows, :] = (p_re * bsw + p_im * bsws).astype(BF16)
            wct_scr[g, rows, :] = c_pow[t + 1].astype(BF16)
        base = lax.dot_general(bb, jnp.concatenate(c_pow[:CHUNK], axis=0), (((1,), (1,)), ((), ())),
                               precision=lax.Precision.HIGHEST, preferred_element_type=F32)
        for sblk in range(CHUNK):
            row = base if sblk == 0 else jnp.where(
                lane256 >= sblk * SSM_GROUP_CH, pltpu.roll(base, sblk * SSM_GROUP_CH, axis=1), 0.0)
            m_scr[g, sblk * CHUNK:(sblk + 1) * CHUNK, :] = row.astype(BF16)

    rt_in = 16
    lane_blk_in = lax.broadcasted_iota(jnp.int32, (rt_in // 2, LANES), 1) // SSM_GROUP_CH

    def load_u(r):
        return [[pltpu.bitcast(u_ref[tok_rows(r, rt_in, hf * nblk + tb), :].astype(BF16), jnp.uint32)
                 for tb in range(nblk)] for hf in range(half)]

    def store_lhs(r, groups):
        rows = pl.ds(pl.multiple_of(r * rt_in, rt_in), rt_in)
        for hf in range(half):
            for g in range(nblk):
                col = g * CHUNK_W + hf * LANES
                lhs[rows, col:col + LANES] = pltpu.bitcast(groups[hf][g], BF16)

    _pipelined_block_transpose(rc // rt_in, load_u, store_lhs,
                               jnp.zeros((rt_in // 2, LANES), jnp.uint32), lane_blk_in)

    for g in range(GROUPS_PER_STEP):
        u_g = lhs[:, g * CHUNK_W:(g + 1) * CHUNK_W]
        zin[:, g * sw:(g + 1) * sw] = jnp.dot(u_g, win_scr[g], preferred_element_type=F32)
        zsw[:, g * sw:(g + 1) * sw] = jnp.dot(u_g, wsw_scr[g], preferred_element_type=F32)

    a1 = a_ref[0:1, :]
    a2 = a_ref[1:2, :]

    def step(c, carry):
        nxt = []
        for b in range(n_batch):
            s, x = carry[b]
            row = b * n_chunks + c
            st[pl.ds(row, 1), :] = s
            nxt.append((a1 * s + a2 * x + zin[pl.ds(row, 1), :],
                        a1 * x - a2 * s + zsw[pl.ds(row, 1), :]))
        return tuple(nxt)

    zero = jnp.zeros((1, GROUPS_PER_STEP * sw), F32)
    lax.fori_loop(0, n_chunks, step, tuple((zero, zero) for _ in range(n_batch)),
                  unroll=SCAN_UNROLL)

    for g in range(GROUPS_PER_STEP):
        cols = slice(g * CHUNK_W, (g + 1) * CHUNK_W)
        y = jnp.dot(lhs[:, cols], m_scr[g], preferred_element_type=F32)
        y = y + lax.dot_general(st[:, g * sw:(g + 1) * sw].astype(BF16), wct_scr[g],
                                (((1,), (1,)), ((), ())), preferred_element_type=F32)
        yscr[:, cols] = y.astype(BF16)

    d_row = d_ref[...]

    def load_y(r):
        rows = pl.ds(pl.multiple_of(r * rt_in, rt_in), rt_in)
        return [[pltpu.bitcast(yscr[rows, g * CHUNK_W + hf * LANES:g * CHUNK_W + (hf + 1) * LANES],
                               jnp.uint32)
                 for g in range(nblk)] for hf in range(half)]

    def store_tokens(r, groups):
        for hf in range(half):
            for tb in range(nblk):
                rows = tok_rows(r, rt_in, hf * nblk + tb)
                y = pltpu.bitcast(groups[hf][tb], BF16).astype(F32)
                o_ref[rows, :] = y + d_row * u_ref[rows, :]

    _pipelined_block_transpose(rc // rt_in, load_y, store_tokens,
                               jnp.zeros((rt_in // 2, LANES), jnp.uint32), lane_blk_in)


def _ssm(u_tok, bb, cc, pw, a16, d, n_batch):
    rows, width = u_tok.shape
    rc = rows // CHUNK
    n_chunks = rc // n_batch
    assert GROUPS_PER_STEP == BLOCKS_PER_VREG and rc % 16 == 0
    gs = GROUPS_PER_STEP
    bw = gs * CHUNK_W
    sw = gs * 2 * SSM_STATE
    kern = functools.partial(_ssm_kernel, n_chunks=n_chunks, n_batch=n_batch)
    grp4 = lambda j: (j, 0, 0, 0)
    return pl.pallas_call(
        kern,
        grid=(SSM_GROUPS // gs,),
        in_specs=[
            pl.BlockSpec((rows, LANES), lambda j: (0, j)),
            pl.BlockSpec((gs,) + bb.shape[1:], grp4),
            pl.BlockSpec((gs,) + cc.shape[1:], grp4),
            pl.BlockSpec((gs,) + pw.shape[1:], grp4),
            pl.BlockSpec((2, sw), lambda j: (0, j)),
            pl.BlockSpec((1, LANES), lambda j: (0, j)),
        ],
        out_specs=pl.BlockSpec((rows, LANES), lambda j: (0, j)),
        out_shape=jax.ShapeDtypeStruct((rows, width), F32),
        scratch_shapes=[
            pltpu.VMEM((rc, bw), BF16),
            pltpu.VMEM((rc, bw), BF16),
            pltpu.VMEM((gs, CHUNK_W, CHUNK_W), BF16),
            pltpu.VMEM((gs, CHUNK_W, 2 * SSM_STATE), BF16),
            pltpu.VMEM((gs, CHUNK_W, 2 * SSM_STATE), BF16),
            pltpu.VMEM((gs, CHUNK_W, 2 * SSM_STATE), BF16),
            pltpu.VMEM((rc, sw), F32),
            pltpu.VMEM((rc, sw), F32),
            pltpu.VMEM((rc, sw), F32),
        ],
        compiler_params=pltpu.CompilerParams(
            dimension_semantics=("parallel",), vmem_limit_bytes=VMEM_LIMIT),
        name="ssm",
    )(u_tok, bb, cc, pw, a16, d)


def _ssm_tables(lam_re, lam_im, log_step, b_re, b_im, c_re, c_im):
    depth = lam_re.shape[0]
    g, p, h, t = SSM_GROUPS, SSM_STATE, SSM_GROUP_CH, CHUNK
    lam = lax.complex(lam_re.astype(F32), lam_im.astype(F32))
    delta = jnp.exp(log_step.astype(F32))[..., None]
    ld = lam * delta
    lam_bar = jnp.exp(ld)
    b_bar = ((lam_bar - 1.0) / lam)[..., None] * lax.complex(b_re.astype(F32), b_im.astype(F32))
    c_c = lax.complex(c_re.astype(F32), c_im.astype(F32))
    pows = [jnp.ones_like(lam_bar), lam_bar]
    for _ in range(t - 1):
        pows.append(pows[-1] * lam_bar)
    powers = jnp.stack(pows, axis=2)

    cat = lambda x, y: jnp.concatenate([x, y], axis=-1)
    br, bi = jnp.real(b_bar).swapaxes(-1, -2), jnp.imag(b_bar).swapaxes(-1, -2)
    bb = jnp.stack([cat(br, bi), cat(-bi, br), cat(bi, br), cat(br, -bi)], axis=2)
    cr, ci = jnp.real(c_c), jnp.imag(c_c)
    cc = jnp.stack([cat(cr, -ci), cat(-ci, -cr)], axis=2)
    pr, pi = jnp.real(powers), jnp.imag(powers)
    pw = jnp.stack([cat(pr, pr), cat(pi, pi)], axis=2)
    pw = jnp.pad(pw, ((0, 0), (0, 0), (0, 0), (0, 24 - (t + 1)), (0, 0)))
    ar, ai = pr[:, :, t], pi[:, :, t]
    a16 = jnp.stack([cat(ar, ar).reshape(depth, g * 2 * p),
                     cat(-ai, ai).reshape(depth, g * 2 * p)], axis=1)
    return bb, cc, pw, a16


def _outproj_kernel(a_ref, g_ref, x_ref, wglu_f32, bglu_ref, sg_ref, wout_f32, ng_ref,
                    o_ref, xn_ref, wglu_ref, wout_ref):
    @pl.when(pl.program_id(0) == 0)
    def _():
        wglu_ref[...] = wglu_f32[...].astype(BF16)
        wout_ref[...] = wout_f32[...].astype(BF16)

    g = _gelu_tanh(g_ref[...])
    z = jnp.dot(g.astype(BF16), wglu_ref[...], preferred_element_type=F32) + bglu_ref[...]
    s = g * jax.nn.sigmoid(z)
    s_n = _rms(s, sg_ref[...]).astype(BF16)
    y = jnp.dot(a_ref[...], wout_ref[:ATTN_WIDTH, :], preferred_element_type=F32)
    y = y + jnp.dot(s_n, wout_ref[ATTN_WIDTH:, :], preferred_element_type=F32)
    h = x_ref[...] + y
    o_ref[...] = h
    xn_ref[...] = _rms(h, ng_ref[...]).astype(xn_ref.dtype)


def _outproj(attn_n, g_tok, x, w_glu_all, b_glu, ssm_gain, w_out_all, mlp_gain, layer):
    rows = x.shape[0]
    tm = _row_tile(rows, 384)
    row = lambda i: (i, 0)
    fixed = lambda i: (0, 0)
    return pl.pallas_call(
        _outproj_kernel,
        grid=(rows // tm,),
        in_specs=[
            pl.BlockSpec((tm, ATTN_WIDTH), row),
            pl.BlockSpec((tm, SSM_WIDTH), row),
            pl.BlockSpec((tm, D_MODEL), row),
            pl.BlockSpec((None, SSM_WIDTH, SSM_WIDTH), lambda i: (layer, 0, 0)),
            pl.BlockSpec((1, SSM_WIDTH), fixed),
            pl.BlockSpec((1, SSM_WIDTH), fixed),
            pl.BlockSpec((None, D_MODEL, D_MODEL), lambda i: (layer, 0, 0)),
            pl.BlockSpec((1, D_MODEL), fixed),
        ],
        out_specs=[pl.BlockSpec((tm, D_MODEL), row), pl.BlockSpec((tm, D_MODEL), row)],
        out_shape=[jax.ShapeDtypeStruct((rows, D_MODEL), F32),
                   jax.ShapeDtypeStruct((rows, D_MODEL), BF16)],
        scratch_shapes=[pltpu.VMEM((SSM_WIDTH, SSM_WIDTH), BF16), pltpu.VMEM((D_MODEL, D_MODEL), BF16)],
        compiler_params=pltpu.CompilerParams(
            dimension_semantics=("arbitrary",), vmem_limit_bytes=BIG_VMEM_LIMIT),
        name="outproj",
    )(attn_n, g_tok, x, w_glu_all, b_glu, ssm_gain, w_out_all, mlp_gain)


def _mlp_tile(xn, w_up, w_down, o_ref, rows):
    a = jnp.dot(xn, w_up, preferred_element_type=F32)
    a = jnp.square(jnp.maximum(a, 0.0)).astype(BF16)
    o_ref[rows, :] += jnp.dot(a, w_down, preferred_element_type=F32)


def _mlp_first_kernel(xn_ref, x_hbm, wu_ref, wd_ref, o_ref, wu_bf_ref, wd_bf_ref, sem, *, tm):
    @pl.when(pl.program_id(0) == 0)
    def _():
        copy = pltpu.make_async_copy(x_hbm.at[pl.ds(0, o_ref.shape[0])], o_ref, sem)
        copy.start()
        copy.wait()

    wu_bf_ref[...] = wu_ref[...].astype(BF16)
    wd_bf_ref[...] = wd_ref[...].astype(BF16)
    for t in range(o_ref.shape[0] // tm):
        rows = slice(t * tm, (t + 1) * tm)
        _mlp_tile(xn_ref[rows, :], wu_bf_ref[...], wd_bf_ref[...], o_ref, rows)


def _row_pieces(lo, hi, tokens):
    if tokens is None:
        return [(0, hi - lo, lo)]
    lp, seq, n_seq = tokens
    pieces = []
    for b in range(n_seq):
        first, last = max(lo, b * lp + N_META), min(hi, b * lp + N_META + seq)
        if last > first:
            pieces.append((first - lo, last - first, b * seq + first - (b * lp + N_META)))
    return pieces


def _mlp_rest_kernel(xn_ref, x_ref, wu_ref, wd_ref, first_hbm, out_hbm, acc_even, acc_odd, sem,
                     *, first_pieces, tile_pieces):
    i = pl.program_id(0)
    f = pl.program_id(1)
    n_tiles = len(tile_pieces)
    last_f = pl.num_programs(1) - 1
    width = max(len(p) for p in tile_pieces)

    first_copies = [pltpu.make_async_copy(first_hbm.at[pl.ds(src, n)], out_hbm.at[pl.ds(dst, n)],
                                          sem.at[k])
                    for k, (src, n, dst) in enumerate(first_pieces)]

    def tile_copies(tile, acc):
        base = len(first_pieces) + (tile % 2) * width
        return [pltpu.make_async_copy(acc.at[pl.ds(src, n)], out_hbm.at[pl.ds(dst, n)],
                                      sem.at[base + k])
                for k, (src, n, dst) in enumerate(tile_pieces[tile])]

    @pl.when((i == 0) & (f == 0))
    def _():
        for c in first_copies:
            c.start()

    for parity, acc in ((0, acc_even), (1, acc_odd)):
        @pl.when(i % 2 == parity)
        def _(parity=parity, acc=acc):
            @pl.when(f == 0)
            def _():
                for tile in range(parity + 2, n_tiles, 2):
                    @pl.when(i == tile)
                    def _(tile=tile):
                        for c in tile_copies(tile - 2, acc):
                            c.wait()
                acc[...] = x_ref[...]

            _mlp_tile(xn_ref[...], wu_ref[...], wd_ref[...], acc, slice(None))

            for tile in range(parity, n_tiles, 2):
                @pl.when((f == last_f) & (i == tile))
                def _(tile=tile):
                    for c in tile_copies(tile, acc):
                        c.start()

    @pl.when((i == n_tiles - 1) & (f == last_f))
    def _():
        for tile in range(max(n_tiles - 2, 0), n_tiles):
            for c in tile_copies(tile, acc_odd if tile % 2 else acc_even):
                c.wait()
        for c in first_copies:
            c.wait()


def _mlp(xn, x, w_up_all, w_down_all, layer, tokens=None):
    rows = x.shape[0]
    tm = _row_tile(rows, MLP_ROWS)
    skip = min(MLP_TILES_FIRST, rows // tm)
    tm_first = skip * tm
    tf_first, tf = MLP_FF_TILE_FIRST, MLP_FF_TILE
    first_rows = lambda f: (0, 0)
    first, w_up_bf, w_down_bf = pl.pallas_call(
        functools.partial(_mlp_first_kernel, tm=tm),
        grid=(D_FF // tf_first,),
        in_specs=[
            pl.BlockSpec((tm_first, D_MODEL), first_rows),
            pl.BlockSpec(memory_space=pl.ANY),
            pl.BlockSpec((None, D_MODEL, tf_first), lambda f: (layer, 0, f)),
            pl.BlockSpec((None, tf_first, D_MODEL), lambda f: (layer, f, 0)),
        ],
        out_specs=[pl.BlockSpec((tm_first, D_MODEL), first_rows),
                   pl.BlockSpec((D_MODEL, tf_first), lambda f: (0, f)),
                   pl.BlockSpec((tf_first, D_MODEL), lambda f: (f, 0))],
        out_shape=[jax.ShapeDtypeStruct((tm_first, D_MODEL), F32),
                   jax.ShapeDtypeStruct((D_MODEL, D_FF), BF16),
                   jax.ShapeDtypeStruct((D_FF, D_MODEL), BF16)],
        scratch_shapes=[pltpu.SemaphoreType.DMA(())],
        compiler_params=pltpu.CompilerParams(
            dimension_semantics=("arbitrary",), vmem_limit_bytes=BIG_VMEM_LIMIT),
        name="mlp_first",
    )(xn, x, w_up_all, w_down_all)
    n_rest = rows // tm - skip
    if n_rest == 0:
        if tokens is None:
            return first
        lp, seq, n_seq = tokens
        return first.reshape(n_seq, lp, D_MODEL)[:, N_META:N_META + seq].reshape(n_seq * seq, D_MODEL)
    first_pieces = _row_pieces(0, tm_first, tokens)
    tile_pieces = [_row_pieces((skip + t) * tm, (skip + t + 1) * tm, tokens) for t in range(n_rest)]
    out_rows = rows if tokens is None else tokens[1] * tokens[2]
    n_sems = len(first_pieces) + 2 * max(len(p) for p in tile_pieces)
    return pl.pallas_call(
        functools.partial(_mlp_rest_kernel, first_pieces=first_pieces, tile_pieces=tile_pieces),
        grid=(n_rest, D_FF // tf),
        in_specs=[
            pl.BlockSpec((tm, D_MODEL), lambda i, f: (i + skip, 0)),
            pl.BlockSpec((tm, D_MODEL), lambda i, f: (i + skip, 0)),
            pl.BlockSpec((D_MODEL, tf), lambda i, f: (0, f)),
            pl.BlockSpec((tf, D_MODEL), lambda i, f: (f, 0)),
            pl.BlockSpec(memory_space=pl.ANY),
        ],
        out_specs=pl.BlockSpec(memory_space=pl.ANY),
        out_shape=jax.ShapeDtypeStruct((out_rows, D_MODEL), F32),
        scratch_shapes=[pltpu.VMEM((tm, D_MODEL), F32), pltpu.VMEM((tm, D_MODEL), F32),
                        pltpu.SemaphoreType.DMA((n_sems,))],
        compiler_params=pltpu.CompilerParams(
            dimension_semantics=("arbitrary", "arbitrary"), vmem_limit_bytes=BIG_VMEM_LIMIT),
        name="mlp_rest",
    )(xn, x, w_up_bf, w_down_bf, first)


def kernel(x, meta_tokens, norm_mix_g, w_in, q_norm_g, k_norm_g, attn_sinks, ssm_lambda_re, ssm_lambda_im, ssm_log_step, ssm_b_re, ssm_b_im, ssm_c_re, ssm_c_im, ssm_d, w_glu, b_glu, attn_out_g, ssm_out_g, w_out, norm_mlp_g, w_up, w_down):
    b, seq, d = x.shape
    depth = w_in.shape[0]
    assert d == D_MODEL
    length = N_META + seq
    lp = -(-length // TILE) * TILE
    rows = b * lp

    meta = jnp.broadcast_to(meta_tokens.astype(x.dtype)[None], (b, N_META, d))
    pad = jnp.zeros((b, lp - length, d), x.dtype)
    h_res = jnp.concatenate([meta, x, pad], axis=1).reshape(rows, d)

    bb, cc, pw, a16 = _ssm_tables(
        ssm_lambda_re, ssm_lambda_im, ssm_log_step, ssm_b_re, ssm_b_im, ssm_c_re, ssm_c_im)

    head_id = jnp.arange(KV_WIDTH) // HEAD_DIM
    head_ones = (head_id[:, None] == head_id[None, :]).astype(BF16)
    slopes = jnp.exp2(-8.0 * jnp.arange(1, N_HEADS + 1, dtype=F32) / N_HEADS)
    attn_bias, attn_off = _attn_bias(slopes)

    row_vec = lambda v: v.astype(F32).reshape(1, -1)
    for l in range(depth):
        q_gain = jnp.tile(row_vec(q_norm_g[l]), (1, N_KV_HEADS)) * (LOG2E / math.sqrt(HEAD_DIM))
        k_gain = jnp.tile(row_vec(k_norm_g[l]), (1, N_KV_HEADS))
        sink_row = jnp.broadcast_to(
            (attn_sinks[l].astype(F32) * LOG2E).reshape(N_KV_HEADS, 1, KV_GROUP, 1),
            (N_KV_HEADS, 1, KV_GROUP, SUB)).reshape(N_KV_HEADS, 1, KV_GROUP * SUB)
        u_tok, attn_n = _inproj_attention(
            h_res, row_vec(norm_mix_g[l]), w_in, l, head_ones, q_gain, k_gain, attn_bias, attn_off,
            sink_row, row_vec(attn_out_g[l]), b)
        g_tok = _ssm(u_tok, bb[l], cc[l], pw[l], a16[l], row_vec(ssm_d[l]), b)
        h_res, xn = _outproj(attn_n, g_tok, h_res, w_glu,
                             row_vec(b_glu[l]), row_vec(ssm_out_g[l]), w_out,
                             row_vec(norm_mlp_g[l]), l)
        h_res = _mlp(xn, h_res, w_up, w_down, l, (lp, seq, b) if l == depth - 1 else None)
    return h_res.reshape(b, seq, d)
```

```python
import functools
import math

import jax
import jax.numpy as jnp
from jax import lax
from jax.experimental import pallas as pl
from jax.experimental.pallas import tpu as pltpu

D_MODEL = 2048
N_META = 16
HEAD_DIM = 64
ATTN_WIDTH = D_MODEL // 2
N_HEADS = ATTN_WIDTH // HEAD_DIM
N_KV_HEADS = N_HEADS // 4
KV_GROUP = N_HEADS // N_KV_HEADS
KV_WIDTH = N_KV_HEADS * HEAD_DIM
SSM_WIDTH = D_MODEL - ATTN_WIDTH
SSM_GROUP_CH = 16
SSM_GROUPS = SSM_WIDTH // SSM_GROUP_CH
SSM_STATE = 64
WINDOW = 128
BLOCK = 128
D_FF = 4 * D_MODEL
IN_WIDTH = ATTN_WIDTH + 2 * KV_WIDTH + SSM_WIDTH
NORM_EPS = 1e-6
NEG_INF = -1e30
LOG2E = math.log2(math.e)

CHUNK = 16
CHUNK_W = CHUNK * SSM_GROUP_CH
GROUPS_PER_STEP = 8
SCAN_UNROLL = 8
RELAYOUT_UNROLL = True

QKV_WIDTH = ATTN_WIDTH + 2 * KV_WIDTH
COL_K = ATTN_WIDTH
COL_V = ATTN_WIDTH + KV_WIDTH
LANES = 128
BLOCKS_PER_VREG = LANES // SSM_GROUP_CH

VMEM_LIMIT = 48 * 1024 * 1024
MLP_ROWS = 768
MLP_FF_TILE = 1024
MLP_TILES_FIRST = 2
MLP_FF_TILE_FIRST = 512
BIG_VMEM_LIMIT = 56 * 1024 * 1024

F32 = jnp.float32
BF16 = jnp.bfloat16


def _row_tile(rows, target):
    k = rows // BLOCK
    best = 1
    for d in range(1, k + 1):
        if k % d == 0 and d * BLOCK <= target:
            best = d
    return best * BLOCK


def _rms(x, gain):
    ms = jnp.mean(x * x, axis=-1, keepdims=True)
    return x * lax.rsqrt(ms + NORM_EPS) * gain


def _head_norm(x, head_ones, gain):
    x2 = x * x
    hi = x2.astype(BF16)
    lo = (x2 - hi.astype(F32)).astype(BF16)
    ssq = (jnp.dot(hi, head_ones, preferred_element_type=F32)
           + jnp.dot(lo, head_ones, preferred_element_type=F32))
    return x * lax.rsqrt(ssq * (1.0 / HEAD_DIM) + NORM_EPS) * gain


SUB = 64
BAND = WINDOW + SUB
KEYS = BAND + N_META
TILE = 3 * BLOCK


def _attn_bias(slopes):
    j = jnp.arange(KEYS)[:, None]
    i = jnp.arange(SUB)[None, :]
    is_band = j < BAND
    dist = WINDOW + i - j
    sl = slopes.reshape(N_KV_HEADS, 1, KV_GROUP, 1)
    dist_f = dist.astype(F32)[None, :, None, :]
    band = jnp.where(((dist >= 0) & (dist < WINDOW))[None, :, None, :], -sl * dist_f, NEG_INF)
    meta = -sl * (i - (j - BAND)).astype(F32)[None, :, None, :]
    bias = jnp.where(is_band[None, :, None, :], band, meta) * LOG2E
    lanes = KV_GROUP * SUB
    off_coef = jnp.broadcast_to(-sl * LOG2E, (N_KV_HEADS, N_META, KV_GROUP, SUB))
    return bias.reshape(N_KV_HEADS, KEYS, lanes), off_coef.reshape(N_KV_HEADS, N_META, lanes)


def _project_tile(x_ref, g_ref, w_bf, ones_ref, qg_ref, kg_ref, qkv_out, u_ref):
    h = _rms(x_ref[...], g_ref[...]).astype(BF16)
    proj = jnp.dot(h, w_bf[...], preferred_element_type=F32)
    head_ones = ones_ref[...]
    for blk in range(ATTN_WIDTH // KV_WIDTH):
        cols = slice(blk * KV_WIDTH, (blk + 1) * KV_WIDTH)
        qkv_out[:, cols] = _head_norm(proj[:, cols], head_ones, qg_ref[...]).astype(BF16)
    qkv_out[:, COL_K:COL_V] = _head_norm(proj[:, COL_K:COL_V], head_ones, kg_ref[...]).astype(BF16)
    qkv_out[:, COL_V:QKV_WIDTH] = proj[:, COL_V:QKV_WIDTH].astype(BF16)
    u_ref[...] = proj[:, QKV_WIDTH:]


def _attend_tile(qkv, tail, meta, bias_ref, offc_ref, sink_ref, og_ref, o_ref, acc, tile_pos, masked):
    lanes = KV_GROUP * SUB
    ones = jnp.ones((KEYS, 8), BF16)
    for blk in range(TILE // BLOCK):
        chains = [(sb, kh) for sb in range(BLOCK // SUB) for kh in range(N_KV_HEADS)]
        scores, values = [], []
        for sb, kh in chains:
            row0 = blk * BLOCK + sb * SUB
            start = tile_pos + row0
            first = row0 - WINDOW

            def window(col):
                cols = slice(col + kh * HEAD_DIM, col + (kh + 1) * HEAD_DIM)
                tcols = slice(cols.start - COL_K, cols.stop - COL_K)
                if first >= 0:
                    band = qkv[first:first + BAND, cols]
                else:
                    band = jnp.concatenate([tail[WINDOW + first:WINDOW, tcols],
                                            qkv[0:BAND + first, cols]], axis=0)
                return jnp.concatenate([band, meta[:, tcols]], axis=0)

            k_win = window(COL_K)
            values.append(jnp.concatenate([window(COL_V), ones], axis=1))
            q_s = jnp.concatenate(
                [qkv[row0:row0 + SUB, (kh * KV_GROUP + g) * HEAD_DIM:(kh * KV_GROUP + g + 1) * HEAD_DIM]
                 for g in range(KV_GROUP)], axis=0)
            s = lax.dot_general(k_win, q_s, (((1,), (1,)), ((), ())),
                                preferred_element_type=F32)
            s = s + bias_ref[kh]
            s = jnp.concatenate([s[:BAND], s[BAND:] + offc_ref[kh] * start.astype(F32)], axis=0)
            if masked:
                j = lax.broadcasted_iota(jnp.int32, (KEYS, lanes), 0)
                i = lax.broadcasted_iota(jnp.int32, (KEYS, lanes), 1) % SUB
                ok = jnp.where(j < BAND, start - WINDOW + j - N_META, start + i - (j - BAND)) >= 0
                s = jnp.where(ok, s, NEG_INF)
            scores.append(s)
        maxes = [jnp.maximum(jnp.max(s, axis=0, keepdims=True), sink_ref[kh])
                 for s, (sb, kh) in zip(scores, chains)]
        probs = [jnp.exp2(s - m).astype(BF16) for s, m in zip(scores, maxes)]
        for (sb, kh), p, m, v_ext in zip(chains, probs, maxes, values):
            o_ext = lax.dot_general(v_ext, p, (((0,), (0,)), ((), ())),
                                    preferred_element_type=F32)
            denom = o_ext[HEAD_DIM:HEAD_DIM + 1] + jnp.exp2(sink_ref[kh] - m)
            o = o_ext[:HEAD_DIM] / denom
            for g in range(KV_GROUP):
                h = kh * KV_GROUP + g
                acc[h * HEAD_DIM:(h + 1) * HEAD_DIM, sb * SUB:(sb + 1) * SUB] = o[:, g * SUB:(g + 1) * SUB]
        a_t = acc[...]
        ms = jnp.mean(a_t * a_t, axis=0, keepdims=True)
        a_n = a_t * lax.rsqrt(ms + NORM_EPS)
        o_ref[blk * BLOCK:(blk + 1) * BLOCK, :] = (a_n.T * og_ref[...]).astype(o_ref.dtype)


def _inproj_attn_kernel(*refs, tiles_per_seq, n_tiles, embed):
    if embed is None:
        (x_ref, g_ref, w_ref, ones_ref, qg_ref, kg_ref, bias_ref, offc_ref, sink_ref, og_ref,
         u_ref, attn_ref, w_bf, ring, tail, meta, acc) = refs
    else:
        (x_hbm, meta_tok_ref, g_ref, w_ref, ones_ref, qg_ref, kg_ref, bias_ref, offc_ref,
         sink_ref, og_ref, u_ref, attn_ref, h0_hbm, w_bf, ring, tail, meta, acc, xbuf,
         xsem) = refs
    s = pl.program_id(0)

    @pl.when(s == 0)
    def _():
        w_bf[...] = w_ref[...].astype(BF16)
        ring[...] = jnp.zeros_like(ring)
        tail[...] = jnp.zeros_like(tail)
        meta[...] = jnp.zeros_like(meta)

    t_proj = jnp.minimum(s, n_tiles - 1)
    if embed is not None:
        seq, n_seq = embed
        lp = tiles_per_seq * TILE

        def token_copies(tile):
            slot = tile % 2
            return [pltpu.make_async_copy(x_hbm.at[pl.ds(dst, n)], xbuf.at[slot, pl.ds(src, n)],
                                          xsem.at[2 * slot + k])
                    for k, (src, n, dst) in enumerate(_row_pieces(tile * TILE, (tile + 1) * TILE,
                                                                  (lp, seq, n_seq)))]

        def stream_copy(tile):
            return pltpu.make_async_copy(xbuf.at[tile % 2], h0_hbm.at[pl.ds(tile * TILE, TILE)],
                                         xsem.at[4 + tile % 2])

        for tile in range(n_tiles):
            @pl.when(s == tile)
            def _(tile=tile):
                if tile == 0:
                    for c in token_copies(0):
                        c.start()
                for c in token_copies(tile):
                    c.wait()
                first_row = (tile % tiles_per_seq) * TILE
                if first_row == 0:
                    xbuf[tile % 2, 0:N_META, :] = meta_tok_ref[...]
                pad_from = N_META + seq - first_row
                if pad_from < TILE:
                    xbuf[tile % 2, max(pad_from, 0):TILE, :] = jnp.zeros(
                        (TILE - max(pad_from, 0), D_MODEL), F32)
                if tile >= 1:
                    stream_copy(tile - 1).wait()
                if tile + 1 < n_tiles:
                    for c in token_copies(tile + 1):
                        c.start()
                stream_copy(tile).start()

        @pl.when(s == n_tiles)
        def _():
            stream_copy(n_tiles - 1).wait()

        x_ref = xbuf.at[t_proj % 2]

    t_attn = jnp.maximum(s - 1, 0)
    seq_tile = t_attn % tiles_per_seq
    seq_slot = (t_attn // tiles_per_seq) % 2
    cur = (s + 1) % 2
    nxt = s % 2

    def step(masked):
        _attend_tile(ring.at[cur], tail, meta.at[seq_slot], bias_ref, offc_ref, sink_ref, og_ref,
                     attn_ref, acc, seq_tile * TILE, masked)
        tail[...] = ring[cur, TILE - WINDOW:TILE, COL_K:QKV_WIDTH]
        _project_tile(x_ref, g_ref, w_bf, ones_ref, qg_ref, kg_ref, ring.at[nxt], u_ref)

    @pl.when(seq_tile != 0)
    def _():
        step(False)

    @pl.when(seq_tile == 0)
    def _():
        step(True)

    @pl.when(t_proj % tiles_per_seq == 0)
    def _():
        meta[(t_proj // tiles_per_seq) % 2] = ring[nxt, 0:N_META, COL_K:QKV_WIDTH]


def _inproj_attention(x, gain, w_all, layer, head_ones, q_gain, k_gain, bias, off_coef, sink_row,
                      out_gain, n_seq, rows, meta_tokens=None):
    n_tiles = rows // TILE
    tiles_per_seq = n_tiles // n_seq
    assert rows == n_seq * tiles_per_seq * TILE
    embed = None if meta_tokens is None else (x.shape[0] // n_seq, n_seq)
    kern = functools.partial(_inproj_attn_kernel, tiles_per_seq=tiles_per_seq, n_tiles=n_tiles,
                             embed=embed)
    fixed = lambda s: (0, 0)
    fixed3 = lambda s: (0, 0, 0)
    proj_tile = lambda s: (jnp.minimum(s, n_tiles - 1), 0)
    if embed is None:
        x_specs, x_args = [pl.BlockSpec((TILE, D_MODEL), proj_tile)], [x]
    else:
        x_specs = [pl.BlockSpec(memory_space=pl.ANY), pl.BlockSpec((N_META, D_MODEL), fixed)]
        x_args = [x, meta_tokens]
    out_specs = [pl.BlockSpec((TILE, SSM_WIDTH), proj_tile),
                 pl.BlockSpec((TILE, ATTN_WIDTH), lambda s: (jnp.maximum(s - 1, 0), 0))]
    out_shape = [jax.ShapeDtypeStruct((rows, SSM_WIDTH), F32),
                 jax.ShapeDtypeStruct((rows, ATTN_WIDTH), BF16)]
    scratch = [
        pltpu.VMEM((D_MODEL, IN_WIDTH), BF16),
        pltpu.VMEM((2, TILE, QKV_WIDTH), BF16),
        pltpu.VMEM((WINDOW, 2 * KV_WIDTH), BF16),
        pltpu.VMEM((2, N_META, 2 * KV_WIDTH), BF16),
        pltpu.VMEM((ATTN_WIDTH, BLOCK), F32),
    ]
    if embed is not None:
        out_specs.append(pl.BlockSpec(memory_space=pl.ANY))
        out_shape.append(jax.ShapeDtypeStruct((rows, D_MODEL), F32))
        scratch += [pltpu.VMEM((2, TILE, D_MODEL), F32),
                    pltpu.SemaphoreType.DMA((6,))]
    return pl.pallas_call(
        kern,
        grid=(n_tiles + 1,),
        in_specs=x_specs + [
            pl.BlockSpec((1, D_MODEL), fixed),
            pl.BlockSpec((None, D_MODEL, IN_WIDTH), lambda s: (layer, 0, 0)),
            pl.BlockSpec((KV_WIDTH, KV_WIDTH), fixed),
            pl.BlockSpec((1, KV_WIDTH), fixed),
            pl.BlockSpec((1, KV_WIDTH), fixed),
            pl.BlockSpec(bias.shape, fixed3),
            pl.BlockSpec(off_coef.shape, fixed3),
            pl.BlockSpec(sink_row.shape, fixed3),
            pl.BlockSpec((1, ATTN_WIDTH), fixed),
        ],
        out_specs=out_specs,
        out_shape=out_shape,
        scratch_shapes=scratch,
        compiler_params=pltpu.CompilerParams(
            dimension_semantics=("arbitrary",), vmem_limit_bytes=BIG_VMEM_LIMIT),
        name="inproj_attn",
    )(*x_args, gain, w_all, head_ones, q_gain, k_gain, bias, off_coef, sink_row, out_gain)


def _gelu_tanh(x):
    c = math.sqrt(2.0 / math.pi)
    return 0.5 * x * (1.0 + jnp.tanh(c * (x + 0.044715 * (x * x * x))))


def _block_transpose_stage(arrs, dist, lane_blk):
    nblk = BLOCKS_PER_VREG
    keep = (lane_blk & dist) == 0
    nxt = list(arrs)
    for i in range(nblk):
        if i & dist == 0:
            lo, hi = arrs[i], arrs[i + dist]
            nxt[i] = jnp.where(keep, lo, pltpu.roll(hi, dist * SSM_GROUP_CH, axis=1))
            nxt[i + dist] = jnp.where(keep, pltpu.roll(lo, (nblk - dist) * SSM_GROUP_CH, axis=1), hi)
    return nxt


def _pipelined_block_transpose(n_tiles, load, store, zero, lane_blk):
    def trip(r, carry):
        after1, after2 = carry
        fresh = load(jnp.minimum(r, n_tiles - 1))
        new1 = [_block_transpose_stage(g, 4, lane_blk) for g in fresh]
        new2 = [_block_transpose_stage(g, 2, lane_blk) for g in after1]
        done = [_block_transpose_stage(g, 1, lane_blk) for g in after2]
        store(jnp.maximum(r - 2, 0), done)
        return new1, new2

    groups = CHUNK // BLOCKS_PER_VREG
    init = [[zero] * BLOCKS_PER_VREG for _ in range(groups)]
    lax.fori_loop(0, n_tiles + 2, trip, (init, init), unroll=RELAYOUT_UNROLL)


def _ssm_kernel(u_ref, bb_ref, cc_ref, pw_ref, a_ref, d_ref, o_ref,
                lhs, yscr, m_scr, win_scr, wsw_scr, wct_scr, zin, zsw, st,
                *, n_chunks, n_batch):
    nblk = BLOCKS_PER_VREG
    sw = 2 * SSM_STATE
    rc = n_batch * n_chunks
    half = CHUNK // nblk

    def tok_rows(tile, tile_chunks, t):
        return pl.ds(tile * (tile_chunks * CHUNK) + t, tile_chunks, stride=CHUNK)

    lane256 = lax.broadcasted_iota(jnp.int32, (CHUNK, CHUNK_W), 1)
    for g in range(GROUPS_PER_STEP):
        bb, bbs, bsw, bsws = bb_ref[g, 0], bb_ref[g, 1], bb_ref[g, 2], bb_ref[g, 3]
        cc, ccs = cc_ref[g, 0], cc_ref[g, 1]
        c_pow = [pw_ref[g, 0, tau:tau + 1, :] * cc + pw_ref[g, 1, tau:tau + 1, :] * ccs
                 for tau in range(CHUNK + 1)]
        for t in range(CHUNK):
            rows = slice(t * CHUNK, (t + 1) * CHUNK)
            p_re = pw_ref[g, 0, CHUNK - 1 - t:CHUNK - t, :]
            p_im = pw_ref[g, 1, CHUNK - 1 - t:CHUNK - t, :]
            win_scr[g, rows, :] = (p_re * bb + p_im * bbs).astype(BF16)
            wsw_scr[g, rows, :] = (p_re * bsw + p_im * bsws).astype(BF16)
            wct_scr[g, rows, :] = c_pow[t + 1].astype(BF16)
        base = lax.dot_general(bb, jnp.concatenate(c_pow[:CHUNK], axis=0), (((1,), (1,)), ((), ())),
                               precision=lax.Precision.HIGHEST, preferred_element_type=F32)
        for sblk in range(CHUNK):
            row = base if sblk == 0 else jnp.where(
                lane256 >= sblk * SSM_GROUP_CH, pltpu.roll(base, sblk * SSM_GROUP_CH, axis=1), 0.0)
            m_scr[g, sblk * CHUNK:(sblk + 1) * CHUNK, :] = row.astype(BF16)

    rt_in = 16
    lane_blk_in = lax.broadcasted_iota(jnp.int32, (rt_in // 2, LANES), 1) // SSM_GROUP_CH

    def load_u(r):
        return [[pltpu.bitcast(u_ref[tok_rows(r, rt_in, hf * nblk + tb), :].astype(BF16), jnp.uint32)
                 for tb in range(nblk)] for hf in range(half)]

    def store_lhs(r, groups):
        rows = pl.ds(pl.multiple_of(r * rt_in, rt_in), rt_in)
        for hf in range(half):
            for g in range(nblk):
                col = g * CHUNK_W + hf * LANES
                lhs[rows, col:col + LANES] = pltpu.bitcast(groups[hf][g], BF16)

    _pipelined_block_transpose(rc // rt_in, load_u, store_lhs,
                               jnp.zeros((rt_in // 2, LANES), jnp.uint32), lane_blk_in)

    for g in range(GROUPS_PER_STEP):
        u_g = lhs[:, g * CHUNK_W:(g + 1) * CHUNK_W]
        zin[:, g * sw:(g + 1) * sw] = jnp.dot(u_g, win_scr[g], preferred_element_type=F32)
        zsw[:, g * sw:(g + 1) * sw] = jnp.dot(u_g, wsw_scr[g], preferred_element_type=F32)

    a1 = a_ref[0:1, :]
    a2 = a_ref[1:2, :]

    def step(c, carry):
        nxt = []
        for b in range(n_batch):
            s, x = carry[b]
            row = b * n_chunks + c
            st[pl.ds(row, 1), :] = s
            nxt.append((a1 * s + a2 * x + zin[pl.ds(row, 1), :],
                        a1 * x - a2 * s + zsw[pl.ds(row, 1), :]))
        return tuple(nxt)

    zero = jnp.zeros((1, GROUPS_PER_STEP * sw), F32)
    lax.fori_loop(0, n_chunks, step, tuple((zero, zero) for _ in range(n_batch)),
                  unroll=SCAN_UNROLL)

    for g in range(GROUPS_PER_STEP):
        cols = slice(g * CHUNK_W, (g + 1) * CHUNK_W)
        y = jnp.dot(lhs[:, cols], m_scr[g], preferred_element_type=F32)
        y = y + lax.dot_general(st[:, g * sw:(g + 1) * sw].astype(BF16), wct_scr[g],
                                (((1,), (1,)), ((), ())), preferred_element_type=F32)
        yscr[:, cols] = y.astype(BF16)

    d_row = d_ref[...]

    def load_y(r):
        rows = pl.ds(pl.multiple_of(r * rt_in, rt_in), rt_in)
        return [[pltpu.bitcast(yscr[rows, g * CHUNK_W + hf * LANES:g * CHUNK_W + (hf + 1) * LANES],
                               jnp.uint32)
                 for g in range(nblk)] for hf in range(half)]

    def store_tokens(r, groups):
        for hf in range(half):
            for tb in range(nblk):
                rows = tok_rows(r, rt_in, hf * nblk + tb)
                y = pltpu.bitcast(groups[hf][tb], BF16).astype(F32)
                o_ref[rows, :] = y + d_row * u_ref[rows, :]

    _pipelined_block_transpose(rc // rt_in, load_y, store_tokens,
                               jnp.zeros((rt_in // 2, LANES), jnp.uint32), lane_blk_in)


def _ssm(u_tok, bb, cc, pw, a16, d, n_batch):
    rows, width = u_tok.shape
    rc = rows // CHUNK
    n_chunks = rc // n_batch
    assert GROUPS_PER_STEP == BLOCKS_PER_VREG and rc % 16 == 0
    gs = GROUPS_PER_STEP
    bw = gs * CHUNK_W
    sw = gs * 2 * SSM_STATE
    kern = functools.partial(_ssm_kernel, n_chunks=n_chunks, n_batch=n_batch)
    grp4 = lambda j: (j, 0, 0, 0)
    return pl.pallas_call(
        kern,
        grid=(SSM_GROUPS // gs,),
        in_specs=[
            pl.BlockSpec((rows, LANES), lambda j: (0, j)),
            pl.BlockSpec((gs,) + bb.shape[1:], grp4),
            pl.BlockSpec((gs,) + cc.shape[1:], grp4),
            pl.BlockSpec((gs,) + pw.shape[1:], grp4),
            pl.BlockSpec((2, sw), lambda j: (0, j)),
            pl.BlockSpec((1, LANES), lambda j: (0, j)),
        ],
        out_specs=pl.BlockSpec((rows, LANES), lambda j: (0, j)),
        out_shape=jax.ShapeDtypeStruct((rows, width), F32),
        scratch_shapes=[
            pltpu.VMEM((rc, bw), BF16),
            pltpu.VMEM((rc, bw), BF16),
            pltpu.VMEM((gs, CHUNK_W, CHUNK_W), BF16),
            pltpu.VMEM((gs, CHUNK_W, 2 * SSM_STATE), BF16),
            pltpu.VMEM((gs, CHUNK_W, 2 * SSM_STATE), BF16),
            pltpu.VMEM((gs, CHUNK_W, 2 * SSM_STATE), BF16),
            pltpu.VMEM((rc, sw), F32),
            pltpu.VMEM((rc, sw), F32),
            pltpu.VMEM((rc, sw), F32),
        ],
        compiler_params=pltpu.CompilerParams(
            dimension_semantics=("parallel",), vmem_limit_bytes=VMEM_LIMIT),
        name="ssm",
    )(u_tok, bb, cc, pw, a16, d)


def _ssm_tables(lam_re, lam_im, log_step, b_re, b_im, c_re, c_im):
    depth = lam_re.shape[0]
    g, p, h, t = SSM_GROUPS, SSM_STATE, SSM_GROUP_CH, CHUNK
    lam = lax.complex(lam_re.astype(F32), lam_im.astype(F32))
    delta = jnp.exp(log_step.astype(F32))[..., None]
    ld = lam * delta
    lam_bar = jnp.exp(ld)
    b_bar = ((lam_bar - 1.0) / lam)[..., None] * lax.complex(b_re.astype(F32), b_im.astype(F32))
    c_c = lax.complex(c_re.astype(F32), c_im.astype(F32))
    pows = [jnp.ones_like(lam_bar), lam_bar]
    for _ in range(t - 1):
        pows.append(pows[-1] * lam_bar)
    powers = jnp.stack(pows, axis=2)

    cat = lambda x, y: jnp.concatenate([x, y], axis=-1)
    br, bi = jnp.real(b_bar).swapaxes(-1, -2), jnp.imag(b_bar).swapaxes(-1, -2)
    bb = jnp.stack([cat(br, bi), cat(-bi, br), cat(bi, br), cat(br, -bi)], axis=2)
    cr, ci = jnp.real(c_c), jnp.imag(c_c)
    cc = jnp.stack([cat(cr, -ci), cat(-ci, -cr)], axis=2)
    pr, pi = jnp.real(powers), jnp.imag(powers)
    pw = jnp.stack([cat(pr, pr), cat(pi, pi)], axis=2)
    pw = jnp.pad(pw, ((0, 0), (0, 0), (0, 0), (0, 24 - (t + 1)), (0, 0)))
    ar, ai = pr[:, :, t], pi[:, :, t]
    a16 = jnp.stack([cat(ar, ar).reshape(depth, g * 2 * p),
                     cat(-ai, ai).reshape(depth, g * 2 * p)], axis=1)
    return bb, cc, pw, a16


def _outproj_kernel(a_ref, g_ref, x_ref, wglu_f32, bglu_ref, sg_ref, wout_f32, ng_ref,
                    o_ref, xn_ref, wglu_ref, wout_ref):
    @pl.when(pl.program_id(0) == 0)
    def _():
        wglu_ref[...] = wglu_f32[...].astype(BF16)
        wout_ref[...] = wout_f32[...].astype(BF16)

    g = _gelu_tanh(g_ref[...])
    z = jnp.dot(g.astype(BF16), wglu_ref[...], preferred_element_type=F32) + bglu_ref[...]
    s = g * jax.nn.sigmoid(z)
    s_n = _rms(s, sg_ref[...]).astype(BF16)
    y = jnp.dot(a_ref[...], wout_ref[:ATTN_WIDTH, :], preferred_element_type=F32)
    y = y + jnp.dot(s_n, wout_ref[ATTN_WIDTH:, :], preferred_element_type=F32)
    h = x_ref[...] + y
    o_ref[...] = h
    xn_ref[...] = _rms(h, ng_ref[...]).astype(xn_ref.dtype)


def _outproj(attn_n, g_tok, x, w_glu_all, b_glu, ssm_gain, w_out_all, mlp_gain, layer):
    rows = x.shape[0]
    tm = _row_tile(rows, 384)
    row = lambda i: (i, 0)
    fixed = lambda i: (0, 0)
    return pl.pallas_call(
        _outproj_kernel,
        grid=(rows // tm,),
        in_specs=[
            pl.BlockSpec((tm, ATTN_WIDTH), row),
            pl.BlockSpec((tm, SSM_WIDTH), row),
            pl.BlockSpec((tm, D_MODEL), row),
            pl.BlockSpec((None, SSM_WIDTH, SSM_WIDTH), lambda i: (layer, 0, 0)),
            pl.BlockSpec((1, SSM_WIDTH), fixed),
            pl.BlockSpec((1, SSM_WIDTH), fixed),
            pl.BlockSpec((None, D_MODEL, D_MODEL), lambda i: (layer, 0, 0)),
            pl.BlockSpec((1, D_MODEL), fixed),
        ],
        out_specs=[pl.BlockSpec((tm, D_MODEL), row), pl.BlockSpec((tm, D_MODEL), row)],
        out_shape=[jax.ShapeDtypeStruct((rows, D_MODEL), F32),
                   jax.ShapeDtypeStruct((rows, D_MODEL), BF16)],
        scratch_shapes=[pltpu.VMEM((SSM_WIDTH, SSM_WIDTH), BF16), pltpu.VMEM((D_MODEL, D_MODEL), BF16)],
        compiler_params=pltpu.CompilerParams(
            dimension_semantics=("arbitrary",), vmem_limit_bytes=BIG_VMEM_LIMIT),
        name="outproj",
    )(attn_n, g_tok, x, w_glu_all, b_glu, ssm_gain, w_out_all, mlp_gain)


def _mlp_tile(xn, w_up, w_down, o_ref, rows):
    a = jnp.dot(xn, w_up, preferred_element_type=F32)
    a = jnp.square(jnp.maximum(a, 0.0)).astype(BF16)
    o_ref[rows, :] += jnp.dot(a, w_down, preferred_element_type=F32)


def _mlp_first_kernel(xn_ref, x_hbm, wu_ref, wd_ref, o_ref, wu_bf_ref, wd_bf_ref, sem, *, tm):
    @pl.when(pl.program_id(0) == 0)
    def _():
        copy = pltpu.make_async_copy(x_hbm.at[pl.ds(0, o_ref.shape[0])], o_ref, sem)
        copy.start()
        copy.wait()

    wu_bf_ref[...] = wu_ref[...].astype(BF16)
    wd_bf_ref[...] = wd_ref[...].astype(BF16)
    for t in range(o_ref.shape[0] // tm):
        rows = slice(t * tm, (t + 1) * tm)
        _mlp_tile(xn_ref[rows, :], wu_bf_ref[...], wd_bf_ref[...], o_ref, rows)


def _row_pieces(lo, hi, tokens):
    if tokens is None:
        return [(0, hi - lo, lo)]
    lp, seq, n_seq = tokens
    pieces = []
    for b in range(n_seq):
        first, last = max(lo, b * lp + N_META), min(hi, b * lp + N_META + seq)
        if last > first:
            pieces.append((first - lo, last - first, b * seq + first - (b * lp + N_META)))
    return pieces


def _mlp_rest_kernel(xn_ref, x_ref, wu_ref, wd_ref, first_hbm, out_hbm, acc_even, acc_odd, sem,
                     *, first_pieces, tile_pieces):
    i = pl.program_id(0)
    f = pl.program_id(1)
    n_tiles = len(tile_pieces)
    last_f = pl.num_programs(1) - 1
    width = max(len(p) for p in tile_pieces)

    first_copies = [pltpu.make_async_copy(first_hbm.at[pl.ds(src, n)], out_hbm.at[pl.ds(dst, n)],
                                          sem.at[k])
                    for k, (src, n, dst) in enumerate(first_pieces)]

    def tile_copies(tile, acc):
        base = len(first_pieces) + (tile % 2) * width
        return [pltpu.make_async_copy(acc.at[pl.ds(src, n)], out_hbm.at[pl.ds(dst, n)],
                                      sem.at[base + k])
                for k, (src, n, dst) in enumerate(tile_pieces[tile])]

    @pl.when((i == 0) & (f == 0))
    def _():
        for c in first_copies:
            c.start()

    for parity, acc in ((0, acc_even), (1, acc_odd)):
        @pl.when(i % 2 == parity)
        def _(parity=parity, acc=acc):
            @pl.when(f == 0)
            def _():
                for tile in range(parity + 2, n_tiles, 2):
                    @pl.when(i == tile)
                    def _(tile=tile):
                        for c in tile_copies(tile - 2, acc):
                            c.wait()
                acc[...] = x_ref[...]

            _mlp_tile(xn_ref[...], wu_ref[...], wd_ref[...], acc, slice(None))

            for tile in range(parity, n_tiles, 2):
                @pl.when((f == last_f) & (i == tile))
                def _(tile=tile):
                    for c in tile_copies(tile, acc):
                        c.start()

    @pl.when((i == n_tiles - 1) & (f == last_f))
    def _():
        for tile in range(max(n_tiles - 2, 0), n_tiles):
            for c in tile_copies(tile, acc_odd if tile % 2 else acc_even):
                c.wait()
        for c in first_copies:
            c.wait()


def _mlp(xn, x, w_up_all, w_down_all, layer, tokens=None):
    rows = x.shape[0]
    tm = _row_tile(rows, MLP_ROWS)
    skip = min(MLP_TILES_FIRST, rows // tm)
    tm_first = skip * tm
    tf_first, tf = MLP_FF_TILE_FIRST, MLP_FF_TILE
    first_rows = lambda f: (0, 0)
    first, w_up_bf, w_down_bf = pl.pallas_call(
        functools.partial(_mlp_first_kernel, tm=tm),
        grid=(D_FF // tf_first,),
        in_specs=[
            pl.BlockSpec((tm_first, D_MODEL), first_rows),
            pl.BlockSpec(memory_space=pl.ANY),
            pl.BlockSpec((None, D_MODEL, tf_first), lambda f: (layer, 0, f)),
            pl.BlockSpec((None, tf_first, D_MODEL), lambda f: (layer, f, 0)),
        ],
        out_specs=[pl.BlockSpec((tm_first, D_MODEL), first_rows),
                   pl.BlockSpec((D_MODEL, tf_first), lambda f: (0, f)),
                   pl.BlockSpec((tf_first, D_MODEL), lambda f: (f, 0))],
        out_shape=[jax.ShapeDtypeStruct((tm_first, D_MODEL), F32),
                   jax.ShapeDtypeStruct((D_MODEL, D_FF), BF16),
                   jax.ShapeDtypeStruct((D_FF, D_MODEL), BF16)],
        scratch_shapes=[pltpu.SemaphoreType.DMA(())],
        compiler_params=pltpu.CompilerParams(
            dimension_semantics=("arbitrary",), vmem_limit_bytes=BIG_VMEM_LIMIT),
        name="mlp_first",
    )(xn, x, w_up_all, w_down_all)
    n_rest = rows // tm - skip
    if n_rest == 0:
        if tokens is None:
            return first
        lp, seq, n_seq = tokens
        return first.reshape(n_seq, lp, D_MODEL)[:, N_META:N_META + seq].reshape(n_seq * seq, D_MODEL)
    first_pieces = _row_pieces(0, tm_first, tokens)
    tile_pieces = [_row_pieces((skip + t) * tm, (skip + t + 1) * tm, tokens) for t in range(n_rest)]
    out_rows = rows if tokens is None else tokens[1] * tokens[2]
    n_sems = len(first_pieces) + 2 * max(len(p) for p in tile_pieces)
    return pl.pallas_call(
        functools.partial(_mlp_rest_kernel, first_pieces=first_pieces, tile_pieces=tile_pieces),
        grid=(n_rest, D_FF // tf),
        in_specs=[
            pl.BlockSpec((tm, D_MODEL), lambda i, f: (i + skip, 0)),
            pl.BlockSpec((tm, D_MODEL), lambda i, f: (i + skip, 0)),
            pl.BlockSpec((D_MODEL, tf), lambda i, f: (0, f)),
            pl.BlockSpec((tf, D_MODEL), lambda i, f: (f, 0)),
            pl.BlockSpec(memory_space=pl.ANY),
        ],
        out_specs=pl.BlockSpec(memory_space=pl.ANY),
        out_shape=jax.ShapeDtypeStruct((out_rows, D_MODEL), F32),
        scratch_shapes=[pltpu.VMEM((tm, D_MODEL), F32), pltpu.VMEM((tm, D_MODEL), F32),
                        pltpu.SemaphoreType.DMA((n_sems,))],
        compiler_params=pltpu.CompilerParams(
            dimension_semantics=("arbitrary", "arbitrary"), vmem_limit_bytes=BIG_VMEM_LIMIT),
        name="mlp_rest",
    )(xn, x, w_up_bf, w_down_bf, first)


def kernel(x, meta_tokens, norm_mix_g, w_in, q_norm_g, k_norm_g, attn_sinks, ssm_lambda_re, ssm_lambda_im, ssm_log_step, ssm_b_re, ssm_b_im, ssm_c_re, ssm_c_im, ssm_d, w_glu, b_glu, attn_out_g, ssm_out_g, w_out, norm_mlp_g, w_up, w_down):
    b, seq, d = x.shape
    depth = w_in.shape[0]
    assert d == D_MODEL
    length = N_META + seq
    lp = -(-length // TILE) * TILE
    rows = b * lp

    bb, cc, pw, a16 = _ssm_tables(
        ssm_lambda_re, ssm_lambda_im, ssm_log_step, ssm_b_re, ssm_b_im, ssm_c_re, ssm_c_im)

    head_id = jnp.arange(KV_WIDTH) // HEAD_DIM
    head_ones = (head_id[:, None] == head_id[None, :]).astype(BF16)
    slopes = jnp.exp2(-8.0 * jnp.arange(1, N_HEADS + 1, dtype=F32) / N_HEADS)
    attn_bias, attn_off = _attn_bias(slopes)

    row_vec = lambda v: v.astype(F32).reshape(1, -1)
    for l in range(depth):
        q_gain = jnp.tile(row_vec(q_norm_g[l]), (1, N_KV_HEADS)) * (LOG2E / math.sqrt(HEAD_DIM))
        k_gain = jnp.tile(row_vec(k_norm_g[l]), (1, N_KV_HEADS))
        sink_row = jnp.broadcast_to(
            (attn_sinks[l].astype(F32) * LOG2E).reshape(N_KV_HEADS, 1, KV_GROUP, 1),
            (N_KV_HEADS, 1, KV_GROUP, SUB)).reshape(N_KV_HEADS, 1, KV_GROUP * SUB)
        shared = (row_vec(norm_mix_g[l]), w_in, l, head_ones, q_gain, k_gain, attn_bias, attn_off,
                  sink_row, row_vec(attn_out_g[l]), b, rows)
        if l == 0:
            u_tok, attn_n, h_res = _inproj_attention(
                x.reshape(b * seq, d), *shared, meta_tokens=meta_tokens.astype(F32))
        else:
            u_tok, attn_n = _inproj_attention(h_res, *shared)
        g_tok = _ssm(u_tok, bb[l], cc[l], pw[l], a16[l], row_vec(ssm_d[l]), b)
        h_res, xn = _outproj(attn_n, g_tok, h_res, w_glu,
                             row_vec(b_glu[l]), row_vec(ssm_out_g[l]), w_out,
                             row_vec(norm_mlp_g[l]), l)
        h_res = _mlp(xn, h_res, w_up, w_down, l, (lp, seq, b) if l == depth - 1 else None)
    return h_res.reshape(b, seq, d)
```

```python
import functools
import math

import jax
import jax.numpy as jnp
from jax import lax
from jax.experimental import pallas as pl
from jax.experimental.pallas import tpu as pltpu

D_MODEL = 2048
N_META = 16
HEAD_DIM = 64
ATTN_WIDTH = D_MODEL // 2
N_HEADS = ATTN_WIDTH // HEAD_DIM
N_KV_HEADS = N_HEADS // 4
KV_GROUP = N_HEADS // N_KV_HEADS
KV_WIDTH = N_KV_HEADS * HEAD_DIM
SSM_WIDTH = D_MODEL - ATTN_WIDTH
SSM_GROUP_CH = 16
SSM_GROUPS = SSM_WIDTH // SSM_GROUP_CH
SSM_STATE = 64
WINDOW = 128
BLOCK = 128
D_FF = 4 * D_MODEL
IN_WIDTH = ATTN_WIDTH + 2 * KV_WIDTH + SSM_WIDTH
NORM_EPS = 1e-6
NEG_INF = -1e30
LOG2E = math.log2(math.e)

CHUNK = 16
CHUNK_W = CHUNK * SSM_GROUP_CH
GROUPS_PER_STEP = 8
SCAN_UNROLL = 8
RELAYOUT_UNROLL = True

QKV_WIDTH = ATTN_WIDTH + 2 * KV_WIDTH
COL_K = ATTN_WIDTH
COL_V = ATTN_WIDTH + KV_WIDTH
LANES = 128
BLOCKS_PER_VREG = LANES // SSM_GROUP_CH

VMEM_LIMIT = 48 * 1024 * 1024
MLP_ROWS = 768
MLP_FF_TILE = 1024
MLP_TILES_FIRST = 3
MLP_FF_TILE_FIRST = 512
BIG_VMEM_LIMIT = 56 * 1024 * 1024

F32 = jnp.float32
BF16 = jnp.bfloat16


def _row_tile(rows, target):
    k = rows // BLOCK
    best = 1
    for d in range(1, k + 1):
        if k % d == 0 and d * BLOCK <= target:
            best = d
    return best * BLOCK


def _rms(x, gain):
    ms = jnp.mean(x * x, axis=-1, keepdims=True)
    return x * lax.rsqrt(ms + NORM_EPS) * gain


def _head_norm(x, head_ones, gain):
    x2 = x * x
    hi = x2.astype(BF16)
    lo = (x2 - hi.astype(F32)).astype(BF16)
    ssq = (jnp.dot(hi, head_ones, preferred_element_type=F32)
           + jnp.dot(lo, head_ones, preferred_element_type=F32))
    return x * lax.rsqrt(ssq * (1.0 / HEAD_DIM) + NORM_EPS) * gain


SUB = 64
BAND = WINDOW + SUB
KEYS = BAND + N_META
TILE = 3 * BLOCK


def _attn_bias(slopes):
    j = jnp.arange(KEYS)[:, None]
    i = jnp.arange(SUB)[None, :]
    is_band = j < BAND
    dist = WINDOW + i - j
    sl = slopes.reshape(N_KV_HEADS, 1, KV_GROUP, 1)
    dist_f = dist.astype(F32)[None, :, None, :]
    band = jnp.where(((dist >= 0) & (dist < WINDOW))[None, :, None, :], -sl * dist_f, NEG_INF)
    meta = -sl * (i - (j - BAND)).astype(F32)[None, :, None, :]
    bias = jnp.where(is_band[None, :, None, :], band, meta) * LOG2E
    lanes = KV_GROUP * SUB
    off_coef = jnp.broadcast_to(-sl * LOG2E, (N_KV_HEADS, N_META, KV_GROUP, SUB))
    return bias.reshape(N_KV_HEADS, KEYS, lanes), off_coef.reshape(N_KV_HEADS, N_META, lanes)


def _project_tile(x_ref, g_ref, w_bf, ones_ref, qg_ref, kg_ref, qkv_out, u_ref):
    h = _rms(x_ref[...], g_ref[...]).astype(BF16)
    proj = jnp.dot(h, w_bf[...], preferred_element_type=F32)
    head_ones = ones_ref[...]
    for blk in range(ATTN_WIDTH // KV_WIDTH):
        cols = slice(blk * KV_WIDTH, (blk + 1) * KV_WIDTH)
        qkv_out[:, cols] = _head_norm(proj[:, cols], head_ones, qg_ref[...]).astype(BF16)
    qkv_out[:, COL_K:COL_V] = _head_norm(proj[:, COL_K:COL_V], head_ones, kg_ref[...]).astype(BF16)
    qkv_out[:, COL_V:QKV_WIDTH] = proj[:, COL_V:QKV_WIDTH].astype(BF16)
    u_ref[...] = proj[:, QKV_WIDTH:]


def _attend_tile(qkv, tail, meta, bias_ref, offc_ref, sink_ref, og_ref, o_ref, acc, tile_pos, masked):
    lanes = KV_GROUP * SUB
    ones = jnp.ones((KEYS, 8), BF16)
    for blk in range(TILE // BLOCK):
        chains = [(sb, kh) for sb in range(BLOCK // SUB) for kh in range(N_KV_HEADS)]
        scores, values = [], []
        for sb, kh in chains:
            row0 = blk * BLOCK + sb * SUB
            start = tile_pos + row0
            first = row0 - WINDOW

            def window(col):
                cols = slice(col + kh * HEAD_DIM, col + (kh + 1) * HEAD_DIM)
                tcols = slice(cols.start - COL_K, cols.stop - COL_K)
                if first >= 0:
                    band = qkv[first:first + BAND, cols]
                else:
                    band = jnp.concatenate([tail[WINDOW + first:WINDOW, tcols],
                                            qkv[0:BAND + first, cols]], axis=0)
                return jnp.concatenate([band, meta[:, tcols]], axis=0)

            k_win = window(COL_K)
            values.append(jnp.concatenate([window(COL_V), ones], axis=1))
            q_s = jnp.concatenate(
                [qkv[row0:row0 + SUB, (kh * KV_GROUP + g) * HEAD_DIM:(kh * KV_GROUP + g + 1) * HEAD_DIM]
                 for g in range(KV_GROUP)], axis=0)
            s = lax.dot_general(k_win, q_s, (((1,), (1,)), ((), ())),
                                preferred_element_type=F32)
            s = s + bias_ref[kh]
            s = jnp.concatenate([s[:BAND], s[BAND:] + offc_ref[kh] * start.astype(F32)], axis=0)
            if masked:
                j = lax.broadcasted_iota(jnp.int32, (KEYS, lanes), 0)
                i = lax.broadcasted_iota(jnp.int32, (KEYS, lanes), 1) % SUB
                ok = jnp.where(j < BAND, start - WINDOW + j - N_META, start + i - (j - BAND)) >= 0
                s = jnp.where(ok, s, NEG_INF)
            scores.append(s)
        maxes = [jnp.maximum(jnp.max(s, axis=0, keepdims=True), sink_ref[kh])
                 for s, (sb, kh) in zip(scores, chains)]
        probs = [jnp.exp2(s - m).astype(BF16) for s, m in zip(scores, maxes)]
        for (sb, kh), p, m, v_ext in zip(chains, probs, maxes, values):
            o_ext = lax.dot_general(v_ext, p, (((0,), (0,)), ((), ())),
                                    preferred_element_type=F32)
            denom = o_ext[HEAD_DIM:HEAD_DIM + 1] + jnp.exp2(sink_ref[kh] - m)
            o = o_ext[:HEAD_DIM] / denom
            for g in range(KV_GROUP):
                h = kh * KV_GROUP + g
                acc[h * HEAD_DIM:(h + 1) * HEAD_DIM, sb * SUB:(sb + 1) * SUB] = o[:, g * SUB:(g + 1) * SUB]
        a_t = acc[...]
        ms = jnp.mean(a_t * a_t, axis=0, keepdims=True)
        a_n = a_t * lax.rsqrt(ms + NORM_EPS)
        o_ref[blk * BLOCK:(blk + 1) * BLOCK, :] = (a_n.T * og_ref[...]).astype(o_ref.dtype)


def _inproj_attn_kernel(*refs, tiles_per_seq, n_tiles, embed):
    if embed is None:
        (x_ref, g_ref, w_ref, ones_ref, qg_ref, kg_ref, bias_ref, offc_ref, sink_ref, og_ref,
         u_ref, attn_ref, w_bf, ring, tail, meta, acc) = refs
    else:
        (x_hbm, meta_tok_ref, g_ref, w_ref, ones_ref, qg_ref, kg_ref, bias_ref, offc_ref,
         sink_ref, og_ref, u_ref, attn_ref, h0_hbm, w_bf, ring, tail, meta, acc, xbuf,
         xsem) = refs
    s = pl.program_id(0)

    @pl.when(s == 0)
    def _():
        w_bf[...] = w_ref[...].astype(BF16)
        ring[...] = jnp.zeros_like(ring)
        tail[...] = jnp.zeros_like(tail)
        meta[...] = jnp.zeros_like(meta)

    t_proj = jnp.minimum(s, n_tiles - 1)
    if embed is not None:
        seq, n_seq = embed
        lp = tiles_per_seq * TILE

        def token_copies(tile):
            slot = tile % 2
            return [pltpu.make_async_copy(x_hbm.at[pl.ds(dst, n)], xbuf.at[slot, pl.ds(src, n)],
                                          xsem.at[2 * slot + k])
                    for k, (src, n, dst) in enumerate(_row_pieces(tile * TILE, (tile + 1) * TILE,
                                                                  (lp, seq, n_seq)))]

        def stream_copy(tile):
            return pltpu.make_async_copy(xbuf.at[tile % 2], h0_hbm.at[pl.ds(tile * TILE, TILE)],
                                         xsem.at[4 + tile % 2])

        for tile in range(n_tiles):
            @pl.when(s == tile)
            def _(tile=tile):
                if tile == 0:
                    for c in token_copies(0):
                        c.start()
                for c in token_copies(tile):
                    c.wait()
                first_row = (tile % tiles_per_seq) * TILE
                if first_row == 0:
                    xbuf[tile % 2, 0:N_META, :] = meta_tok_ref[...]
                pad_from = N_META + seq - first_row
                if pad_from < TILE:
                    xbuf[tile % 2, max(pad_from, 0):TILE, :] = jnp.zeros(
                        (TILE - max(pad_from, 0), D_MODEL), F32)
                if tile >= 1:
                    stream_copy(tile - 1).wait()
                if tile + 1 < n_tiles:
                    for c in token_copies(tile + 1):
                        c.start()
                stream_copy(tile).start()

        @pl.when(s == n_tiles)
        def _():
            stream_copy(n_tiles - 1).wait()

        x_ref = xbuf.at[t_proj % 2]

    t_attn = jnp.maximum(s - 1, 0)
    seq_tile = t_attn % tiles_per_seq
    seq_slot = (t_attn // tiles_per_seq) % 2
    cur = (s + 1) % 2
    nxt = s % 2

    def step(masked):
        _attend_tile(ring.at[cur], tail, meta.at[seq_slot], bias_ref, offc_ref, sink_ref, og_ref,
                     attn_ref, acc, seq_tile * TILE, masked)
        tail[...] = ring[cur, TILE - WINDOW:TILE, COL_K:QKV_WIDTH]
        _project_tile(x_ref, g_ref, w_bf, ones_ref, qg_ref, kg_ref, ring.at[nxt], u_ref)

    @pl.when(seq_tile != 0)
    def _():
        step(False)

    @pl.when(seq_tile == 0)
    def _():
        step(True)

    @pl.when(t_proj % tiles_per_seq == 0)
    def _():
        meta[(t_proj // tiles_per_seq) % 2] = ring[nxt, 0:N_META, COL_K:QKV_WIDTH]


def _inproj_attention(x, gain, w_all, layer, head_ones, q_gain, k_gain, bias, off_coef, sink_row,
                      out_gain, n_seq, rows, meta_tokens=None):
    n_tiles = rows // TILE
    tiles_per_seq = n_tiles // n_seq
    assert rows == n_seq * tiles_per_seq * TILE
    embed = None if meta_tokens is None else (x.shape[0] // n_seq, n_seq)
    kern = functools.partial(_inproj_attn_kernel, tiles_per_seq=tiles_per_seq, n_tiles=n_tiles,
                             embed=embed)
    fixed = lambda s: (0, 0)
    fixed3 = lambda s: (0, 0, 0)
    proj_tile = lambda s: (jnp.minimum(s, n_tiles - 1), 0)
    if embed is None:
        x_specs, x_args = [pl.BlockSpec((TILE, D_MODEL), proj_tile)], [x]
    else:
        x_specs = [pl.BlockSpec(memory_space=pl.ANY), pl.BlockSpec((N_META, D_MODEL), fixed)]
        x_args = [x, meta_tokens]
    out_specs = [pl.BlockSpec((TILE, SSM_WIDTH), proj_tile),
                 pl.BlockSpec((TILE, ATTN_WIDTH), lambda s: (jnp.maximum(s - 1, 0), 0))]
    out_shape = [jax.ShapeDtypeStruct((rows, SSM_WIDTH), F32),
                 jax.ShapeDtypeStruct((rows, ATTN_WIDTH), BF16)]
    scratch = [
        pltpu.VMEM((D_MODEL, IN_WIDTH), BF16),
        pltpu.VMEM((2, TILE, QKV_WIDTH), BF16),
        pltpu.VMEM((WINDOW, 2 * KV_WIDTH), BF16),
        pltpu.VMEM((2, N_META, 2 * KV_WIDTH), BF16),
        pltpu.VMEM((ATTN_WIDTH, BLOCK), F32),
    ]
    if embed is not None:
        out_specs.append(pl.BlockSpec(memory_space=pl.ANY))
        out_shape.append(jax.ShapeDtypeStruct((rows, D_MODEL), F32))
        scratch += [pltpu.VMEM((2, TILE, D_MODEL), F32),
                    pltpu.SemaphoreType.DMA((6,))]
    return pl.pallas_call(
        kern,
        grid=(n_tiles + 1,),
        in_specs=x_specs + [
            pl.BlockSpec((1, D_MODEL), fixed),
            pl.BlockSpec((None, D_MODEL, IN_WIDTH), lambda s: (layer, 0, 0)),
            pl.BlockSpec((KV_WIDTH, KV_WIDTH), fixed),
            pl.BlockSpec((1, KV_WIDTH), fixed),
            pl.BlockSpec((1, KV_WIDTH), fixed),
            pl.BlockSpec(bias.shape, fixed3),
            pl.BlockSpec(off_coef.shape, fixed3),
            pl.BlockSpec(sink_row.shape, fixed3),
            pl.BlockSpec((1, ATTN_WIDTH), fixed),
        ],
        out_specs=out_specs,
        out_shape=out_shape,
        scratch_shapes=scratch,
        compiler_params=pltpu.CompilerParams(
            dimension_semantics=("arbitrary",), vmem_limit_bytes=BIG_VMEM_LIMIT),
        name="inproj_attn",
    )(*x_args, gain, w_all, head_ones, q_gain, k_gain, bias, off_coef, sink_row, out_gain)


def _gelu_tanh(x):
    c = math.sqrt(2.0 / math.pi)
    return 0.5 * x * (1.0 + jnp.tanh(c * (x + 0.044715 * (x * x * x))))


def _block_transpose_stage(arrs, dist, lane_blk):
    nblk = BLOCKS_PER_VREG
    keep = (lane_blk & dist) == 0
    nxt = list(arrs)
    for i in range(nblk):
        if i & dist == 0:
            lo, hi = arrs[i], arrs[i + dist]
            nxt[i] = jnp.where(keep, lo, pltpu.roll(hi, dist * SSM_GROUP_CH, axis=1))
            nxt[i + dist] = jnp.where(keep, pltpu.roll(lo, (nblk - dist) * SSM_GROUP_CH, axis=1), hi)
    return nxt


def _pipelined_block_transpose(n_tiles, load, store, zero, lane_blk):
    def trip(r, carry):
        after1, after2 = carry
        fresh = load(jnp.minimum(r, n_tiles - 1))
        new1 = [_block_transpose_stage(g, 4, lane_blk) for g in fresh]
        new2 = [_block_transpose_stage(g, 2, lane_blk) for g in after1]
        done = [_block_transpose_stage(g, 1, lane_blk) for g in after2]
        store(jnp.maximum(r - 2, 0), done)
        return new1, new2

    groups = CHUNK // BLOCKS_PER_VREG
    init = [[zero] * BLOCKS_PER_VREG for _ in range(groups)]
    lax.fori_loop(0, n_tiles + 2, trip, (init, init), unroll=RELAYOUT_UNROLL)


def _ssm_kernel(u_ref, bb_ref, cc_ref, pw_ref, a_ref, d_ref, o_ref,
                lhs, yscr, m_scr, win_scr, wsw_scr, wct_scr, zin, zsw, st,
                *, n_chunks, n_batch):
    nblk = BLOCKS_PER_VREG
    sw = 2 * SSM_STATE
    rc = n_batch * n_chunks
    half = CHUNK // nblk

    def tok_rows(tile, tile_chunks, t):
        return pl.ds(tile * (tile_chunks * CHUNK) + t, tile_chunks, stride=CHUNK)

    lane256 = lax.broadcasted_iota(jnp.int32, (CHUNK, CHUNK_W), 1)
    for g in range(GROUPS_PER_STEP):
        bb, bbs, bsw, bsws = bb_ref[g, 0], bb_ref[g, 1], bb_ref[g, 2], bb_ref[g, 3]
        cc, ccs = cc_ref[g, 0], cc_ref[g, 1]
        c_pow = [pw_ref[g, 0, tau:tau + 1, :] * cc + pw_ref[g, 1, tau:tau + 1, :] * ccs
                 for tau in range(CHUNK + 1)]
        for t in range(CHUNK):
            rows = slice(t * CHUNK, (t + 1) * CHUNK)
            p_re = pw_ref[g, 0, CHUNK - 1 - t:CHUNK - t, :]
            p_im = pw_ref[g, 1, CHUNK - 1 - t:CHUNK - t, :]
            win_scr[g, rows, :] = (p_re * bb + p_im * bbs).astype(BF16)
            wsw_scr[g, rows, :] = (p_re * bsw + p_im * bsws).astype(BF16)
            wct_scr[g, rows, :] = c_pow[t + 1].astype(BF16)
        base = lax.dot_general(bb, jnp.concatenate(c_pow[:CHUNK], axis=0), (((1,), (1,)), ((), ())),
                               precision=lax.Precision.HIGHEST, preferred_element_type=F32)
        for sblk in range(CHUNK):
            row = base if sblk == 0 else jnp.where(
                lane256 >= sblk * SSM_GROUP_CH, pltpu.roll(base, sblk * SSM_GROUP_CH, axis=1), 0.0)
            m_scr[g, sblk * CHUNK:(sblk + 1) * CHUNK, :] = row.astype(BF16)

    rt_in = 16
    lane_blk_in = lax.broadcasted_iota(jnp.int32, (rt_in // 2, LANES), 1) // SSM_GROUP_CH

    def load_u(r):
        return [[pltpu.bitcast(u_ref[tok_rows(r, rt_in, hf * nblk + tb), :].astype(BF16), jnp.uint32)
                 for tb in range(nblk)] for hf in range(half)]

    def store_lhs(r, groups):
        rows = pl.ds(pl.multiple_of(r * rt_in, rt_in), rt_in)
        for hf in range(half):
            for g in range(nblk):
                col = g * CHUNK_W + hf * LANES
                lhs[rows, col:col + LANES] = pltpu.bitcast(groups[hf][g], BF16)

    _pipelined_block_transpose(rc // rt_in, load_u, store_lhs,
                               jnp.zeros((rt_in // 2, LANES), jnp.uint32), lane_blk_in)

    for g in range(GROUPS_PER_STEP):
        u_g = lhs[:, g * CHUNK_W:(g + 1) * CHUNK_W]
        zin[:, g * sw:(g + 1) * sw] = jnp.dot(u_g, win_scr[g], preferred_element_type=F32)
        zsw[:, g * sw:(g + 1) * sw] = jnp.dot(u_g, wsw_scr[g], preferred_element_type=F32)

    a1 = a_ref[0:1, :]
    a2 = a_ref[1:2, :]

    def step(c, carry):
        nxt = []
        for b in range(n_batch):
            s, x = carry[b]
            row = b * n_chunks + c
            st[pl.ds(row, 1), :] = s
            nxt.append((a1 * s + a2 * x + zin[pl.ds(row, 1), :],
                        a1 * x - a2 * s + zsw[pl.ds(row, 1), :]))
        return tuple(nxt)

    zero = jnp.zeros((1, GROUPS_PER_STEP * sw), F32)
    lax.fori_loop(0, n_chunks, step, tuple((zero, zero) for _ in range(n_batch)),
                  unroll=SCAN_UNROLL)

    for g in range(GROUPS_PER_STEP):
        cols = slice(g * CHUNK_W, (g + 1) * CHUNK_W)
        y = jnp.dot(lhs[:, cols], m_scr[g], preferred_element_type=F32)
        y = y + lax.dot_general(st[:, g * sw:(g + 1) * sw].astype(BF16), wct_scr[g],
                                (((1,), (1,)), ((), ())), preferred_element_type=F32)
        yscr[:, cols] = y.astype(BF16)

    d_row = d_ref[...]

    def load_y(r):
        rows = pl.ds(pl.multiple_of(r * rt_in, rt_in), rt_in)
        return [[pltpu.bitcast(yscr[rows, g * CHUNK_W + hf * LANES:g * CHUNK_W + (hf + 1) * LANES],
                               jnp.uint32)
                 for g in range(nblk)] for hf in range(half)]

    def store_tokens(r, groups):
        for hf in range(half):
            for tb in range(nblk):
                rows = tok_rows(r, rt_in, hf * nblk + tb)
                y = pltpu.bitcast(groups[hf][tb], BF16).astype(F32)
                o_ref[rows, :] = y + d_row * u_ref[rows, :]

    _pipelined_block_transpose(rc // rt_in, load_y, store_tokens,
                               jnp.zeros((rt_in // 2, LANES), jnp.uint32), lane_blk_in)


def _ssm(u_tok, bb, cc, pw, a16, d, n_batch):
    rows, width = u_tok.shape
    rc = rows // CHUNK
    n_chunks = rc // n_batch
    assert GROUPS_PER_STEP == BLOCKS_PER_VREG and rc % 16 == 0
    gs = GROUPS_PER_STEP
    bw = gs * CHUNK_W
    sw = gs * 2 * SSM_STATE
    kern = functools.partial(_ssm_kernel, n_chunks=n_chunks, n_batch=n_batch)
    grp4 = lambda j: (j, 0, 0, 0)
    return pl.pallas_call(
        kern,
        grid=(SSM_GROUPS // gs,),
        in_specs=[
            pl.BlockSpec((rows, LANES), lambda j: (0, j)),
            pl.BlockSpec((gs,) + bb.shape[1:], grp4),
            pl.BlockSpec((gs,) + cc.shape[1:], grp4),
            pl.BlockSpec((gs,) + pw.shape[1:], grp4),
            pl.BlockSpec((2, sw), lambda j: (0, j)),
            pl.BlockSpec((1, LANES), lambda j: (0, j)),
        ],
        out_specs=pl.BlockSpec((rows, LANES), lambda j: (0, j)),
        out_shape=jax.ShapeDtypeStruct((rows, width), F32),
        scratch_shapes=[
            pltpu.VMEM((rc, bw), BF16),
            pltpu.VMEM((rc, bw), BF16),
            pltpu.VMEM((gs, CHUNK_W, CHUNK_W), BF16),
            pltpu.VMEM((gs, CHUNK_W, 2 * SSM_STATE), BF16),
            pltpu.VMEM((gs, CHUNK_W, 2 * SSM_STATE), BF16),
            pltpu.VMEM((gs, CHUNK_W, 2 * SSM_STATE), BF16),
            pltpu.VMEM((rc, sw), F32),
            pltpu.VMEM((rc, sw), F32),
            pltpu.VMEM((rc, sw), F32),
        ],
        compiler_params=pltpu.CompilerParams(
            dimension_semantics=("parallel",), vmem_limit_bytes=VMEM_LIMIT),
        name="ssm",
    )(u_tok, bb, cc, pw, a16, d)


def _ssm_tables(lam_re, lam_im, log_step, b_re, b_im, c_re, c_im):
    depth = lam_re.shape[0]
    g, p, h, t = SSM_GROUPS, SSM_STATE, SSM_GROUP_CH, CHUNK
    lam = lax.complex(lam_re.astype(F32), lam_im.astype(F32))
    delta = jnp.exp(log_step.astype(F32))[..., None]
    ld = lam * delta
    lam_bar = jnp.exp(ld)
    b_bar = ((lam_bar - 1.0) / lam)[..., None] * lax.complex(b_re.astype(F32), b_im.astype(F32))
    c_c = lax.complex(c_re.astype(F32), c_im.astype(F32))
    pows = [jnp.ones_like(lam_bar), lam_bar]
    for _ in range(t - 1):
        pows.append(pows[-1] * lam_bar)
    powers = jnp.stack(pows, axis=2)

    cat = lambda x, y: jnp.concatenate([x, y], axis=-1)
    br, bi = jnp.real(b_bar).swapaxes(-1, -2), jnp.imag(b_bar).swapaxes(-1, -2)
    bb = jnp.stack([cat(br, bi), cat(-bi, br), cat(bi, br), cat(br, -bi)], axis=2)
    cr, ci = jnp.real(c_c), jnp.imag(c_c)
    cc = jnp.stack([cat(cr, -ci), cat(-ci, -cr)], axis=2)
    pr, pi = jnp.real(powers), jnp.imag(powers)
    pw = jnp.stack([cat(pr, pr), cat(pi, pi)], axis=2)
    pw = jnp.pad(pw, ((0, 0), (0, 0), (0, 0), (0, 24 - (t + 1)), (0, 0)))
    ar, ai = pr[:, :, t], pi[:, :, t]
    a16 = jnp.stack([cat(ar, ar).reshape(depth, g * 2 * p),
                     cat(-ai, ai).reshape(depth, g * 2 * p)], axis=1)
    return bb, cc, pw, a16


def _outproj_kernel(a_ref, g_ref, x_ref, wglu_f32, bglu_ref, sg_ref, wout_f32, ng_ref,
                    o_ref, xn_ref, wglu_ref, wout_ref):
    @pl.when(pl.program_id(0) == 0)
    def _():
        wglu_ref[...] = wglu_f32[...].astype(BF16)
        wout_ref[...] = wout_f32[...].astype(BF16)

    g = _gelu_tanh(g_ref[...])
    z = jnp.dot(g.astype(BF16), wglu_ref[...], preferred_element_type=F32) + bglu_ref[...]
    s = g * jax.nn.sigmoid(z)
    s_n = _rms(s, sg_ref[...]).astype(BF16)
    y = jnp.dot(a_ref[...], wout_ref[:ATTN_WIDTH, :], preferred_element_type=F32)
    y = y + jnp.dot(s_n, wout_ref[ATTN_WIDTH:, :], preferred_element_type=F32)
    h = x_ref[...] + y
    o_ref[...] = h
    xn_ref[...] = _rms(h, ng_ref[...]).astype(xn_ref.dtype)


def _outproj(attn_n, g_tok, x, w_glu_all, b_glu, ssm_gain, w_out_all, mlp_gain, layer):
    rows = x.shape[0]
    tm = _row_tile(rows, 384)
    row = lambda i: (i, 0)
    fixed = lambda i: (0, 0)
    return pl.pallas_call(
        _outproj_kernel,
        grid=(rows // tm,),
        in_specs=[
            pl.BlockSpec((tm, ATTN_WIDTH), row),
            pl.BlockSpec((tm, SSM_WIDTH), row),
            pl.BlockSpec((tm, D_MODEL), row),
            pl.BlockSpec((None, SSM_WIDTH, SSM_WIDTH), lambda i: (layer, 0, 0)),
            pl.BlockSpec((1, SSM_WIDTH), fixed),
            pl.BlockSpec((1, SSM_WIDTH), fixed),
            pl.BlockSpec((None, D_MODEL, D_MODEL), lambda i: (layer, 0, 0)),
            pl.BlockSpec((1, D_MODEL), fixed),
        ],
        out_specs=[pl.BlockSpec((tm, D_MODEL), row), pl.BlockSpec((tm, D_MODEL), row)],
        out_shape=[jax.ShapeDtypeStruct((rows, D_MODEL), F32),
                   jax.ShapeDtypeStruct((rows, D_MODEL), BF16)],
        scratch_shapes=[pltpu.VMEM((SSM_WIDTH, SSM_WIDTH), BF16), pltpu.VMEM((D_MODEL, D_MODEL), BF16)],
        compiler_params=pltpu.CompilerParams(
            dimension_semantics=("arbitrary",), vmem_limit_bytes=BIG_VMEM_LIMIT),
        name="outproj",
    )(attn_n, g_tok, x, w_glu_all, b_glu, ssm_gain, w_out_all, mlp_gain)


def _mlp_tile(xn, w_up, w_down, o_ref, rows):
    a = jnp.dot(xn, w_up, preferred_element_type=F32)
    a = jnp.square(jnp.maximum(a, 0.0)).astype(BF16)
    o_ref[rows, :] += jnp.dot(a, w_down, preferred_element_type=F32)


def _mlp_first_kernel(xn_ref, x_hbm, wu_ref, wd_ref, o_ref, wu_bf_ref, wd_bf_ref, sem, *, tm):
    @pl.when(pl.program_id(0) == 0)
    def _():
        copy = pltpu.make_async_copy(x_hbm.at[pl.ds(0, o_ref.shape[0])], o_ref, sem)
        copy.start()
        copy.wait()

    wu_bf_ref[...] = wu_ref[...].astype(BF16)
    wd_bf_ref[...] = wd_ref[...].astype(BF16)
    for t in range(o_ref.shape[0] // tm):
        rows = slice(t * tm, (t + 1) * tm)
        _mlp_tile(xn_ref[rows, :], wu_bf_ref[...], wd_bf_ref[...], o_ref, rows)


def _row_pieces(lo, hi, tokens):
    if tokens is None:
        return [(0, hi - lo, lo)]
    lp, seq, n_seq = tokens
    pieces = []
    for b in range(n_seq):
        first, last = max(lo, b * lp + N_META), min(hi, b * lp + N_META + seq)
        if last > first:
            pieces.append((first - lo, last - first, b * seq + first - (b * lp + N_META)))
    return pieces


def _mlp_rest_kernel(xn_ref, x_ref, wu_ref, wd_ref, first_hbm, out_hbm, acc_even, acc_odd, sem,
                     *, first_pieces, tile_pieces):
    i = pl.program_id(0)
    f = pl.program_id(1)
    n_tiles = len(tile_pieces)
    last_f = pl.num_programs(1) - 1
    width = max(len(p) for p in tile_pieces)

    first_copies = [pltpu.make_async_copy(first_hbm.at[pl.ds(src, n)], out_hbm.at[pl.ds(dst, n)],
                                          sem.at[k])
                    for k, (src, n, dst) in enumerate(first_pieces)]

    def tile_copies(tile, acc):
        base = len(first_pieces) + (tile % 2) * width
        return [pltpu.make_async_copy(acc.at[pl.ds(src, n)], out_hbm.at[pl.ds(dst, n)],
                                      sem.at[base + k])
                for k, (src, n, dst) in enumerate(tile_pieces[tile])]

    @pl.when((i == 0) & (f == 0))
    def _():
        for c in first_copies:
            c.start()

    for parity, acc in ((0, acc_even), (1, acc_odd)):
        @pl.when(i % 2 == parity)
        def _(parity=parity, acc=acc):
            @pl.when(f == 0)
            def _():
                for tile in range(parity + 2, n_tiles, 2):
                    @pl.when(i == tile)
                    def _(tile=tile):
                        for c in tile_copies(tile - 2, acc):
                            c.wait()
                acc[...] = x_ref[...]

            _mlp_tile(xn_ref[...], wu_ref[...], wd_ref[...], acc, slice(None))

            for tile in range(parity, n_tiles, 2):
                @pl.when((f == last_f) & (i == tile))
                def _(tile=tile):
                    for c in tile_copies(tile, acc):
                        c.start()

    @pl.when((i == n_tiles - 1) & (f == last_f))
    def _():
        for tile in range(max(n_tiles - 2, 0), n_tiles):
            for c in tile_copies(tile, acc_odd if tile % 2 else acc_even):
                c.wait()
        for c in first_copies:
            c.wait()


def _mlp(xn, x, w_up_all, w_down_all, layer, tokens=None):
    rows = x.shape[0]
    tm = _row_tile(rows, MLP_ROWS)
    skip = min(MLP_TILES_FIRST, rows // tm)
    tm_first = skip * tm
    tf_first, tf = MLP_FF_TILE_FIRST, MLP_FF_TILE
    first_rows = lambda f: (0, 0)
    first, w_up_bf, w_down_bf = pl.pallas_call(
        functools.partial(_mlp_first_kernel, tm=tm),
        grid=(D_FF // tf_first,),
        in_specs=[
            pl.BlockSpec((tm_first, D_MODEL), first_rows),
            pl.BlockSpec(memory_space=pl.ANY),
            pl.BlockSpec((None, D_MODEL, tf_first), lambda f: (layer, 0, f)),
            pl.BlockSpec((None, tf_first, D_MODEL), lambda f: (layer, f, 0)),
        ],
        out_specs=[pl.BlockSpec((tm_first, D_MODEL), first_rows),
                   pl.BlockSpec((D_MODEL, tf_first), lambda f: (0, f)),
                   pl.BlockSpec((tf_first, D_MODEL), lambda f: (f, 0))],
        out_shape=[jax.ShapeDtypeStruct((tm_first, D_MODEL), F32),
                   jax.ShapeDtypeStruct((D_MODEL, D_FF), BF16),
                   jax.ShapeDtypeStruct((D_FF, D_MODEL), BF16)],
        scratch_shapes=[pltpu.SemaphoreType.DMA(())],
        compiler_params=pltpu.CompilerParams(
            dimension_semantics=("arbitrary",), vmem_limit_bytes=BIG_VMEM_LIMIT),
        name="mlp_first",
    )(xn, x, w_up_all, w_down_all)
    n_rest = rows // tm - skip
    if n_rest == 0:
        if tokens is None:
            return first
        lp, seq, n_seq = tokens
        return first.reshape(n_seq, lp, D_MODEL)[:, N_META:N_META + seq].reshape(n_seq * seq, D_MODEL)
    first_pieces = _row_pieces(0, tm_first, tokens)
    tile_pieces = [_row_pieces((skip + t) * tm, (skip + t + 1) * tm, tokens) for t in range(n_rest)]
    out_rows = rows if tokens is None else tokens[1] * tokens[2]
    n_sems = len(first_pieces) + 2 * max(len(p) for p in tile_pieces)
    return pl.pallas_call(
        functools.partial(_mlp_rest_kernel, first_pieces=first_pieces, tile_pieces=tile_pieces),
        grid=(n_rest, D_FF // tf),
        in_specs=[
            pl.BlockSpec((tm, D_MODEL), lambda i, f: (i + skip, 0)),
            pl.BlockSpec((tm, D_MODEL), lambda i, f: (i + skip, 0)),
            pl.BlockSpec((D_MODEL, tf), lambda i, f: (0, f)),
            pl.BlockSpec((tf, D_MODEL), lambda i, f: (f, 0)),
            pl.BlockSpec(memory_space=pl.ANY),
        ],
        out_specs=pl.BlockSpec(memory_space=pl.ANY),
        out_shape=jax.ShapeDtypeStruct((out_rows, D_MODEL), F32),
        scratch_shapes=[pltpu.VMEM((tm, D_MODEL), F32), pltpu.VMEM((tm, D_MODEL), F32),
                        pltpu.SemaphoreType.DMA((n_sems,))],
        compiler_params=pltpu.CompilerParams(
            dimension_semantics=("arbitrary", "arbitrary"), vmem_limit_bytes=BIG_VMEM_LIMIT),
        name="mlp_rest",
    )(xn, x, w_up_bf, w_down_bf, first)


def kernel(x, meta_tokens, norm_mix_g, w_in, q_norm_g, k_norm_g, attn_sinks, ssm_lambda_re, ssm_lambda_im, ssm_log_step, ssm_b_re, ssm_b_im, ssm_c_re, ssm_c_im, ssm_d, w_glu, b_glu, attn_out_g, ssm_out_g, w_out, norm_mlp_g, w_up, w_down):
    b, seq, d = x.shape
    depth = w_in.shape[0]
    assert d == D_MODEL
    length = N_META + seq
    lp = -(-length // TILE) * TILE
    rows = b * lp

    bb, cc, pw, a16 = _ssm_tables(
        ssm_lambda_re, ssm_lambda_im, ssm_log_step, ssm_b_re, ssm_b_im, ssm_c_re, ssm_c_im)

    head_id = jnp.arange(KV_WIDTH) // HEAD_DIM
    head_ones = (head_id[:, None] == head_id[None, :]).astype(BF16)
    slopes = jnp.exp2(-8.0 * jnp.arange(1, N_HEADS + 1, dtype=F32) / N_HEADS)
    attn_bias, attn_off = _attn_bias(slopes)

    row_vec = lambda v: v.astype(F32).reshape(1, -1)
    for l in range(depth):
        q_gain = jnp.tile(row_vec(q_norm_g[l]), (1, N_KV_HEADS)) * (LOG2E / math.sqrt(HEAD_DIM))
        k_gain = jnp.tile(row_vec(k_norm_g[l]), (1, N_KV_HEADS))
        sink_row = jnp.broadcast_to(
            (attn_sinks[l].astype(F32) * LOG2E).reshape(N_KV_HEADS, 1, KV_GROUP, 1),
            (N_KV_HEADS, 1, KV_GROUP, SUB)).reshape(N_KV_HEADS, 1, KV_GROUP * SUB)
        shared = (row_vec(norm_mix_g[l]), w_in, l, head_ones, q_gain, k_gain, attn_bias, attn_off,
                  sink_row, row_vec(attn_out_g[l]), b, rows)
        if l == 0:
            u_tok, attn_n, h_res = _inproj_attention(
                x.reshape(b * seq, d), *shared, meta_tokens=meta_tokens.astype(F32))
        else:
            u_tok, attn_n = _inproj_attention(h_res, *shared)
        g_tok = _ssm(u_tok, bb[l], cc[l], pw[l], a16[l], row_vec(ssm_d[l]), b)
        h_res, xn = _outproj(attn_n, g_tok, h_res, w_glu,
                             row_vec(b_glu[l]), row_vec(ssm_out_g[l]), w_out,
                             row_vec(norm_mlp_g[l]), l)
        h_res = _mlp(xn, h_res, w_up, w_down, l, (lp, seq, b) if l == depth - 1 else None)
    return h_res.reshape(b, seq, d)
```

```python
import functools
import math

import jax
import jax.numpy as jnp
from jax import lax
from jax.experimental import pallas as pl
from jax.experimental.pallas import tpu as pltpu

D_MODEL = 2048
N_META = 16
HEAD_DIM = 64
ATTN_WIDTH = D_MODEL // 2
N_HEADS = ATTN_WIDTH // HEAD_DIM
N_KV_HEADS = N_HEADS // 4
KV_GROUP = N_HEADS // N_KV_HEADS
KV_WIDTH = N_KV_HEADS * HEAD_DIM
SSM_WIDTH = D_MODEL - ATTN_WIDTH
SSM_GROUP_CH = 16
SSM_GROUPS = SSM_WIDTH // SSM_GROUP_CH
SSM_STATE = 64
WINDOW = 128
BLOCK = 128
D_FF = 4 * D_MODEL
IN_WIDTH = ATTN_WIDTH + 2 * KV_WIDTH + SSM_WIDTH
NORM_EPS = 1e-6
NEG_INF = -1e30
LOG2E = math.log2(math.e)

CHUNK = 16
CHUNK_W = CHUNK * SSM_GROUP_CH
GROUPS_PER_STEP = 8
SCAN_UNROLL = 8
RELAYOUT_UNROLL = True

QKV_WIDTH = ATTN_WIDTH + 2 * KV_WIDTH
COL_K = ATTN_WIDTH
COL_V = ATTN_WIDTH + KV_WIDTH
LANES = 128
BLOCKS_PER_VREG = LANES // SSM_GROUP_CH

VMEM_LIMIT = 48 * 1024 * 1024
OUTPROJ_ROWS = 384
MLP_ROWS = 768
MLP_FF_TILE = 1024
MLP_TILES_FIRST = 2
MLP_FF_TILE_FIRST = 512
BIG_VMEM_LIMIT = 56 * 1024 * 1024

F32 = jnp.float32
BF16 = jnp.bfloat16


def _row_tile(rows, target):
    k = rows // BLOCK
    best = 1
    for d in range(1, k + 1):
        if k % d == 0 and d * BLOCK <= target:
            best = d
    return best * BLOCK


def _rms(x, gain):
    ms = jnp.mean(x * x, axis=-1, keepdims=True)
    return x * lax.rsqrt(ms + NORM_EPS) * gain


def _head_norm(x, head_ones, gain):
    x2 = x * x
    hi = x2.astype(BF16)
    lo = (x2 - hi.astype(F32)).astype(BF16)
    ssq = (jnp.dot(hi, head_ones, preferred_element_type=F32)
           + jnp.dot(lo, head_ones, preferred_element_type=F32))
    return x * lax.rsqrt(ssq * (1.0 / HEAD_DIM) + NORM_EPS) * gain


SUB = 64
BAND = WINDOW + SUB
KEYS = BAND + N_META
TILE = 3 * BLOCK


def _attn_bias(slopes):
    j = jnp.arange(KEYS)[:, None]
    i = jnp.arange(SUB)[None, :]
    is_band = j < BAND
    dist = WINDOW + i - j
    sl = slopes.reshape(N_KV_HEADS, 1, KV_GROUP, 1)
    dist_f = dist.astype(F32)[None, :, None, :]
    band = jnp.where(((dist >= 0) & (dist < WINDOW))[None, :, None, :], -sl * dist_f, NEG_INF)
    meta = -sl * (i - (j - BAND)).astype(F32)[None, :, None, :]
    bias = jnp.where(is_band[None, :, None, :], band, meta) * LOG2E
    lanes = KV_GROUP * SUB
    off_coef = jnp.broadcast_to(-sl * LOG2E, (N_KV_HEADS, N_META, KV_GROUP, SUB))
    return bias.reshape(N_KV_HEADS, KEYS, lanes), off_coef.reshape(N_KV_HEADS, N_META, lanes)


def _project_tile(x_ref, g_ref, w_bf, ones_ref, qg_ref, kg_ref, qkv_out, u_ref):
    h = _rms(x_ref[...], g_ref[...]).astype(BF16)
    proj = jnp.dot(h, w_bf[...], preferred_element_type=F32)
    head_ones = ones_ref[...]
    for blk in range(ATTN_WIDTH // KV_WIDTH):
        cols = slice(blk * KV_WIDTH, (blk + 1) * KV_WIDTH)
        qkv_out[:, cols] = _head_norm(proj[:, cols], head_ones, qg_ref[...]).astype(BF16)
    qkv_out[:, COL_K:COL_V] = _head_norm(proj[:, COL_K:COL_V], head_ones, kg_ref[...]).astype(BF16)
    qkv_out[:, COL_V:QKV_WIDTH] = proj[:, COL_V:QKV_WIDTH].astype(BF16)
    u_ref[...] = proj[:, QKV_WIDTH:]


def _attend_tile(qkv, tail, meta, bias_ref, offc_ref, sink_ref, og_ref, o_ref, acc, tile_pos, masked):
    lanes = KV_GROUP * SUB
    ones = jnp.ones((KEYS, 8), BF16)
    for blk in range(TILE // BLOCK):
        chains = [(sb, kh) for sb in range(BLOCK // SUB) for kh in range(N_KV_HEADS)]
        scores, values = [], []
        for sb, kh in chains:
            row0 = blk * BLOCK + sb * SUB
            start = tile_pos + row0
            first = row0 - WINDOW

            def window(col):
                cols = slice(col + kh * HEAD_DIM, col + (kh + 1) * HEAD_DIM)
                tcols = slice(cols.start - COL_K, cols.stop - COL_K)
                if first >= 0:
                    band = qkv[first:first + BAND, cols]
                else:
                    band = jnp.concatenate([tail[WINDOW + first:WINDOW, tcols],
                                            qkv[0:BAND + first, cols]], axis=0)
                return jnp.concatenate([band, meta[:, tcols]], axis=0)

            k_win = window(COL_K)
            values.append(jnp.concatenate([window(COL_V), ones], axis=1))
            q_s = jnp.concatenate(
                [qkv[row0:row0 + SUB, (kh * KV_GROUP + g) * HEAD_DIM:(kh * KV_GROUP + g + 1) * HEAD_DIM]
                 for g in range(KV_GROUP)], axis=0)
            s = lax.dot_general(k_win, q_s, (((1,), (1,)), ((), ())),
                                preferred_element_type=F32)
            s = s + bias_ref[kh]
            s = jnp.concatenate([s[:BAND], s[BAND:] + offc_ref[kh] * start.astype(F32)], axis=0)
            if masked:
                j = lax.broadcasted_iota(jnp.int32, (KEYS, lanes), 0)
                i = lax.broadcasted_iota(jnp.int32, (KEYS, lanes), 1) % SUB
                ok = jnp.where(j < BAND, start - WINDOW + j - N_META, start + i - (j - BAND)) >= 0
                s = jnp.where(ok, s, NEG_INF)
            scores.append(s)
        maxes = [jnp.maximum(jnp.max(s, axis=0, keepdims=True), sink_ref[kh])
                 for s, (sb, kh) in zip(scores, chains)]
        probs = [jnp.exp2(s - m).astype(BF16) for s, m in zip(scores, maxes)]
        for (sb, kh), p, m, v_ext in zip(chains, probs, maxes, values):
            o_ext = lax.dot_general(v_ext, p, (((0,), (0,)), ((), ())),
                                    preferred_element_type=F32)
            denom = o_ext[HEAD_DIM:HEAD_DIM + 1] + jnp.exp2(sink_ref[kh] - m)
            o = o_ext[:HEAD_DIM] / denom
            for g in range(KV_GROUP):
                h = kh * KV_GROUP + g
                acc[h * HEAD_DIM:(h + 1) * HEAD_DIM, sb * SUB:(sb + 1) * SUB] = o[:, g * SUB:(g + 1) * SUB]
        a_t = acc[...]
        ms = jnp.mean(a_t * a_t, axis=0, keepdims=True)
        a_n = a_t * lax.rsqrt(ms + NORM_EPS)
        o_ref[blk * BLOCK:(blk + 1) * BLOCK, :] = (a_n.T * og_ref[...]).astype(o_ref.dtype)


def _inproj_attn_kernel(*refs, tiles_per_seq, n_tiles, embed):
    if embed is None:
        (x_ref, g_ref, w_ref, ones_ref, qg_ref, kg_ref, bias_ref, offc_ref, sink_ref, og_ref,
         u_ref, attn_ref, w_bf, ring, tail, meta, acc) = refs
    else:
        (x_hbm, meta_tok_ref, g_ref, w_ref, ones_ref, qg_ref, kg_ref, bias_ref, offc_ref,
         sink_ref, og_ref, u_ref, attn_ref, h0_hbm, w_bf, ring, tail, meta, acc, xbuf,
         xsem) = refs
    s = pl.program_id(0)

    @pl.when(s == 0)
    def _():
        w_bf[...] = w_ref[...].astype(BF16)
        ring[...] = jnp.zeros_like(ring)
        tail[...] = jnp.zeros_like(tail)
        meta[...] = jnp.zeros_like(meta)

    t_proj = jnp.minimum(s, n_tiles - 1)
    if embed is not None:
        seq, n_seq = embed
        lp = tiles_per_seq * TILE

        def token_copies(tile):
            slot = tile % 2
            return [pltpu.make_async_copy(x_hbm.at[pl.ds(dst, n)], xbuf.at[slot, pl.ds(src, n)],
                                          xsem.at[2 * slot + k])
                    for k, (src, n, dst) in enumerate(_row_pieces(tile * TILE, (tile + 1) * TILE,
                                                                  (lp, seq, n_seq)))]

        def stream_copy(tile):
            return pltpu.make_async_copy(xbuf.at[tile % 2], h0_hbm.at[pl.ds(tile * TILE, TILE)],
                                         xsem.at[4 + tile % 2])

        for tile in range(n_tiles):
            @pl.when(s == tile)
            def _(tile=tile):
                if tile == 0:
                    for c in token_copies(0):
                        c.start()
                for c in token_copies(tile):
                    c.wait()
                first_row = (tile % tiles_per_seq) * TILE
                if first_row == 0:
                    xbuf[tile % 2, 0:N_META, :] = meta_tok_ref[...]
                pad_from = N_META + seq - first_row
                if pad_from < TILE:
                    xbuf[tile % 2, max(pad_from, 0):TILE, :] = jnp.zeros(
                        (TILE - max(pad_from, 0), D_MODEL), F32)
                if tile >= 1:
                    stream_copy(tile - 1).wait()
                if tile + 1 < n_tiles:
                    for c in token_copies(tile + 1):
                        c.start()
                stream_copy(tile).start()

        @pl.when(s == n_tiles)
        def _():
            stream_copy(n_tiles - 1).wait()

        x_ref = xbuf.at[t_proj % 2]

    t_attn = jnp.maximum(s - 1, 0)
    seq_tile = t_attn % tiles_per_seq
    seq_slot = (t_attn // tiles_per_seq) % 2
    cur = (s + 1) % 2
    nxt = s % 2

    def step(attend, project, masked=False):
        if attend:
            _attend_tile(ring.at[cur], tail, meta.at[seq_slot], bias_ref, offc_ref, sink_ref,
                         og_ref, attn_ref, acc, seq_tile * TILE, masked)
            tail[...] = ring[cur, TILE - WINDOW:TILE, COL_K:QKV_WIDTH]
        if project:
            _project_tile(x_ref, g_ref, w_bf, ones_ref, qg_ref, kg_ref, ring.at[nxt], u_ref)

    middle = (s > 0) & (s < n_tiles)
    pl.when(s == 0)(functools.partial(step, False, True))
    pl.when(middle & (seq_tile != 0))(functools.partial(step, True, True))
    pl.when(middle & (seq_tile == 0))(functools.partial(step, True, True, True))
    pl.when(s == n_tiles)(functools.partial(step, True, False, (n_tiles - 1) % tiles_per_seq == 0))

    @pl.when(t_proj % tiles_per_seq == 0)
    def _():
        meta[(t_proj // tiles_per_seq) % 2] = ring[nxt, 0:N_META, COL_K:QKV_WIDTH]


def _inproj_attention(x, gain, w_all, layer, head_ones, q_gain, k_gain, bias, off_coef, sink_row,
                      out_gain, n_seq, rows, meta_tokens=None):
    n_tiles = rows // TILE
    tiles_per_seq = n_tiles // n_seq
    assert rows == n_seq * tiles_per_seq * TILE
    embed = None if meta_tokens is None else (x.shape[0] // n_seq, n_seq)
    kern = functools.partial(_inproj_attn_kernel, tiles_per_seq=tiles_per_seq, n_tiles=n_tiles,
                             embed=embed)
    fixed = lambda s: (0, 0)
    fixed3 = lambda s: (0, 0, 0)
    proj_tile = lambda s: (jnp.minimum(s, n_tiles - 1), 0)
    if embed is None:
        x_specs, x_args = [pl.BlockSpec((TILE, D_MODEL), proj_tile)], [x]
    else:
        x_specs = [pl.BlockSpec(memory_space=pl.ANY), pl.BlockSpec((N_META, D_MODEL), fixed)]
        x_args = [x, meta_tokens]
    out_specs = [pl.BlockSpec((TILE, SSM_WIDTH), proj_tile),
                 pl.BlockSpec((TILE, ATTN_WIDTH), lambda s: (jnp.maximum(s - 1, 0), 0))]
    out_shape = [jax.ShapeDtypeStruct((rows, SSM_WIDTH), F32),
                 jax.ShapeDtypeStruct((rows, ATTN_WIDTH), BF16)]
    scratch = [
        pltpu.VMEM((D_MODEL, IN_WIDTH), BF16),
        pltpu.VMEM((2, TILE, QKV_WIDTH), BF16),
        pltpu.VMEM((WINDOW, 2 * KV_WIDTH), BF16),
        pltpu.VMEM((2, N_META, 2 * KV_WIDTH), BF16),
        pltpu.VMEM((ATTN_WIDTH, BLOCK), F32),
    ]
    if embed is not None:
        out_specs.append(pl.BlockSpec(memory_space=pl.ANY))
        out_shape.append(jax.ShapeDtypeStruct((rows, D_MODEL), F32))
        scratch += [pltpu.VMEM((2, TILE, D_MODEL), F32),
                    pltpu.SemaphoreType.DMA((6,))]
    return pl.pallas_call(
        kern,
        grid=(n_tiles + 1,),
        in_specs=x_specs + [
            pl.BlockSpec((1, D_MODEL), fixed),
            pl.BlockSpec((None, D_MODEL, IN_WIDTH), lambda s: (layer, 0, 0)),
            pl.BlockSpec((KV_WIDTH, KV_WIDTH), fixed),
            pl.BlockSpec((1, KV_WIDTH), fixed),
            pl.BlockSpec((1, KV_WIDTH), fixed),
            pl.BlockSpec(bias.shape, fixed3),
            pl.BlockSpec(off_coef.shape, fixed3),
            pl.BlockSpec(sink_row.shape, fixed3),
            pl.BlockSpec((1, ATTN_WIDTH), fixed),
        ],
        out_specs=out_specs,
        out_shape=out_shape,
        scratch_shapes=scratch,
        compiler_params=pltpu.CompilerParams(
            dimension_semantics=("arbitrary",), vmem_limit_bytes=BIG_VMEM_LIMIT),
        name="inproj_attn",
    )(*x_args, gain, w_all, head_ones, q_gain, k_gain, bias, off_coef, sink_row, out_gain)


def _gelu_tanh(x):
    c = math.sqrt(2.0 / math.pi)
    return 0.5 * x * (1.0 + jnp.tanh(c * (x + 0.044715 * (x * x * x))))


def _block_transpose_stage(arrs, dist, lane_blk):
    nblk = BLOCKS_PER_VREG
    keep = (lane_blk & dist) == 0
    nxt = list(arrs)
    for i in range(nblk):
        if i & dist == 0:
            lo, hi = arrs[i], arrs[i + dist]
            nxt[i] = jnp.where(keep, lo, pltpu.roll(hi, dist * SSM_GROUP_CH, axis=1))
            nxt[i + dist] = jnp.where(keep, pltpu.roll(lo, (nblk - dist) * SSM_GROUP_CH, axis=1), hi)
    return nxt


def _pipelined_block_transpose(n_tiles, load, store, zero, lane_blk):
    def trip(r, carry):
        after1, after2 = carry
        fresh = load(jnp.minimum(r, n_tiles - 1))
        new1 = [_block_transpose_stage(g, 4, lane_blk) for g in fresh]
        new2 = [_block_transpose_stage(g, 2, lane_blk) for g in after1]
        done = [_block_transpose_stage(g, 1, lane_blk) for g in after2]
        store(jnp.maximum(r - 2, 0), done)
        return new1, new2

    groups = CHUNK // BLOCKS_PER_VREG
    init = [[zero] * BLOCKS_PER_VREG for _ in range(groups)]
    lax.fori_loop(0, n_tiles + 2, trip, (init, init), unroll=RELAYOUT_UNROLL)


def _ssm_kernel(u_ref, bb_ref, cc_ref, pw_ref, a_ref, d_ref, wout_ref, wglu_ref,
                o_ref, wout_bf_ref, wglu_bf_ref,
                lhs, yscr, m_scr, win_scr, wsw_scr, wct_scr, zin, zsw, st,
                *, n_chunks, n_batch):
    wout_bf_ref[...] = wout_ref[...].astype(BF16)
    wglu_bf_ref[...] = wglu_ref[...].astype(BF16)

    nblk = BLOCKS_PER_VREG
    sw = 2 * SSM_STATE
    rc = n_batch * n_chunks
    half = CHUNK // nblk

    def tok_rows(tile, tile_chunks, t):
        return pl.ds(tile * (tile_chunks * CHUNK) + t, tile_chunks, stride=CHUNK)

    lane256 = lax.broadcasted_iota(jnp.int32, (CHUNK, CHUNK_W), 1)
    for g in range(GROUPS_PER_STEP):
        bb, bbs, bsw, bsws = bb_ref[g, 0], bb_ref[g, 1], bb_ref[g, 2], bb_ref[g, 3]
        cc, ccs = cc_ref[g, 0], cc_ref[g, 1]
        c_pow = [pw_ref[g, 0, tau:tau + 1, :] * cc + pw_ref[g, 1, tau:tau + 1, :] * ccs
                 for tau in range(CHUNK + 1)]
        for t in range(CHUNK):
            rows = slice(t * CHUNK, (t + 1) * CHUNK)
            p_re = pw_ref[g, 0, CHUNK - 1 - t:CHUNK - t, :]
            p_im = pw_ref[g, 1, CHUNK - 1 - t:CHUNK - t, :]
            win_scr[g, rows, :] = (p_re * bb + p_im * bbs).astype(BF16)
            wsw_scr[g, rows, :] = (p_re * bsw + p_im * bsws).astype(BF16)
            wct_scr[g, rows, :] = c_pow[t + 1].astype(BF16)
        base = lax.dot_general(bb, jnp.concatenate(c_pow[:CHUNK], axis=0), (((1,), (1,)), ((), ())),
                               precision=lax.Precision.HIGHEST, preferred_element_type=F32)
        for sblk in range(CHUNK):
            row = base if sblk == 0 else jnp.where(
                lane256 >= sblk * SSM_GROUP_CH, pltpu.roll(base, sblk * SSM_GROUP_CH, axis=1), 0.0)
            m_scr[g, sblk * CHUNK:(sblk + 1) * CHUNK, :] = row.astype(BF16)

    rt_in = 16
    lane_blk_in = lax.broadcasted_iota(jnp.int32, (rt_in // 2, LANES), 1) // SSM_GROUP_CH

    def load_u(r):
        return [[pltpu.bitcast(u_ref[tok_rows(r, rt_in, hf * nblk + tb), :].astype(BF16), jnp.uint32)
                 for tb in range(nblk)] for hf in range(half)]

    def store_lhs(r, groups):
        rows = pl.ds(pl.multiple_of(r * rt_in, rt_in), rt_in)
        for hf in range(half):
            for g in range(nblk):
                col = g * CHUNK_W + hf * LANES
                lhs[rows, col:col + LANES] = pltpu.bitcast(groups[hf][g], BF16)

    _pipelined_block_transpose(rc // rt_in, load_u, store_lhs,
                               jnp.zeros((rt_in // 2, LANES), jnp.uint32), lane_blk_in)

    for g in range(GROUPS_PER_STEP):
        u_g = lhs[:, g * CHUNK_W:(g + 1) * CHUNK_W]
        zin[:, g * sw:(g + 1) * sw] = jnp.dot(u_g, win_scr[g], preferred_element_type=F32)
        zsw[:, g * sw:(g + 1) * sw] = jnp.dot(u_g, wsw_scr[g], preferred_element_type=F32)

    a1 = a_ref[0:1, :]
    a2 = a_ref[1:2, :]

    def step(c, carry):
        nxt = []
        for b in range(n_batch):
            s, x = carry[b]
            row = b * n_chunks + c
            st[pl.ds(row, 1), :] = s
            nxt.append((a1 * s + a2 * x + zin[pl.ds(row, 1), :],
                        a1 * x - a2 * s + zsw[pl.ds(row, 1), :]))
        return tuple(nxt)

    zero = jnp.zeros((1, GROUPS_PER_STEP * sw), F32)
    lax.fori_loop(0, n_chunks, step, tuple((zero, zero) for _ in range(n_batch)),
                  unroll=SCAN_UNROLL)

    for g in range(GROUPS_PER_STEP):
        cols = slice(g * CHUNK_W, (g + 1) * CHUNK_W)
        y = jnp.dot(lhs[:, cols], m_scr[g], preferred_element_type=F32)
        y = y + lax.dot_general(st[:, g * sw:(g + 1) * sw].astype(BF16), wct_scr[g],
                                (((1,), (1,)), ((), ())), preferred_element_type=F32)
        yscr[:, cols] = y.astype(BF16)

    d_row = d_ref[...]

    def load_y(r):
        rows = pl.ds(pl.multiple_of(r * rt_in, rt_in), rt_in)
        return [[pltpu.bitcast(yscr[rows, g * CHUNK_W + hf * LANES:g * CHUNK_W + (hf + 1) * LANES],
                               jnp.uint32)
                 for g in range(nblk)] for hf in range(half)]

    def store_tokens(r, groups):
        for hf in range(half):
            for tb in range(nblk):
                rows = tok_rows(r, rt_in, hf * nblk + tb)
                y = pltpu.bitcast(groups[hf][tb], BF16).astype(F32)
                o_ref[rows, :] = y + d_row * u_ref[rows, :]

    _pipelined_block_transpose(rc // rt_in, load_y, store_tokens,
                               jnp.zeros((rt_in // 2, LANES), jnp.uint32), lane_blk_in)


def _ssm(u_tok, bb, cc, pw, a16, d, n_batch, w_out_all, w_glu_all, layer):
    rows, width = u_tok.shape
    rc = rows // CHUNK
    n_chunks = rc // n_batch
    assert GROUPS_PER_STEP == BLOCKS_PER_VREG and rc % 16 == 0
    gs = GROUPS_PER_STEP
    bw = gs * CHUNK_W
    sw = gs * 2 * SSM_STATE
    kern = functools.partial(_ssm_kernel, n_chunks=n_chunks, n_batch=n_batch)
    grp4 = lambda j: (j, 0, 0, 0)
    steps = SSM_GROUPS // gs
    wo_rows, wg_rows = D_MODEL // steps, SSM_WIDTH // steps
    return pl.pallas_call(
        kern,
        grid=(steps,),
        in_specs=[
            pl.BlockSpec((rows, LANES), lambda j: (0, j)),
            pl.BlockSpec((gs,) + bb.shape[1:], grp4),
            pl.BlockSpec((gs,) + cc.shape[1:], grp4),
            pl.BlockSpec((gs,) + pw.shape[1:], grp4),
            pl.BlockSpec((2, sw), lambda j: (0, j)),
            pl.BlockSpec((1, LANES), lambda j: (0, j)),
            pl.BlockSpec((None, wo_rows, D_MODEL), lambda j: (layer, j, 0)),
            pl.BlockSpec((None, wg_rows, SSM_WIDTH), lambda j: (layer, j, 0)),
        ],
        out_specs=[pl.BlockSpec((rows, LANES), lambda j: (0, j)),
                   pl.BlockSpec((wo_rows, D_MODEL), lambda j: (j, 0)),
                   pl.BlockSpec((wg_rows, SSM_WIDTH), lambda j: (j, 0))],
        out_shape=[jax.ShapeDtypeStruct((rows, width), F32),
                   jax.ShapeDtypeStruct((D_MODEL, D_MODEL), BF16),
                   jax.ShapeDtypeStruct((SSM_WIDTH, SSM_WIDTH), BF16)],
        scratch_shapes=[
            pltpu.VMEM((rc, bw), BF16),
            pltpu.VMEM((rc, bw), BF16),
            pltpu.VMEM((gs, CHUNK_W, CHUNK_W), BF16),
            pltpu.VMEM((gs, CHUNK_W, 2 * SSM_STATE), BF16),
            pltpu.VMEM((gs, CHUNK_W, 2 * SSM_STATE), BF16),
            pltpu.VMEM((gs, CHUNK_W, 2 * SSM_STATE), BF16),
            pltpu.VMEM((rc, sw), F32),
            pltpu.VMEM((rc, sw), F32),
            pltpu.VMEM((rc, sw), F32),
        ],
        compiler_params=pltpu.CompilerParams(
            dimension_semantics=("parallel",), vmem_limit_bytes=VMEM_LIMIT),
        name="ssm",
    )(u_tok, bb, cc, pw, a16, d, w_out_all, w_glu_all)


def _ssm_tables(lam_re, lam_im, log_step, b_re, b_im, c_re, c_im):
    depth = lam_re.shape[0]
    g, p, h, t = SSM_GROUPS, SSM_STATE, SSM_GROUP_CH, CHUNK
    lam = lax.complex(lam_re.astype(F32), lam_im.astype(F32))
    delta = jnp.exp(log_step.astype(F32))[..., None]
    ld = lam * delta
    lam_bar = jnp.exp(ld)
    b_bar = ((lam_bar - 1.0) / lam)[..., None] * lax.complex(b_re.astype(F32), b_im.astype(F32))
    c_c = lax.complex(c_re.astype(F32), c_im.astype(F32))
    pows = [jnp.ones_like(lam_bar), lam_bar]
    for _ in range(t - 1):
        pows.append(pows[-1] * lam_bar)
    powers = jnp.stack(pows, axis=2)

    cat = lambda x, y: jnp.concatenate([x, y], axis=-1)
    br, bi = jnp.real(b_bar).swapaxes(-1, -2), jnp.imag(b_bar).swapaxes(-1, -2)
    bb = jnp.stack([cat(br, bi), cat(-bi, br), cat(bi, br), cat(br, -bi)], axis=2)
    cr, ci = jnp.real(c_c), jnp.imag(c_c)
    cc = jnp.stack([cat(cr, -ci), cat(-ci, -cr)], axis=2)
    pr, pi = jnp.real(powers), jnp.imag(powers)
    pw = jnp.stack([cat(pr, pr), cat(pi, pi)], axis=2)
    pw = jnp.pad(pw, ((0, 0), (0, 0), (0, 0), (0, 24 - (t + 1)), (0, 0)))
    ar, ai = pr[:, :, t], pi[:, :, t]
    a16 = jnp.stack([cat(ar, ar).reshape(depth, g * 2 * p),
                     cat(-ai, ai).reshape(depth, g * 2 * p)], axis=1)
    return bb, cc, pw, a16


def _outproj_kernel(a_ref, g_ref, x_ref, wglu_ref, bglu_ref, sg_ref, wout_ref, ng_ref,
                    o_ref, xn_ref):
    g = _gelu_tanh(g_ref[...])
    z = jnp.dot(g.astype(BF16), wglu_ref[...], preferred_element_type=F32) + bglu_ref[...]
    s = g * jax.nn.sigmoid(z)
    s_n = _rms(s, sg_ref[...]).astype(BF16)
    y = jnp.dot(a_ref[...], wout_ref[:ATTN_WIDTH, :], preferred_element_type=F32)
    y = y + jnp.dot(s_n, wout_ref[ATTN_WIDTH:, :], preferred_element_type=F32)
    h = x_ref[...] + y
    o_ref[...] = h
    xn_ref[...] = _rms(h, ng_ref[...]).astype(xn_ref.dtype)


def _outproj(attn_n, g_tok, x, w_glu_bf, b_glu, ssm_gain, w_out_bf, mlp_gain):
    rows = x.shape[0]
    tm = _row_tile(rows, OUTPROJ_ROWS)
    row = lambda i: (i, 0)
    fixed = lambda i: (0, 0)
    return pl.pallas_call(
        _outproj_kernel,
        grid=(rows // tm,),
        in_specs=[
            pl.BlockSpec((tm, ATTN_WIDTH), row),
            pl.BlockSpec((tm, SSM_WIDTH), row),
            pl.BlockSpec((tm, D_MODEL), row),
            pl.BlockSpec((SSM_WIDTH, SSM_WIDTH), fixed),
            pl.BlockSpec((1, SSM_WIDTH), fixed),
            pl.BlockSpec((1, SSM_WIDTH), fixed),
            pl.BlockSpec((D_MODEL, D_MODEL), fixed),
            pl.BlockSpec((1, D_MODEL), fixed),
        ],
        out_specs=[pl.BlockSpec((tm, D_MODEL), row), pl.BlockSpec((tm, D_MODEL), row)],
        out_shape=[jax.ShapeDtypeStruct((rows, D_MODEL), F32),
                   jax.ShapeDtypeStruct((rows, D_MODEL), BF16)],
        compiler_params=pltpu.CompilerParams(
            dimension_semantics=("parallel",), vmem_limit_bytes=VMEM_LIMIT),
        name="outproj",
    )(attn_n, g_tok, x, w_glu_bf, b_glu, ssm_gain, w_out_bf, mlp_gain)


def _mlp_tile(xn, w_up, w_down, o_ref, rows):
    a = jnp.dot(xn, w_up, preferred_element_type=F32)
    a = jnp.square(jnp.maximum(a, 0.0)).astype(BF16)
    o_ref[rows, :] += jnp.dot(a, w_down, preferred_element_type=F32)


def _mlp_first_kernel(xn_ref, x_hbm, wu_ref, wd_ref, o_ref, wu_bf_ref, wd_bf_ref, sem, *, tm):
    @pl.when(pl.program_id(0) == 0)
    def _():
        copy = pltpu.make_async_copy(x_hbm.at[pl.ds(0, o_ref.shape[0])], o_ref, sem)
        copy.start()
        copy.wait()

    wu_bf_ref[...] = wu_ref[...].astype(BF16)
    wd_bf_ref[...] = wd_ref[...].astype(BF16)
    for t in range(o_ref.shape[0] // tm):
        rows = slice(t * tm, (t + 1) * tm)
        _mlp_tile(xn_ref[rows, :], wu_bf_ref[...], wd_bf_ref[...], o_ref, rows)


def _row_pieces(lo, hi, tokens):
    if tokens is None:
        return [(0, hi - lo, lo)]
    lp, seq, n_seq = tokens
    pieces = []
    for b in range(n_seq):
        first, last = max(lo, b * lp + N_META), min(hi, b * lp + N_META + seq)
        if last > first:
            pieces.append((first - lo, last - first, b * seq + first - (b * lp + N_META)))
    return pieces


def _mlp_rest_kernel(xn_ref, x_ref, wu_ref, wd_ref, first_hbm, out_hbm, acc_even, acc_odd, sem,
                     *, first_pieces, tile_pieces):
    i = pl.program_id(0)
    f = pl.program_id(1)
    n_tiles = len(tile_pieces)
    last_f = pl.num_programs(1) - 1
    width = max(len(p) for p in tile_pieces)

    first_copies = [pltpu.make_async_copy(first_hbm.at[pl.ds(src, n)], out_hbm.at[pl.ds(dst, n)],
                                          sem.at[k])
                    for k, (src, n, dst) in enumerate(first_pieces)]

    def tile_copies(tile, acc):
        base = len(first_pieces) + (tile % 2) * width
        return [pltpu.make_async_copy(acc.at[pl.ds(src, n)], out_hbm.at[pl.ds(dst, n)],
                                      sem.at[base + k])
                for k, (src, n, dst) in enumerate(tile_pieces[tile])]

    @pl.when((i == 0) & (f == 0))
    def _():
        for c in first_copies:
            c.start()

    for parity, acc in ((0, acc_even), (1, acc_odd)):
        @pl.when(i % 2 == parity)
        def _(parity=parity, acc=acc):
            @pl.when(f == 0)
            def _():
                for tile in range(parity + 2, n_tiles, 2):
                    @pl.when(i == tile)
                    def _(tile=tile):
                        for c in tile_copies(tile - 2, acc):
                            c.wait()
                acc[...] = x_ref[...]

            _mlp_tile(xn_ref[...], wu_ref[...], wd_ref[...], acc, slice(None))

            for tile in range(parity, n_tiles, 2):
                @pl.when((f == last_f) & (i == tile))
                def _(tile=tile):
                    for c in tile_copies(tile, acc):
                        c.start()

    @pl.when((i == n_tiles - 1) & (f == last_f))
    def _():
        for tile in range(max(n_tiles - 2, 0), n_tiles):
            for c in tile_copies(tile, acc_odd if tile % 2 else acc_even):
                c.wait()
        for c in first_copies:
            c.wait()


def _mlp(xn, x, w_up_all, w_down_all, layer, tokens=None):
    rows = x.shape[0]
    tm = _row_tile(rows, MLP_ROWS)
    skip = min(MLP_TILES_FIRST, rows // tm)
    tm_first = skip * tm
    tf_first, tf = MLP_FF_TILE_FIRST, MLP_FF_TILE
    first_rows = lambda f: (0, 0)
    first, w_up_bf, w_down_bf = pl.pallas_call(
        functools.partial(_mlp_first_kernel, tm=tm),
        grid=(D_FF // tf_first,),
        in_specs=[
            pl.BlockSpec((tm_first, D_MODEL), first_rows),
            pl.BlockSpec(memory_space=pl.ANY),
            pl.BlockSpec((None, D_MODEL, tf_first), lambda f: (layer, 0, f)),
            pl.BlockSpec((None, tf_first, D_MODEL), lambda f: (layer, f, 0)),
        ],
        out_specs=[pl.BlockSpec((tm_first, D_MODEL), first_rows),
                   pl.BlockSpec((D_MODEL, tf_first), lambda f: (0, f)),
                   pl.BlockSpec((tf_first, D_MODEL), lambda f: (f, 0))],
        out_shape=[jax.ShapeDtypeStruct((tm_first, D_MODEL), F32),
                   jax.ShapeDtypeStruct((D_MODEL, D_FF), BF16),
                   jax.ShapeDtypeStruct((D_FF, D_MODEL), BF16)],
        scratch_shapes=[pltpu.SemaphoreType.DMA(())],
        compiler_params=pltpu.CompilerParams(
            dimension_semantics=("arbitrary",), vmem_limit_bytes=BIG_VMEM_LIMIT),
        name="mlp_first",
    )(xn, x, w_up_all, w_down_all)
    n_rest = rows // tm - skip
    if n_rest == 0:
        if tokens is None:
            return first
        lp, seq, n_seq = tokens
        return first.reshape(n_seq, lp, D_MODEL)[:, N_META:N_META + seq].reshape(n_seq * seq, D_MODEL)
    first_pieces = _row_pieces(0, tm_first, tokens)
    tile_pieces = [_row_pieces((skip + t) * tm, (skip + t + 1) * tm, tokens) for t in range(n_rest)]
    out_rows = rows if tokens is None else tokens[1] * tokens[2]
    n_sems = len(first_pieces) + 2 * max(len(p) for p in tile_pieces)
    return pl.pallas_call(
        functools.partial(_mlp_rest_kernel, first_pieces=first_pieces, tile_pieces=tile_pieces),
        grid=(n_rest, D_FF // tf),
        in_specs=[
            pl.BlockSpec((tm, D_MODEL), lambda i, f: (i + skip, 0)),
            pl.BlockSpec((tm, D_MODEL), lambda i, f: (i + skip, 0)),
            pl.BlockSpec((D_MODEL, tf), lambda i, f: (0, f)),
            pl.BlockSpec((tf, D_MODEL), lambda i, f: (f, 0)),
            pl.BlockSpec(memory_space=pl.ANY),
        ],
        out_specs=pl.BlockSpec(memory_space=pl.ANY),
        out_shape=jax.ShapeDtypeStruct((out_rows, D_MODEL), F32),
        scratch_shapes=[pltpu.VMEM((tm, D_MODEL), F32), pltpu.VMEM((tm, D_MODEL), F32),
                        pltpu.SemaphoreType.DMA((n_sems,))],
        compiler_params=pltpu.CompilerParams(
            dimension_semantics=("arbitrary", "arbitrary"), vmem_limit_bytes=BIG_VMEM_LIMIT),
        name="mlp_rest",
    )(xn, x, w_up_bf, w_down_bf, first)


def kernel(x, meta_tokens, norm_mix_g, w_in, q_norm_g, k_norm_g, attn_sinks, ssm_lambda_re, ssm_lambda_im, ssm_log_step, ssm_b_re, ssm_b_im, ssm_c_re, ssm_c_im, ssm_d, w_glu, b_glu, attn_out_g, ssm_out_g, w_out, norm_mlp_g, w_up, w_down):
    b, seq, d = x.shape
    depth = w_in.shape[0]
    assert d == D_MODEL
    length = N_META + seq
    lp = -(-length // TILE) * TILE
    rows = b * lp

    bb, cc, pw, a16 = _ssm_tables(
        ssm_lambda_re, ssm_lambda_im, ssm_log_step, ssm_b_re, ssm_b_im, ssm_c_re, ssm_c_im)

    head_id = jnp.arange(KV_WIDTH) // HEAD_DIM
    head_ones = (head_id[:, None] == head_id[None, :]).astype(BF16)
    slopes = jnp.exp2(-8.0 * jnp.arange(1, N_HEADS + 1, dtype=F32) / N_HEADS)
    attn_bias, attn_off = _attn_bias(slopes)

    row_vec = lambda v: v.astype(F32).reshape(1, -1)
    for l in range(depth):
        q_gain = jnp.tile(row_vec(q_norm_g[l]), (1, N_KV_HEADS)) * (LOG2E / math.sqrt(HEAD_DIM))
        k_gain = jnp.tile(row_vec(k_norm_g[l]), (1, N_KV_HEADS))
        sink_row = jnp.broadcast_to(
            (attn_sinks[l].astype(F32) * LOG2E).reshape(N_KV_HEADS, 1, KV_GROUP, 1),
            (N_KV_HEADS, 1, KV_GROUP, SUB)).reshape(N_KV_HEADS, 1, KV_GROUP * SUB)
        shared = (row_vec(norm_mix_g[l]), w_in, l, head_ones, q_gain, k_gain, attn_bias, attn_off,
                  sink_row, row_vec(attn_out_g[l]), b, rows)
        if l == 0:
            u_tok, attn_n, h_res = _inproj_attention(
                x.reshape(b * seq, d), *shared, meta_tokens=meta_tokens.astype(F32))
        else:
            u_tok, attn_n = _inproj_attention(h_res, *shared)
        g_tok, w_out_bf, w_glu_bf = _ssm(u_tok, bb[l], cc[l], pw[l], a16[l], row_vec(ssm_d[l]), b,
                                         w_out, w_glu, l)
        h_res, xn = _outproj(attn_n, g_tok, h_res, w_glu_bf,
                             row_vec(b_glu[l]), row_vec(ssm_out_g[l]), w_out_bf,
                             row_vec(norm_mlp_g[l]))
        h_res = _mlp(xn, h_res, w_up, w_down, l, (lp, seq, b) if l == depth - 1 else None)
    return h_res.reshape(b, seq, d)
```
